```python
import math
import jax, jax.numpy as jnp
from jax import lax
import numpy as np

D_MODEL = 1024
BATCH = 2
SEQ = 16384
DEPTH = 4

GRID_W = 64
Q_BLOCK = 128
N_BRANCHES = 4
BRANCH_WIDTH = 256
A_Q_HEADS = 4
A_KV_HEADS = 2
A_HEAD_DIM = 64
ROPE_THETA = 10000.0
SSM_HEADS = 4
SSM_HEAD_DIM = 64
SSM_GROUPS = 2
SSM_STATE = 128
SSM_CONV = 5
SSM_CHUNK = 128
DIFF_HEADS = 4
DIFF_QK_DIM = 32
DIFF_V_DIM = 64
REL_BUCKETS = 32
REL_MAX_DIST = 128
SGU_GROUPS = 4
SGU_GROUP_DIM = 64
SGU_CHUNK = 128
N_EXPERTS = 16
EC_CAPACITY = 2
D_FF_EXPERT = 1024
PLE_DIM = 256
EPS = 1e-6

IN_SIZES = (
    A_Q_HEADS * A_HEAD_DIM, A_KV_HEADS * A_HEAD_DIM, A_KV_HEADS * A_HEAD_DIM,
    SSM_HEADS * SSM_HEAD_DIM, SSM_HEADS * SSM_HEAD_DIM,
    SSM_GROUPS * SSM_STATE, SSM_GROUPS * SSM_STATE, SSM_HEADS, SSM_HEADS,
    DIFF_HEADS * 2 * DIFF_QK_DIM, DIFF_HEADS * 2 * DIFF_QK_DIM, DIFF_HEADS * DIFF_V_DIM,
    2 * SGU_GROUPS * SGU_GROUP_DIM,
)
IN_WIDTH = sum(IN_SIZES)
CONV_CH = SSM_HEADS * SSM_HEAD_DIM + 2 * SSM_GROUPS * SSM_STATE

kernel_name = 'hybrid_gated_parallel_encoder'


def rms_norm(x, g):
    xf = x.astype(jnp.float32)
    y = xf * lax.rsqrt(jnp.mean(xf * xf, axis=-1, keepdims=True) + EPS)
    return (y * g.astype(jnp.float32)).astype(x.dtype)


def rope_1d(xh, pos):
    n = xh.shape[-1]
    freqs = ROPE_THETA ** (-jnp.arange(0, n, 2, dtype=jnp.float32) / n)
    ang = pos.astype(jnp.float32)[:, None] * freqs[None, :]
    cos = jnp.cos(ang)[None, :, None, :]
    sin = jnp.sin(ang)[None, :, None, :]
    x1 = xh[..., : n // 2].astype(jnp.float32)
    x2 = xh[..., n // 2:].astype(jnp.float32)
    return jnp.concatenate([x1 * cos - x2 * sin, x1 * sin + x2 * cos], axis=-1).astype(xh.dtype)


def axial_rope(x, row, col):
    half = x.shape[-1] // 2
    return jnp.concatenate([rope_1d(x[..., :half], row), rope_1d(x[..., half:], col)], axis=-1)


def gqa_block_attention(q, k, v):
    b, s, hq, d = q.shape
    hkv = k.shape[2]
    nblk = s // Q_BLOCK
    scale = d ** -0.5
    qb = jnp.moveaxis(q.reshape(b, nblk, Q_BLOCK, hkv, hq // hkv, d), 1, 0)

    def one_block(qi):
        logits = jnp.einsum('bqgrd,bkgd->bgrqk', qi, k).astype(jnp.float32) * scale
        probs = jax.nn.softmax(logits, axis=-1).astype(v.dtype)
        return jnp.einsum('bgrqk,bkgd->bqgrd', probs, v)

    out = lax.map(one_block, qb)
    return jnp.moveaxis(out, 0, 1).reshape(b, s, hq * d)


def t5_bucket(rel):
    nb = REL_BUCKETS // 2
    max_exact = nb // 2
    ret = jnp.where(rel > 0, nb, 0)
    r = jnp.abs(rel)
    rf = jnp.maximum(r, 1).astype(jnp.float32)
    large = max_exact + (jnp.log(rf / max_exact) / math.log(REL_MAX_DIST / max_exact)
                         * (nb - max_exact)).astype(jnp.int32)
    large = jnp.minimum(large, nb - 1)
    return ret + jnp.where(r < max_exact, r, large)


def diff_block_attention(q1, q2, k1, k2, v, lam, rel_bias):
    b, s, h, d = q1.shape
    nblk = s // Q_BLOCK
    scale = d ** -0.5
    kpos = jnp.arange(s, dtype=jnp.int32)
    starts = jnp.arange(nblk, dtype=jnp.int32) * Q_BLOCK

    def split_blocks(t):
        return jnp.moveaxis(t.reshape(b, nblk, Q_BLOCK, h, d), 1, 0)

    def one_block(args):
        q1i, q2i, start = args
        qpos = start + jnp.arange(Q_BLOCK, dtype=jnp.int32)
        bias = jnp.transpose(rel_bias[t5_bucket(kpos[None, :] - qpos[:, None])], (2, 0, 1)).astype(jnp.float32)
        p1 = jax.nn.softmax(jnp.einsum('bqhd,bkhd->bhqk', q1i, k1).astype(jnp.float32) * scale + bias, axis=-1)
        p2 = jax.nn.softmax(jnp.einsum('bqhd,bkhd->bhqk', q2i, k2).astype(jnp.float32) * scale + bias, axis=-1)
        w = (p1 - lam * p2).astype(v.dtype)
        return jnp.einsum('bhqk,bkhd->bqhd', w, v)

    out = lax.map(one_block, (split_blocks(q1), split_blocks(q2), starts))
    return jnp.moveaxis(out, 0, 1).reshape(b, s, h, v.shape[-1])


def depthwise_conv_centred(x, w, bias):
    k, c = w.shape
    y = lax.conv_general_dilated(x, w[:, None, :].astype(x.dtype), window_strides=(1,),
                                 padding=[((k - 1) // 2, k // 2)],
                                 dimension_numbers=('NWC', 'WIO', 'NWC'), feature_group_count=c)
    return y + bias.astype(x.dtype)


def ssd_chunked(x, dt, a, bm, cm):
    dtype = x.dtype
    f32 = jnp.float32
    b, s, h, pdim = x.shape
    nc = s // SSM_CHUNK
    rep = h // bm.shape[2]
    xc = (x.astype(f32) * dt[..., None]).reshape(b, nc, SSM_CHUNK, h, pdim)
    bc = jnp.repeat(bm.astype(f32), rep, axis=2).reshape(b, nc, SSM_CHUNK, h, -1)
    cc = jnp.repeat(cm.astype(f32), rep, axis=2).reshape(b, nc, SSM_CHUNK, h, -1)
    acs = jnp.cumsum((dt * a[None, None, :]).reshape(b, nc, SSM_CHUNK, h), axis=2)
    causal = jnp.tril(jnp.ones((SSM_CHUNK, SSM_CHUNK), dtype=bool))
    seg = acs[:, :, :, None, :] - acs[:, :, None, :, :]
    decay_in = jnp.exp(jnp.where(causal[None, None, :, :, None], seg, -jnp.inf))
    scores = jnp.einsum('bclhn,bcshn->bclsh', cc, bc) * decay_in
    y_diag = jnp.einsum('bclsh,bcshp->bclhp', scores, xc)
    decay_to_end = jnp.exp(acs[:, :, -1:, :] - acs)
    chunk_states = jnp.einsum('bclhn,bclh,bclhp->bchpn', bc, decay_to_end, xc)
    chunk_decay = jnp.exp(acs[:, :, -1, :])

    def carry_state(state, inp):
        cs, cd = inp
        return state * cd[:, :, None, None] + cs, state

    init = jnp.zeros((b, h, pdim, bc.shape[-1]), f32)
    _, prev = lax.scan(carry_state, init, (jnp.moveaxis(chunk_states, 1, 0), jnp.moveaxis(chunk_decay, 1, 0)))
    prev = jnp.moveaxis(prev, 0, 1)
    y_off = jnp.einsum('bclhn,bchpn,bclh->bclhp', cc, prev, jnp.exp(acs))
    return (y_diag + y_off).reshape(b, s, h, pdim).astype(dtype)


def bidir_ssd(z, xs, bm, cm, dt_f, dt_b, conv_w, conv_b, dtb_f, dtb_b, alog_f, alog_b, d_skip, g_norm):
    f32 = jnp.float32
    b, s, _ = xs.shape
    d_x = SSM_HEADS * SSM_HEAD_DIM
    d_bc = SSM_GROUPS * SSM_STATE
    xbc = jax.nn.silu(depthwise_conv_centred(jnp.concatenate([xs, bm, cm], axis=-1), conv_w, conv_b))
    xh = xbc[..., :d_x].reshape(b, s, SSM_HEADS, SSM_HEAD_DIM)
    bg = xbc[..., d_x:d_x + d_bc].reshape(b, s, SSM_GROUPS, SSM_STATE)
    cg = xbc[..., d_x + d_bc:].reshape(b, s, SSM_GROUPS, SSM_STATE)
    delta_f = jax.nn.softplus(dt_f.astype(f32) + dtb_f.astype(f32))
    delta_b = jax.nn.softplus(dt_b.astype(f32) + dtb_b.astype(f32))
    y_f = ssd_chunked(xh, delta_f, -jnp.exp(alog_f.astype(f32)), bg, cg)
    y_b = jnp.flip(ssd_chunked(jnp.flip(xh, 1), jnp.flip(delta_b, 1), -jnp.exp(alog_b.astype(f32)),
                               jnp.flip(bg, 1), jnp.flip(cg, 1)), 1)
    y = y_f + y_b + d_skip[:, None].astype(xh.dtype) * xh
    y = y.reshape(b, s, d_x) * jax.nn.silu(z)
    return rms_norm(y, g_norm)


def spatial_gating(uv, g_norm, w_s, b_s):
    b, s, _ = uv.shape
    u, v = jnp.split(jax.nn.gelu(uv), 2, axis=-1)
    v = rms_norm(v, g_norm).reshape(b, s // SGU_CHUNK, SGU_CHUNK, SGU_GROUPS, SGU_GROUP_DIM)
    sv = jnp.einsum('gts,bcsgd->bctgd', w_s.astype(v.dtype), v) + b_s.T[None, None, :, :, None].astype(v.dtype)
    return u * sv.reshape(b, s, SGU_GROUPS * SGU_GROUP_DIM)


def expert_choice_moe(h, w_router, w_gate_e, w_up_e, w_down_e):
    b, s, _ = h.shape
    cap = EC_CAPACITY * s // N_EXPERTS
    affinity = jax.nn.softmax((h @ w_router).astype(jnp.float32), axis=-1)
    top_aff, idx = lax.top_k(jnp.swapaxes(affinity, 1, 2), cap)
    bidx = jnp.arange(b)[:, None, None]
    xe = h[bidx, idx]
    hid = jax.nn.silu(jnp.einsum('becd,edf->becf', xe, w_gate_e)) * jnp.einsum('becd,edf->becf', xe, w_up_e)
    ye = jnp.einsum('becf,efd->becd', hid, w_down_e)
    return jnp.zeros_like(h).at[bidx, idx].add(ye * top_aff[..., None].astype(ye.dtype))


def setup_inputs(seed: int = 0) -> dict:
    key = jax.random.key(seed)
    keys = list(jax.random.split(key, 40))

    def nrm(shape, scale):
        return jax.random.normal(keys.pop(), shape, jnp.float32) * scale

    def gain(shape):
        return 1.0 + nrm(shape, 0.05)

    def dt_bias(shape):
        u = jax.random.uniform(keys.pop(), shape, jnp.float32)
        dt = jnp.exp(math.log(1e-3) + u * (math.log(1e-1) - math.log(1e-3)))
        return dt + jnp.log(-jnp.expm1(-dt))

    def a_log(shape):
        return jnp.log(jax.random.uniform(keys.pop(), shape, jnp.float32, 1.0, 16.0))

    L = DEPTH
    return {
        'x': nrm((BATCH, SEQ, D_MODEL), 1.0),
        'p': nrm((DEPTH, BATCH, SEQ, PLE_DIM), 1.0),
        'rel_bias': nrm((REL_BUCKETS, DIFF_HEADS), 0.5),
        'g_mix': gain((L, D_MODEL)),
        'w_in': nrm((L, D_MODEL, IN_WIDTH), D_MODEL ** -0.5),
        'g_qnorm': gain((L, A_HEAD_DIM)),
        'g_knorm': gain((L, A_HEAD_DIM)),
        'conv_w': nrm((L, SSM_CONV, CONV_CH), SSM_CONV ** -0.5),
        'conv_b': nrm((L, CONV_CH), 0.02),
        'dt_bias_f': dt_bias((L, SSM_HEADS)),
        'dt_bias_b': dt_bias((L, SSM_HEADS)),
        'a_log_f': a_log((L, SSM_HEADS)),
        'a_log_b': a_log((L, SSM_HEADS)),
        'd_skip': gain((L, SSM_HEADS)),
        'g_ssm': gain((L, SSM_HEADS * SSM_HEAD_DIM)),
        'lambda_q1': nrm((L, DIFF_QK_DIM), 0.1),
        'lambda_k1': nrm((L, DIFF_QK_DIM), 0.1),
        'lambda_q2': nrm((L, DIFF_QK_DIM), 0.1),
        'lambda_k2': nrm((L, DIFF_QK_DIM), 0.1),
        'g_diff': gain((L, DIFF_V_DIM)),
        'g_sgu': gain((L, SGU_GROUPS * SGU_GROUP_DIM)),
        'w_spatial': nrm((L, SGU_GROUPS, SGU_CHUNK, SGU_CHUNK), SGU_CHUNK ** -0.5),
        'b_spatial': gain((L, SGU_GROUPS, SGU_CHUNK)),
        'w_branch': nrm((L, N_BRANCHES, BRANCH_WIDTH, D_MODEL), BRANCH_WIDTH ** -0.5),
        'w_branch_gate': nrm((L, D_MODEL, N_BRANCHES * D_MODEL), D_MODEL ** -0.5),
        'w_out': nrm((L, D_MODEL, D_MODEL), D_MODEL ** -0.5),
        'g_moe': gain((L, D_MODEL)),
        'w_router': nrm((L, D_MODEL, N_EXPERTS), D_MODEL ** -0.5),
        'w_exp_gate': nrm((L, N_EXPERTS, D_MODEL, D_FF_EXPERT), D_MODEL ** -0.5),
        'w_exp_up': nrm((L, N_EXPERTS, D_MODEL, D_FF_EXPERT), D_MODEL ** -0.5),
        'w_exp_down': nrm((L, N_EXPERTS, D_FF_EXPERT, D_MODEL), D_FF_EXPERT ** -0.5),
        'g_ple': gain((L, D_MODEL)),
        'w_ple_gate': nrm((L, D_MODEL, D_MODEL), D_MODEL ** -0.5),
        'w_ple': nrm((L, PLE_DIM, D_MODEL), PLE_DIM ** -0.5),
        'g_final': gain((D_MODEL,)),
    }


def reference(x, p, rel_bias, g_mix, w_in, g_qnorm, g_knorm, conv_w, conv_b, dt_bias_f, dt_bias_b,
              a_log_f, a_log_b, d_skip, g_ssm, lambda_q1, lambda_k1, lambda_q2, lambda_k2, g_diff,
              g_sgu, w_spatial, b_spatial, w_branch, w_branch_gate, w_out, g_moe, w_router,
              w_exp_gate, w_exp_up, w_exp_down, g_ple, w_ple_gate, w_ple, g_final):
    f32 = jnp.float32
    b, s, _ = x.shape
    rows = s // GRID_W
    row = jnp.repeat(jnp.arange(rows, dtype=jnp.int32), GRID_W)
    col = jnp.tile(jnp.arange(GRID_W, dtype=jnp.int32), rows)
    offsets = [int(o) for o in np.cumsum(IN_SIZES)[:-1]]

    for i in range(DEPTH):
        h = rms_norm(x, g_mix[i])
        (a_q, a_k, a_v, s_z, s_x, s_b, s_c, s_dtf, s_dtb,
         c_q, c_k, c_v, d_uv) = jnp.split(h @ w_in[i], offsets, axis=-1)

        q = axial_rope(rms_norm(a_q.reshape(b, s, A_Q_HEADS, A_HEAD_DIM), g_qnorm[i]), row, col)
        k = axial_rope(rms_norm(a_k.reshape(b, s, A_KV_HEADS, A_HEAD_DIM), g_knorm[i]), row, col)
        o_a = gqa_block_attention(q, k, a_v.reshape(b, s, A_KV_HEADS, A_HEAD_DIM))

        o_b = bidir_ssd(s_z, s_x, s_b, s_c, s_dtf, s_dtb, conv_w[i], conv_b[i], dt_bias_f[i], dt_bias_b[i],
                        a_log_f[i], a_log_b[i], d_skip[i], g_ssm[i])

        lam_init = 0.8 - 0.6 * math.exp(-0.3 * i)
        lam = (jnp.exp(jnp.sum(lambda_q1[i].astype(f32) * lambda_k1[i].astype(f32)))
               - jnp.exp(jnp.sum(lambda_q2[i].astype(f32) * lambda_k2[i].astype(f32))) + lam_init)
        cq = c_q.reshape(b, s, DIFF_HEADS, 2, DIFF_QK_DIM)
        ck = c_k.reshape(b, s, DIFF_HEADS, 2, DIFF_QK_DIM)
        o_c = diff_block_attention(cq[:, :, :, 0], cq[:, :, :, 1], ck[:, :, :, 0], ck[:, :, :, 1],
                                   c_v.reshape(b, s, DIFF_HEADS, DIFF_V_DIM), lam, rel_bias)
        o_c = (rms_norm(o_c, g_diff[i]) * (1.0 - lam_init)).reshape(b, s, DIFF_HEADS * DIFF_V_DIM)

        o_d = spatial_gating(d_uv, g_sgu[i], w_spatial[i], b_spatial[i])

        branches = jnp.stack([o_a, o_b, o_c, o_d], axis=2)
        gates = jax.nn.sigmoid(h @ w_branch_gate[i]).reshape(b, s, N_BRANCHES, D_MODEL)
        merged = jnp.sum(gates * jnp.einsum('bsnc,ncd->bsnd', branches, w_branch[i]), axis=2)
        x = x + merged @ w_out[i]

        x = x + expert_choice_moe(rms_norm(x, g_moe[i]), w_router[i], w_exp_gate[i], w_exp_up[i], w_exp_down[i])

        ple_gate = jax.nn.sigmoid(rms_norm(x, g_ple[i]) @ w_ple_gate[i])
        x = x + ple_gate * (p[i] @ w_ple[i])

    return rms_norm(x, g_final)
```

```python
import functools
import math

import jax
import jax.numpy as jnp
from jax import lax
from jax.experimental import pallas as pl
from jax.experimental.pallas import tpu as pltpu

F32 = jnp.float32
BF16 = jnp.bfloat16
HIGHEST = lax.Precision.HIGHEST

EPS = 1e-6
GRID_W = 64
A_Q_HEADS, A_KV_HEADS, A_HEAD_DIM = 4, 2, 64
ROPE_THETA = 10000.0
SSM_HEADS, SSM_HEAD_DIM, SSM_GROUPS, SSM_STATE, SSM_CONV, SSM_CHUNK = 4, 64, 2, 128, 5, 128
DIFF_HEADS, DIFF_QK_DIM, DIFF_V_DIM = 4, 32, 64
REL_BUCKETS, REL_MAX_DIST = 32, 128
SGU_GROUPS, SGU_GROUP_DIM, SGU_CHUNK = 4, 64, 128
N_EXPERTS, EC_CAPACITY = 16, 2
N_BRANCHES, BRANCH_WIDTH = 4, 256

V7X_LANES = 128
V7X_SUBLANES = 8
V7X_VMEM_LIMIT_BYTES = 56 * 1024 * 1024

NEG_BIG = -1e30


def _cparams(sem):
    return pltpu.CompilerParams(dimension_semantics=sem, vmem_limit_bytes=V7X_VMEM_LIMIT_BYTES)


def _rms(x, g):
    return x * lax.rsqrt(jnp.mean(x * x, axis=-1, keepdims=True) + EPS) * g


def _dot(a, b):
    return jnp.dot(a, b, preferred_element_type=F32)


def _dot_nt(a, b):
    return lax.dot_general(a, b, (((1,), (1,)), ((), ())), preferred_element_type=F32)


def _dot_tn(a, b):
    return lax.dot_general(a, b, (((0,), (0,)), ((), ())), preferred_element_type=F32)


def _full(shape):
    n = len(shape)
    return pl.BlockSpec(shape, lambda *_: (0,) * n)


def _inproj_kernel(x_ref, gmix_ref, wa_ref, wb_ref, wc_ref, wd_ref, rc_ref, rs1_ref, rs2_ref,
                   gq_ref, gk_ref, bd_ref, gsgu_ref,
                   qa_ref, ka_ref, va_ref, z_ref, xbc_ref, dt_ref, cq_ref, ck_ref, cv_ref,
                   du_ref, dv_ref):
    hb = _rms(x_ref[...], gmix_ref[...]).astype(BF16)

    a = _dot(hb, wa_ref[...])
    nq = A_Q_HEADS * A_HEAD_DIM
    nk = A_KV_HEADS * A_HEAD_DIM
    q, k, v = a[:, :nq], a[:, nq:nq + nk], a[:, nq + nk:]
    bd = bd_ref[...]
    q = q * lax.rsqrt(jnp.dot(q * q, bd, precision=HIGHEST, preferred_element_type=F32) + EPS) * gq_ref[...]
    k = k * lax.rsqrt(jnp.dot(k * k, bd[:nk, :nk], precision=HIGHEST, preferred_element_type=F32) + EPS) * gk_ref[...]
    rc, rs1, rs2 = rc_ref[...], rs1_ref[...], rs2_ref[...]
    quarter = A_HEAD_DIM // 4

    def rope(t, c, s1, s2):
        w = t.shape[1]
        return t * c + pltpu.roll(t, w - quarter, 1) * s1 + pltpu.roll(t, quarter, 1) * s2

    k = rope(k, rc, rs1, rs2)
    q = rope(q, jnp.concatenate([rc, rc], axis=1), jnp.concatenate([rs1, rs1], axis=1),
             jnp.concatenate([rs2, rs2], axis=1))
    qa_ref[...] = (q * (A_HEAD_DIM ** -0.5)).astype(BF16)
    ka_ref[...] = k.astype(BF16)
    va_ref[...] = v.astype(BF16)

    b = _dot(hb, wb_ref[...])
    dx = SSM_HEADS * SSM_HEAD_DIM
    conv_ch = dx + 2 * SSM_GROUPS * SSM_STATE
    z_ref[...] = b[:, :dx]
    xbc_ref[...] = b[:, dx:dx + conv_ch]
    dt_ref[...] = b[:, dx + conv_ch:]

    c = _dot(hb, wc_ref[...])
    nqc = DIFF_HEADS * 2 * DIFF_QK_DIM
    cq_ref[...] = (c[:, :nqc] * (DIFF_QK_DIM ** -0.5)).astype(BF16)
    ck_ref[...] = c[:, nqc:2 * nqc].astype(BF16)
    cv_ref[...] = c[:, 2 * nqc:].astype(BF16)

    d = jax.nn.gelu(_dot(hb, wd_ref[...]))
    w = SGU_GROUPS * SGU_GROUP_DIM
    du_ref[...] = d[:, :w]
    dv_ref[...] = _rms(d[:, w:], gsgu_ref[...]).astype(BF16)


def _inproj(x, gmix, wa, wb, wc, wd, rc, rs1, rs2, gq, gk, bd, gsgu, s, tm):
    t, dm = x.shape
    nst = s // tm
    row = lambda w: pl.BlockSpec((tm, w), lambda i: (i, 0))
    tab = pl.BlockSpec((tm, rc.shape[1]), lambda i: (i % nst, 0))
    outs = [(256, BF16), (128, BF16), (128, BF16), (256, F32), (768, F32), (128, F32),
            (256, BF16), (256, BF16), (256, BF16), (256, F32), (256, BF16)]
    return pl.pallas_call(
        _inproj_kernel,
        grid=(t // tm,),
        in_specs=[row(dm), _full(gmix.shape), _full(wa.shape), _full(wb.shape), _full(wc.shape),
                  _full(wd.shape), tab, tab, tab, _full(gq.shape), _full(gk.shape), _full(bd.shape),
                  _full(gsgu.shape)],
        out_specs=[row(w) for w, _ in outs],
        out_shape=[jax.ShapeDtypeStruct((t, w), dt) for w, dt in outs],
        compiler_params=_cparams(("parallel",)),
        name="inproj",
    )(x, gmix, wa, wb, wc, wd, rc, rs1, rs2, gq, gk, bd, gsgu)


def _attn_a_kernel(q_ref, k_ref, v_ref, o_ref, m_ref, l_ref, acc_ref):
    kj = pl.program_id(3)

    @pl.when(kj == 0)
    def _():
        m_ref[...] = jnp.full(m_ref.shape, NEG_BIG, F32)
        l_ref[...] = jnp.zeros(l_ref.shape, F32)
        acc_ref[...] = jnp.zeros(acc_ref.shape, F32)

    s = _dot_nt(q_ref[...], k_ref[...])
    m_old = m_ref[...]
    m_new = jnp.maximum(m_old, jnp.max(s, axis=1, keepdims=True))
    alpha = jnp.exp(m_old - m_new)
    p = jnp.exp(s - m_new)
    l_ref[...] = alpha * l_ref[...] + jnp.sum(p, axis=1, keepdims=True)
    acc_ref[...] = alpha * acc_ref[...] + _dot(p.astype(BF16), v_ref[...])
    m_ref[...] = m_new

    @pl.when(kj == pl.num_programs(3) - 1)
    def _():
        o_ref[...] = (acc_ref[...] / l_ref[...]).astype(o_ref.dtype)


def _attn_a(q, k, v, tq, tk):
    b, hq, s, d = q.shape
    rep = hq // k.shape[1]
    return pl.pallas_call(
        _attn_a_kernel,
        grid=(b, hq, s // tq, s // tk),
        in_specs=[pl.BlockSpec((None, None, tq, d), lambda bi, h, i, j: (bi, h, i, 0)),
                  pl.BlockSpec((None, None, tk, d), lambda bi, h, i, j: (bi, h // rep, j, 0)),
                  pl.BlockSpec((None, None, tk, d), lambda bi, h, i, j: (bi, h // rep, j, 0))],
        out_specs=pl.BlockSpec((None, None, tq, d), lambda bi, h, i, j: (bi, h, i, 0)),
        out_shape=jax.ShapeDtypeStruct(q.shape, BF16),
        scratch_shapes=[pltpu.VMEM((tq, 1), F32), pltpu.VMEM((tq, 1), F32), pltpu.VMEM((tq, d), F32)],
        compiler_params=_cparams(("parallel", "parallel", "parallel", "arbitrary")),
        name="attn_a",
    )(q, k, v)


def _attn_c_kernel(nband_lo, q_ref, k_ref, v_ref, gband_ref, cfar_ref, lam_ref, gd_ref, o_ref,
                   m_ref, l_ref, acc_ref, *, out_scale):
    i = pl.program_id(2)
    j = pl.program_id(3)
    tq = q_ref.shape[0]
    tk = k_ref.shape[0]
    dq = DIFF_QK_DIM
    rep = tq // tk
    nband = gband_ref.shape[0]

    @pl.when(j == 0)
    def _():
        m_ref[...] = jnp.full(m_ref.shape, NEG_BIG, F32)
        l_ref[...] = jnp.zeros(l_ref.shape, F32)
        acc_ref[...] = jnp.zeros(acc_ref.shape, F32)

    q = q_ref[...]
    k = k_ref[...]
    v = v_ref[...]
    d2 = j - rep * i + nband_lo

    def update(bias):
        ps = []
        for c in range(2):
            s = _dot_nt(q[:, c * dq:(c + 1) * dq], k[:, c * dq:(c + 1) * dq]) + bias
            m_old = m_ref[c]
            m_new = jnp.maximum(m_old, jnp.max(s, axis=1, keepdims=True))
            alpha = jnp.exp(m_old - m_new)
            p = jnp.exp(s - m_new)
            l_ref[c] = alpha * l_ref[c] + jnp.sum(p, axis=1, keepdims=True)
            acc_ref[c] = alpha * acc_ref[c]
            m_ref[c] = m_new
            ps.append(p.astype(BF16))
        pv = _dot(jnp.concatenate(ps, axis=0), v)
        acc_ref[0] += pv[:tq]
        acc_ref[1] += pv[tq:]

    in_band = jnp.logical_and(d2 >= 0, d2 < nband)

    @pl.when(in_band)
    def _():
        g = gband_ref[jnp.clip(d2, 0, nband - 1)]
        w = g.shape[1]
        t = pltpu.roll(jnp.broadcast_to(g, (tq, w)), tk + 1, 1, stride=1, stride_axis=0)
        update(t[:, :tk])

    @pl.when(jnp.logical_not(in_band))
    def _():
        cf = cfar_ref[...]
        update(jnp.where(d2 < 0, cf[:, 0:1], cf[:, 1:2]))

    @pl.when(j == pl.num_programs(3) - 1)
    def _():
        o = acc_ref[0] / l_ref[0] - lam_ref[...] * (acc_ref[1] / l_ref[1])
        o_ref[...] = (_rms(o, gd_ref[...]) * out_scale).astype(o_ref.dtype)


def _attn_c(q, k, v, gband, cfar, lam, gd, nband_lo, out_scale, tq, tk):
    b, h, s, dqk2 = q.shape
    dv = v.shape[-1]
    nband = gband.shape[1]
    kern = functools.partial(_attn_c_kernel, nband_lo, out_scale=out_scale)
    return pl.pallas_call(
        kern,
        grid=(b, h, s // tq, s // tk),
        in_specs=[pl.BlockSpec((None, None, tq, dqk2), lambda bi, hi, i, j: (bi, hi, i, 0)),
                  pl.BlockSpec((None, None, tk, dqk2), lambda bi, hi, i, j: (bi, hi, j, 0)),
                  pl.BlockSpec((None, None, tk, dv), lambda bi, hi, i, j: (bi, hi, j, 0)),
                  pl.BlockSpec((None, nband, 1, tq + tk), lambda bi, hi, i, j: (hi, 0, 0, 0)),
                  pl.BlockSpec((None, 1, 2), lambda bi, hi, i, j: (hi, 0, 0)),
                  _full(lam.shape), _full(gd.shape)],
        out_specs=pl.BlockSpec((None, None, tq, dv), lambda bi, hi, i, j: (bi, hi, i, 0)),
        out_shape=jax.ShapeDtypeStruct((b, h, s, dv), BF16),
        scratch_shapes=[pltpu.VMEM((2, tq, 1), F32), pltpu.VMEM((2, tq, 1), F32),
                        pltpu.VMEM((2, tq, dv), F32)],
        compiler_params=_cparams(("parallel", "parallel", "parallel", "arbitrary")),
        name="attn_c",
    )(q, k, v, gband, cfar, lam, gd)


def _ssd_prep_kernel(nst, xc_ref, xp_ref, xn_ref, dt_ref, cw_ref, cb_ref, dtb_ref, xa_ref, dl_ref, ext_ref):
    i = pl.program_id(0)
    tm = xc_ref.shape[0]
    halo = V7X_SUBLANES
    pos = i % nst
    keep_prev = (pos != 0).astype(F32)
    keep_next = (pos != nst - 1).astype(F32)
    ext_ref[0:halo, :] = xp_ref[...] * keep_prev
    ext_ref[halo:halo + tm, :] = xc_ref[...]
    ext_ref[halo + tm:, :] = xn_ref[...] * keep_next
    pad_l = (SSM_CONV - 1) // 2
    cw = cw_ref[...]
    acc = jnp.zeros(xc_ref.shape, F32) + cb_ref[...]
    for t in range(SSM_CONV):
        acc = acc + ext_ref[halo - pad_l + t:halo - pad_l + t + tm, :] * cw[t:t + 1, :]
    xa_ref[...] = jax.nn.silu(acc)
    dl_ref[...] = jax.nn.softplus(dt_ref[...] + dtb_ref[...])


def _ssd_prep(xbc, dt, cw, cb, dtb, s, tm):
    t, c = xbc.shape
    nst = s // tm
    hb = tm // V7X_SUBLANES
    nhb = t // V7X_SUBLANES
    return pl.pallas_call(
        functools.partial(_ssd_prep_kernel, nst),
        grid=(t // tm,),
        in_specs=[pl.BlockSpec((tm, c), lambda i: (i, 0)),
                  pl.BlockSpec((V7X_SUBLANES, c), lambda i: (jnp.maximum(i * hb - 1, 0), 0)),
                  pl.BlockSpec((V7X_SUBLANES, c), lambda i: (jnp.minimum((i + 1) * hb, nhb - 1), 0)),
                  pl.BlockSpec((tm, dt.shape[1]), lambda i: (i, 0)),
                  _full(cw.shape), _full(cb.shape), _full(dtb.shape)],
        out_specs=[pl.BlockSpec((tm, c), lambda i: (i, 0)),
                   pl.BlockSpec((tm, dt.shape[1]), lambda i: (i, 0))],
        out_shape=[jax.ShapeDtypeStruct((t, c), F32), jax.ShapeDtypeStruct(dt.shape, F32)],
        scratch_shapes=[pltpu.VMEM((tm + 2 * V7X_SUBLANES, c), F32)],
        compiler_params=_cparams(("parallel",)),
        name="ssd_prep",
    )(xbc, xbc, xbc, dt, cw, cb, dtb)


def _ssd_chunk(reverse, xa, dl, alog, st_ref):
    ln = SSM_CHUNK
    row = lax.broadcasted_iota(jnp.int32, (ln, ln), 0)
    col = lax.broadcasted_iota(jnp.int32, (ln, ln), 1)
    tri = (col >= row) if reverse else (col <= row)
    eye = row == col
    last = 0 if reverse else ln - 1
    lane0 = SSM_HEADS if reverse else 0
    a_neg = -jnp.exp(alog)
    acs = jnp.dot(tri.astype(F32), dl * a_neg, precision=HIGHEST, preferred_element_type=F32)
    dx = SSM_HEADS * SSM_HEAD_DIM
    gw = SSM_STATE
    hpg = SSM_HEADS // SSM_GROUPS
    ys = []
    for g in range(SSM_GROUPS):
        bg = xa[:, dx + g * gw:dx + (g + 1) * gw].astype(BF16)
        cg = xa[:, dx + SSM_GROUPS * gw + g * gw:dx + SSM_GROUPS * gw + (g + 1) * gw].astype(BF16)
        cb = _dot_nt(cg, bg)
        for r in range(hpg):
            h = g * hpg + r
            ln_h = lane0 + h
            ac = acs[:, ln_h:ln_h + 1]
            ar = jnp.sum(jnp.where(eye, ac, 0.0), axis=0, keepdims=True)
            dec = jnp.exp(jnp.where(tri, ac - ar, NEG_BIG))
            xdt = xa[:, h * SSM_HEAD_DIM:(h + 1) * SSM_HEAD_DIM] * dl[:, ln_h:ln_h + 1]
            st = st_ref[h]
            y = _dot((cb * dec).astype(BF16), xdt.astype(BF16))
            y = y + _dot(cg, st.astype(BF16)) * jnp.exp(ac)
            a_last = acs[last:last + 1, ln_h:ln_h + 1]
            st_ref[h] = st * jnp.exp(a_last) + _dot_tn(bg, (xdt * jnp.exp(a_last - ac)).astype(BF16))
            ys.append(y)
    return jnp.concatenate(ys, axis=1)


def _ssd_fwd_kernel(cps, xa_ref, dl_ref, alog_ref, y_ref, st_ref):
    @pl.when(pl.program_id(1) == 0)
    def _():
        st_ref[...] = jnp.zeros(st_ref.shape, F32)

    for c in range(cps):
        sl = slice(c * SSM_CHUNK, (c + 1) * SSM_CHUNK)
        y_ref[sl, :] = _ssd_chunk(False, xa_ref[sl, :], dl_ref[sl, :], alog_ref[...], st_ref)


def _ssd_bwd_kernel(cps, xa_ref, dl_ref, alog_ref, yf_ref, z_ref, dsk_ref, g_ref, o_ref, st_ref):
    @pl.when(pl.program_id(1) == 0)
    def _():
        st_ref[...] = jnp.zeros(st_ref.shape, F32)

    dx = SSM_HEADS * SSM_HEAD_DIM
    for c in reversed(range(cps)):
        sl = slice(c * SSM_CHUNK, (c + 1) * SSM_CHUNK)
        xa = xa_ref[sl, :]
        yb = _ssd_chunk(True, xa, dl_ref[sl, :], alog_ref[...], st_ref)
        y = yf_ref[sl, :] + yb + dsk_ref[...] * xa[:, :dx]
        y = y * jax.nn.silu(z_ref[sl, :])
        o_ref[sl, :] = _rms(y, g_ref[...]).astype(o_ref.dtype)


def _ssd(xa, dl, alog, z, dsk, gssm, b, s, cps):
    t, c = xa.shape
    tm = cps * SSM_CHUNK
    nblk = s // tm
    dx = SSM_HEADS * SSM_HEAD_DIM
    st = pltpu.VMEM((SSM_HEADS, SSM_STATE, SSM_HEAD_DIM), F32)
    fw = lambda w: pl.BlockSpec((tm, w), lambda bi, ci: (bi * nblk + ci, 0))
    bw = lambda w: pl.BlockSpec((tm, w), lambda bi, ci: (bi * nblk + nblk - 1 - ci, 0))
    yf = pl.pallas_call(
        functools.partial(_ssd_fwd_kernel, cps),
        grid=(b, nblk),
        in_specs=[fw(c), fw(dl.shape[1]), _full(alog.shape)],
        out_specs=fw(dx),
        out_shape=jax.ShapeDtypeStruct((t, dx), F32),
        scratch_shapes=[st],
        compiler_params=_cparams(("parallel", "arbitrary")),
        name="ssd_fwd",
    )(xa, dl, alog)
    return pl.pallas_call(
        functools.partial(_ssd_bwd_kernel, cps),
        grid=(b, nblk),
        in_specs=[bw(c), bw(dl.shape[1]), _full(alog.shape), bw(dx), bw(dx), _full(dsk.shape),
                  _full(gssm.shape)],
        out_specs=bw(dx),
        out_shape=jax.ShapeDtypeStruct((t, dx), BF16),
        scratch_shapes=[st],
        compiler_params=_cparams(("parallel", "arbitrary")),
        name="ssd_bwd",
    )(xa, dl, alog, yf, z, dsk, gssm)


def _sgu_kernel(nck, u_ref, v_ref, w_ref, bias_ref, o_ref):
    for c in range(nck):
        sl = slice(c * SGU_CHUNK, (c + 1) * SGU_CHUNK)
        v = v_ref[sl, :]
        sv = [_dot(w_ref[g], v[:, g * SGU_GROUP_DIM:(g + 1) * SGU_GROUP_DIM]) for g in range(SGU_GROUPS)]
        o_ref[sl, :] = (u_ref[sl, :] * (jnp.concatenate(sv, axis=1) + bias_ref[...])).astype(o_ref.dtype)


def _sgu(u, v, w, bias, nck):
    t, c = u.shape
    tm = nck * SGU_CHUNK
    return pl.pallas_call(
        functools.partial(_sgu_kernel, nck),
        grid=(t // tm,),
        in_specs=[pl.BlockSpec((tm, c), lambda i: (i, 0)), pl.BlockSpec((tm, c), lambda i: (i, 0)),
                  _full(w.shape), _full(bias.shape)],
        out_specs=pl.BlockSpec((tm, c), lambda i: (i, 0)),
        out_shape=jax.ShapeDtypeStruct((t, c), BF16),
        compiler_params=_cparams(("parallel",)),
        name="sgu",
    )(u, v, w, bias)


def _merge_kernel(x_ref, gmix_ref, oa_ref, ob_ref, oc_ref, od_ref, wg_ref, wbr_ref, wo_ref, o_ref):
    x = x_ref[...]
    dm = x.shape[1]
    hb = _rms(x, gmix_ref[...]).astype(BF16)
    merged = jnp.zeros(x.shape, F32)
    for n, br in enumerate((oa_ref, ob_ref, oc_ref, od_ref)):
        gate = jax.nn.sigmoid(_dot(hb, wg_ref[:, n * dm:(n + 1) * dm]))
        merged = merged + gate * _dot(br[...], wbr_ref[n])
    o_ref[...] = x + _dot(merged.astype(BF16), wo_ref[...])


def _merge(x, gmix, oa, ob, oc, od, wg, wbr, wo, tm):
    t, dm = x.shape
    row = lambda w: pl.BlockSpec((tm, w), lambda i: (i, 0))
    return pl.pallas_call(
        _merge_kernel,
        grid=(t // tm,),
        in_specs=[row(dm), _full(gmix.shape), row(256), row(256), row(256), row(256),
                  _full(wg.shape), _full(wbr.shape), _full(wo.shape)],
        out_specs=row(dm),
        out_shape=jax.ShapeDtypeStruct((t, dm), F32),
        compiler_params=_cparams(("parallel",)),
        name="merge",
    )(x, gmix, oa, ob, oc, od, wg, wbr, wo)


def _router_kernel(x_ref, g_ref, wrt_ref, h_ref, aff_ref):
    h = _rms(x_ref[...], g_ref[...])
    h_ref[...] = h.astype(BF16)
    logits = lax.dot_general(wrt_ref[...], h, (((1,), (1,)), ((), ())), precision=HIGHEST,
                             preferred_element_type=F32)
    e = jnp.exp(logits - jnp.max(logits, axis=0, keepdims=True))
    aff_ref[...] = e / jnp.sum(e, axis=0, keepdims=True)


def _router(x, g, wrt, b, s, tm):
    t, dm = x.shape
    ne = wrt.shape[0]
    nst = s // tm
    return pl.pallas_call(
        _router_kernel,
        grid=(t // tm,),
        in_specs=[pl.BlockSpec((tm, dm), lambda i: (i, 0)), _full(g.shape), _full(wrt.shape)],
        out_specs=[pl.BlockSpec((tm, dm), lambda i: (i, 0)),
                   pl.BlockSpec((None, ne, tm), lambda i: (i // nst, 0, i % nst))],
        out_shape=[jax.ShapeDtypeStruct((t, dm), BF16), jax.ShapeDtypeStruct((b, ne, s), F32)],
        compiler_params=_cparams(("parallel",)),
        name="router",
    )(x, g, wrt)


def _select_kernel(cap, aff_ref, gate_ref):
    ne, s = aff_ref.shape
    bits = pltpu.bitcast(aff_ref[...], jnp.int32)
    capf = jnp.float32(cap)

    def bisect(i, thr):
        cand = thr | lax.shift_left(jnp.int32(1), 30 - i)
        cnt = jnp.sum((bits >= cand).astype(F32), axis=1, keepdims=True)
        return jnp.where(cnt >= capf, cand, thr)

    thr = lax.fori_loop(0, 31, bisect, jnp.zeros((ne, 1), jnp.int32))
    need = capf - jnp.sum((bits > thr).astype(F32), axis=1, keepdims=True)
    lanes = V7X_LANES
    r = lax.broadcasted_iota(jnp.int32, (lanes, lanes), 0)
    c = lax.broadcasted_iota(jnp.int32, (lanes, lanes), 1)
    before = (r < c).astype(BF16)

    def block(jb, carry):
        off = pl.multiple_of(jb * lanes, lanes)
        a = aff_ref[:, pl.ds(off, lanes)]
        bb = pltpu.bitcast(a, jnp.int32)
        eq = bb == thr
        eqf = eq.astype(F32)
        rank = carry + _dot(eqf.astype(BF16), before)
        keep = jnp.logical_or(bb > thr, jnp.logical_and(eq, rank < need))
        gate_ref[:, pl.ds(off, lanes)] = jnp.where(keep, a, 0.0)
        return carry + jnp.sum(eqf, axis=1, keepdims=True)

    lax.fori_loop(0, s // lanes, block, jnp.zeros((ne, 1), F32))


def _select(aff, cap):
    b, ne, s = aff.shape
    return pl.pallas_call(
        functools.partial(_select_kernel, cap),
        grid=(b,),
        in_specs=[pl.BlockSpec((None, ne, s), lambda i: (i, 0, 0))],
        out_specs=pl.BlockSpec((None, ne, s), lambda i: (i, 0, 0)),
        out_shape=jax.ShapeDtypeStruct(aff.shape, F32),
        compiler_params=_cparams(("parallel",)),
        name="moe_select",
    )(aff)


def _moe_kernel(x_ref, h_ref, gates_ref, wg_ref, wu_ref, wd_ref, o_ref):
    e = pl.program_id(1)

    @pl.when(e == 0)
    def _():
        o_ref[...] = x_ref[...]

    h = h_ref[...]
    lane = lax.broadcasted_iota(jnp.int32, gates_ref.shape, 1)
    gcol = jnp.sum(jnp.where(lane == e, gates_ref[...], 0.0), axis=1, keepdims=True)
    hid = jax.nn.silu(_dot(h, wg_ref[...])) * _dot(h, wu_ref[...])
    o_ref[...] += _dot(hid.astype(BF16), wd_ref[...]) * gcol


def _moe(x, h, gates, wg, wu, wd, tm):
    t, dm = x.shape
    ne, _, dff = wg.shape
    return pl.pallas_call(
        _moe_kernel,
        grid=(t // tm, ne),
        in_specs=[pl.BlockSpec((tm, dm), lambda i, e: (i, 0)),
                  pl.BlockSpec((tm, dm), lambda i, e: (i, 0)),
                  pl.BlockSpec((tm, ne), lambda i, e: (i, 0)),
                  pl.BlockSpec((None, dm, dff), lambda i, e: (e, 0, 0)),
                  pl.BlockSpec((None, dm, dff), lambda i, e: (e, 0, 0)),
                  pl.BlockSpec((None, dff, dm), lambda i, e: (e, 0, 0))],
        out_specs=pl.BlockSpec((tm, dm), lambda i, e: (i, 0)),
        out_shape=jax.ShapeDtypeStruct((t, dm), F32),
        compiler_params=_cparams(("parallel", "arbitrary")),
        name="moe_ffn",
    )(x, h, gates, wg, wu, wd)


def _ple_kernel(final, x_ref, p_ref, g_ref, wpg_ref, wp_ref, gf_ref, o_ref):
    x = x_ref[...]
    gate = jax.nn.sigmoid(_dot(_rms(x, g_ref[...]).astype(BF16), wpg_ref[...]))
    y = x + gate * _dot(p_ref[...].astype(BF16), wp_ref[...])
    o_ref[...] = _rms(y, gf_ref[...]) if final else y


def _ple(x, p, g, wpg, wp, gf, final, tm):
    t, dm = x.shape
    return pl.pallas_call(
        functools.partial(_ple_kernel, final),
        grid=(t // tm,),
        in_specs=[pl.BlockSpec((tm, dm), lambda i: (i, 0)), pl.BlockSpec((tm, p.shape[1]), lambda i: (i, 0)),
                  _full(g.shape), _full(wpg.shape), _full(wp.shape), _full(gf.shape)],
        out_specs=pl.BlockSpec((tm, dm), lambda i: (i, 0)),
        out_shape=jax.ShapeDtypeStruct((t, dm), F32),
        compiler_params=_cparams(("parallel",)),
        name="ple",
    )(x, p, g, wpg, wp, gf)


def _rope_tables(s):
    quarter = A_HEAD_DIM // 4
    half = A_HEAD_DIM // 2
    pos = jnp.arange(s, dtype=jnp.int32)
    freqs = ROPE_THETA ** (-jnp.arange(0, half, 2, dtype=F32) / half)
    ang_r = (pos // GRID_W).astype(F32)[:, None] * freqs[None, :]
    ang_c = (pos % GRID_W).astype(F32)[:, None] * freqs[None, :]
    zero = jnp.zeros((s, quarter), F32)
    cos = jnp.concatenate([jnp.cos(ang_r)] * 2 + [jnp.cos(ang_c)] * 2, axis=1)
    s1 = jnp.concatenate([-jnp.sin(ang_r), zero, -jnp.sin(ang_c), zero], axis=1)
    s2 = jnp.concatenate([zero, jnp.sin(ang_r), zero, jnp.sin(ang_c)], axis=1)
    two = lambda a: jnp.concatenate([a, a], axis=1)
    return two(cos), two(s1), two(s2)


def _t5_bucket(rel):
    nb = REL_BUCKETS // 2
    max_exact = nb // 2
    ret = jnp.where(rel > 0, nb, 0)
    r = jnp.abs(rel)
    rf = jnp.maximum(r, 1).astype(F32)
    large = max_exact + (jnp.log(rf / max_exact) / math.log(REL_MAX_DIST / max_exact)
                         * (nb - max_exact)).astype(jnp.int32)
    large = jnp.minimum(large, nb - 1)
    return ret + jnp.where(r < max_exact, r, large)


def _bias_tables(rel_bias, tq, tk):
    assert tk >= REL_MAX_DIST and tq % tk == 0
    rep = tq // tk
    offs = jnp.arange(-1, rep + 1, dtype=jnp.int32) * tk
    u = jnp.arange(tq + tk, dtype=jnp.int32)
    rel = offs[:, None] - (tq - 1) + u[None, :]
    gband = jnp.transpose(rel_bias[_t5_bucket(rel)], (2, 0, 1))[:, :, None, :].astype(F32)
    far = jnp.array([-REL_MAX_DIST, REL_MAX_DIST], dtype=jnp.int32)
    cfar = jnp.transpose(rel_bias[_t5_bucket(far)], (1, 0))[:, None, :].astype(F32)
    return gband, cfar, 1


def _tile(n, pref):
    t = min(n, pref)
    assert n % t == 0
    return t


def _heads(a, b, s, h):
    return jnp.transpose(a.reshape(b, s, h, a.shape[1] // h), (0, 2, 1, 3))


def _unheads(a):
    b, h, s, d = a.shape
    return jnp.transpose(a, (0, 2, 1, 3)).reshape(b * s, h * d)


def kernel(x, p, rel_bias, g_mix, w_in, g_qnorm, g_knorm, conv_w, conv_b, dt_bias_f, dt_bias_b, a_log_f, a_log_b, d_skip, g_ssm, lambda_q1, lambda_k1, lambda_q2, lambda_k2, g_diff, g_sgu, w_spatial, b_spatial, w_branch, w_branch_gate, w_out, g_moe, w_router, w_exp_gate, w_exp_up, w_exp_down, g_ple, w_ple_gate, w_ple, g_final):
    b, s, dm = x.shape
    depth = w_in.shape[0]
    t = b * s
    cap = EC_CAPACITY * s // N_EXPERTS

    tm_proj = _tile(s, 512)
    tm_prep = _tile(s, 1024)
    tq_a, tk_a = _tile(s, 1024), _tile(s, 1024)
    tq_c, tk_c = _tile(s, 1024), _tile(s, 512)
    cps = _tile(s // SSM_CHUNK, 4)
    nck = _tile(s // SGU_CHUNK, 8)
    tm_moe = _tile(t, 1024)

    rc, rs1, rs2 = _rope_tables(s)
    gband, cfar, nband_lo = _bias_tables(rel_bias, tq_c, tk_c)
    hd = A_HEAD_DIM
    bd = (jnp.arange(A_Q_HEADS * hd)[:, None] // hd == jnp.arange(A_Q_HEADS * hd)[None, :] // hd).astype(F32) / hd

    sizes = (A_Q_HEADS * hd, A_KV_HEADS * hd, A_KV_HEADS * hd,
             SSM_HEADS * SSM_HEAD_DIM, SSM_HEADS * SSM_HEAD_DIM, SSM_GROUPS * SSM_STATE,
             SSM_GROUPS * SSM_STATE, SSM_HEADS, SSM_HEADS,
             DIFF_HEADS * 2 * DIFF_QK_DIM, DIFF_HEADS * 2 * DIFF_QK_DIM, DIFF_HEADS * DIFF_V_DIM,
             2 * SGU_GROUPS * SGU_GROUP_DIM)
    off = [0]
    for sz in sizes:
        off.append(off[-1] + sz)
    o_a0, o_b0, o_dt0, o_c0, o_d0, o_end = off[0], off[3], off[7], off[9], off[12], off[13]
    dt_pad = V7X_LANES - 2 * SSM_HEADS

    def lane_row(v, width=None):
        v = v.astype(F32).reshape(1, -1)
        if width is not None and v.shape[1] < width:
            v = jnp.pad(v, ((0, 0), (0, width - v.shape[1])))
        return v

    xt = x.reshape(t, dm)
    for i in range(depth):
        wi = w_in[i]
        wa = wi[:, o_a0:o_b0].astype(BF16)
        wb = jnp.pad(wi[:, o_b0:o_c0], ((0, 0), (0, dt_pad))).astype(BF16)
        wc = wi[:, o_c0:o_d0].astype(BF16)
        wd = wi[:, o_d0:o_end].astype(BF16)
        gmix = lane_row(g_mix[i])
        (qa, ka, va, z, xbc, dt, cq, ck, cv, du, dv) = _inproj(
            xt, gmix, wa, wb, wc, wd, rc, rs1, rs2,
            lane_row(jnp.tile(g_qnorm[i], A_Q_HEADS)), lane_row(jnp.tile(g_knorm[i], A_KV_HEADS)),
            bd, lane_row(g_sgu[i]), s, tm_proj)

        o_a = _unheads(_attn_a(_heads(qa, b, s, A_Q_HEADS), _heads(ka, b, s, A_KV_HEADS),
                               _heads(va, b, s, A_KV_HEADS), tq_a, tk_a))

        xa, dl = _ssd_prep(xbc, dt, conv_w[i].astype(F32), lane_row(conv_b[i]),
                           lane_row(jnp.concatenate([dt_bias_f[i], dt_bias_b[i]]), V7X_LANES), s, tm_prep)
        o_b = _ssd(xa, dl, lane_row(jnp.concatenate([a_log_f[i], a_log_b[i]]), V7X_LANES), z,
                   lane_row(jnp.repeat(d_skip[i], SSM_HEAD_DIM)), lane_row(g_ssm[i]), b, s, cps)

        lam_init = 0.8 - 0.6 * math.exp(-0.3 * i)
        lam = (jnp.exp(jnp.sum(lambda_q1[i].astype(F32) * lambda_k1[i].astype(F32)))
               - jnp.exp(jnp.sum(lambda_q2[i].astype(F32) * lambda_k2[i].astype(F32))) + lam_init).reshape(1, 1)
        o_c = _unheads(_attn_c(_heads(cq, b, s, DIFF_HEADS), _heads(ck, b, s, DIFF_HEADS),
                               _heads(cv, b, s, DIFF_HEADS), gband, cfar, lam, lane_row(g_diff[i]),
                               nband_lo, 1.0 - lam_init, tq_c, tk_c))

        sgu_bias = jnp.repeat(b_spatial[i].T.astype(F32), SGU_GROUP_DIM, axis=1)
        o_d = _sgu(du, dv, w_spatial[i].astype(BF16), sgu_bias, nck)

        xt = _merge(xt, gmix, o_a, o_b, o_c, o_d, w_branch_gate[i].astype(BF16),
                    w_branch[i].astype(BF16), w_out[i].astype(BF16), tm_proj)

        h2, aff = _router(xt, lane_row(g_moe[i]), w_router[i].T.astype(F32), b, s, tm_proj)
        gates = _select(aff, cap)
        gates_tm = jnp.transpose(gates, (0, 2, 1)).reshape(t, N_EXPERTS)
        xt = _moe(xt, h2, gates_tm, w_exp_gate[i].astype(BF16), w_exp_up[i].astype(BF16),
                  w_exp_down[i].astype(BF16), tm_moe)

        xt = _ple(xt, p[i].reshape(t, -1), lane_row(g_ple[i]), w_ple_gate[i].astype(BF16),
                  w_ple[i].astype(BF16), lane_row(g_final), i == depth - 1, tm_proj)

    return xt.reshape(b, s, dm)
```

```python
import functools
import math

import jax
import jax.numpy as jnp
from jax import lax
from jax.experimental import pallas as pl
from jax.experimental.pallas import tpu as pltpu

F32 = jnp.float32
BF16 = jnp.bfloat16
HIGHEST = lax.Precision.HIGHEST

EPS = 1e-6
GRID_W = 64
A_Q_HEADS, A_KV_HEADS, A_HEAD_DIM = 4, 2, 64
ROPE_THETA = 10000.0
SSM_HEADS, SSM_HEAD_DIM, SSM_GROUPS, SSM_STATE, SSM_CONV, SSM_CHUNK = 4, 64, 2, 128, 5, 128
DIFF_HEADS, DIFF_QK_DIM, DIFF_V_DIM = 4, 32, 64
REL_BUCKETS, REL_MAX_DIST = 32, 128
SGU_GROUPS, SGU_GROUP_DIM, SGU_CHUNK = 4, 64, 128
N_EXPERTS, EC_CAPACITY = 16, 2
N_BRANCHES, BRANCH_WIDTH = 4, 256

V7X_LANES = 128
V7X_SUBLANES = 8
BF16_SUBLANE_PACK = 16
V7X_VMEM_LIMIT_BYTES = 56 * 1024 * 1024

NEG_BIG = -1e30
LOG2E = math.log2(math.e)


def _cparams(sem):
    return pltpu.CompilerParams(dimension_semantics=sem, vmem_limit_bytes=V7X_VMEM_LIMIT_BYTES)


def _rms(x, g):
    return x * lax.rsqrt(jnp.mean(x * x, axis=-1, keepdims=True) + EPS) * g


def _dot(a, b):
    return jnp.dot(a, b, preferred_element_type=F32)


def _dot_nt(a, b):
    return lax.dot_general(a, b, (((1,), (1,)), ((), ())), preferred_element_type=F32)


def _dot_tn(a, b):
    return lax.dot_general(a, b, (((0,), (0,)), ((), ())), preferred_element_type=F32)


def _full(shape):
    n = len(shape)
    return pl.BlockSpec(shape, lambda *_: (0,) * n)


def _inproj_kernel(x_ref, gmix_ref, wa_ref, wb_ref, wc_ref, wd_ref, rc_ref, rs1_ref, rs2_ref,
                   gq_ref, gk_ref, bd_ref, gsgu_ref,
                   qa_ref, ka_ref, va_ref, z_ref, xbc_ref, dt_ref, cq_ref, ck_ref, cv_ref,
                   du_ref, dv_ref):
    hb = _rms(x_ref[...], gmix_ref[...]).astype(BF16)

    a = _dot(hb, wa_ref[...])
    nq = A_Q_HEADS * A_HEAD_DIM
    nk = A_KV_HEADS * A_HEAD_DIM
    q, k, v = a[:, :nq], a[:, nq:nq + nk], a[:, nq + nk:]
    bd = bd_ref[...]
    q = q * lax.rsqrt(jnp.dot(q * q, bd, precision=HIGHEST, preferred_element_type=F32) + EPS) * gq_ref[...]
    k = k * lax.rsqrt(jnp.dot(k * k, bd[:nk, :nk], precision=HIGHEST, preferred_element_type=F32) + EPS) * gk_ref[...]
    rc, rs1, rs2 = rc_ref[...], rs1_ref[...], rs2_ref[...]
    quarter = A_HEAD_DIM // 4

    def rope(t, c, s1, s2):
        w = t.shape[1]
        return t * c + pltpu.roll(t, w - quarter, 1) * s1 + pltpu.roll(t, quarter, 1) * s2

    k = rope(k, rc, rs1, rs2)
    q = rope(q, jnp.concatenate([rc, rc], axis=1), jnp.concatenate([rs1, rs1], axis=1),
             jnp.concatenate([rs2, rs2], axis=1))
    qa_ref[...] = (q * (A_HEAD_DIM ** -0.5 * LOG2E)).astype(BF16)
    ka_ref[...] = k.astype(BF16)
    va_ref[...] = v.astype(BF16)

    b = _dot(hb, wb_ref[...])
    dx = SSM_HEADS * SSM_HEAD_DIM
    conv_ch = dx + 2 * SSM_GROUPS * SSM_STATE
    z_ref[...] = b[:, :dx]
    xbc_ref[...] = b[:, dx:dx + conv_ch]
    dt_ref[...] = b[:, dx + conv_ch:]

    c = _dot(hb, wc_ref[...])
    nqc = DIFF_HEADS * 2 * DIFF_QK_DIM
    cq_ref[...] = (c[:, :nqc] * (DIFF_QK_DIM ** -0.5 * LOG2E)).astype(BF16)
    ck_ref[...] = c[:, nqc:2 * nqc].astype(BF16)
    cv_ref[...] = c[:, 2 * nqc:].astype(BF16)

    d = jax.nn.gelu(_dot(hb, wd_ref[...]))
    w = SGU_GROUPS * SGU_GROUP_DIM
    du_ref[...] = d[:, :w]
    dv_ref[...] = _rms(d[:, w:], gsgu_ref[...]).astype(BF16)


def _inproj(x, gmix, wa, wb, wc, wd, rc, rs1, rs2, gq, gk, bd, gsgu, s, tm):
    t, dm = x.shape
    nst = s // tm
    row = lambda w: pl.BlockSpec((tm, w), lambda i: (i, 0))
    tab = pl.BlockSpec((tm, rc.shape[1]), lambda i: (i % nst, 0))
    outs = [(256, BF16), (128, BF16), (128, BF16), (256, F32), (768, F32), (128, F32),
            (256, BF16), (256, BF16), (256, BF16), (256, F32), (256, BF16)]
    return pl.pallas_call(
        _inproj_kernel,
        grid=(t // tm,),
        in_specs=[row(dm), _full(gmix.shape), _full(wa.shape), _full(wb.shape), _full(wc.shape),
                  _full(wd.shape), tab, tab, tab, _full(gq.shape), _full(gk.shape), _full(bd.shape),
                  _full(gsgu.shape)],
        out_specs=[row(w) for w, _ in outs],
        out_shape=[jax.ShapeDtypeStruct((t, w), dt) for w, dt in outs],
        compiler_params=_cparams(("parallel",)),
        name="inproj",
    )(x, gmix, wa, wb, wc, wd, rc, rs1, rs2, gq, gk, bd, gsgu)


def _osm_tile(s, shift, vt, m_ref, acc_ref, c):
    m_old = m_ref[c]
    mt = jnp.max(s, axis=0, keepdims=True)
    if shift is not None:
        mt = mt + shift
    m_new = jnp.maximum(m_old, mt)
    p = jnp.exp2(s - (m_new if shift is None else m_new - shift))
    acc_ref[c] = jnp.exp2(m_old - m_new) * acc_ref[c] + _dot(vt, p.astype(BF16))
    m_ref[c] = m_new


def _osm_init(m_ref, acc_ref):
    m_ref[...] = jnp.full(m_ref.shape, NEG_BIG, F32)
    acc_ref[...] = jnp.zeros(acc_ref.shape, F32)


def _osm_result(acc_ref, c, dv):
    acc = acc_ref[c]
    return acc[:dv] / acc[dv:dv + 1]


def _attn_group(tk, js, qs, k_of, vt_ref, s_ref, m_ref, acc_ref, bias_tile=None, shifts=None):
    offs = [pl.multiple_of(j * tk, tk) for j in js]
    for u, off in enumerate(offs):
        for c, q in enumerate(qs):
            s = _dot(k_of(c, off), q)
            s_ref[u * len(qs) + c] = s if bias_tile is None else s + bias_tile
    for u, off in enumerate(offs):
        vt = vt_ref[:, pl.ds(off, tk)]
        for c in range(len(qs)):
            _osm_tile(s_ref[u * len(qs) + c], None if shifts is None else shifts[u], vt,
                      m_ref, acc_ref, c)


def _attn_a_kernel(tk, unroll, qt_ref, k_ref, vt_ref, o_ref, m_ref, acc_ref, s_ref):
    _osm_init(m_ref, acc_ref)
    qs = (qt_ref[...],)
    k_of = lambda c, off: k_ref[pl.ds(off, tk), :]
    nk = k_ref.shape[0] // tk

    def body(t, carry):
        _attn_group(tk, [t * unroll + u for u in range(unroll)], qs, k_of, vt_ref, s_ref,
                    m_ref, acc_ref)
        return carry

    lax.fori_loop(0, nk // unroll, body, 0)
    for j in range(nk - nk % unroll, nk):
        _attn_group(tk, [j], qs, k_of, vt_ref, s_ref, m_ref, acc_ref)
    o_ref[...] = _osm_result(acc_ref, 0, o_ref.shape[0]).astype(o_ref.dtype)


def _attn_a(qt, k, vt, tq, tk, unroll):
    b, hq, d, s = qt.shape
    rep = hq // k.shape[1]
    da = vt.shape[2]
    return pl.pallas_call(
        functools.partial(_attn_a_kernel, tk, unroll),
        grid=(b, hq, s // tq),
        in_specs=[pl.BlockSpec((None, None, d, tq), lambda bi, h, i: (bi, h, 0, i)),
                  pl.BlockSpec((None, None, s, d), lambda bi, h, i: (bi, h // rep, 0, 0)),
                  pl.BlockSpec((None, None, da, s), lambda bi, h, i: (bi, h // rep, 0, 0))],
        out_specs=pl.BlockSpec((None, None, d, tq), lambda bi, h, i: (bi, h, 0, i)),
        out_shape=jax.ShapeDtypeStruct(qt.shape, BF16),
        scratch_shapes=[pltpu.VMEM((1, 1, tq), F32), pltpu.VMEM((1, da, tq), F32),
                        pltpu.VMEM((unroll, tk, tq), F32)],
        compiler_params=_cparams(("parallel", "parallel", "arbitrary")),
        name="attn_a",
    )(qt, k, vt)


def _attn_c_kernel(tk, out_scale, qt_ref, k_ref, vt_ref, gband_ref, cfar_ref, lam_ref, gd_ref, o_ref,
                   m_ref, acc_ref, s_ref):
    i = pl.program_id(2)
    tq = qt_ref.shape[2]
    nk = k_ref.shape[1] // tk
    nband = gband_ref.shape[0]
    dv = o_ref.shape[0]
    _osm_init(m_ref, acc_ref)
    qs = (qt_ref[0], qt_ref[1])
    k_of = lambda c, off: k_ref[c, pl.ds(off, tk), :]
    group = functools.partial(_attn_group, tk, qs=qs, k_of=k_of, vt_ref=vt_ref, s_ref=s_ref,
                              m_ref=m_ref, acc_ref=acc_ref)

    j_lo = (tq // tk) * i - 1
    n_left = jnp.clip(j_lo, 0, nk)
    j_hi = jnp.clip(j_lo + nband, 0, nk)
    n_far = n_left + nk - j_hi
    cf = cfar_ref[...]

    def far_tile(t):
        return jnp.where(t < n_left, t, t - n_left + j_hi), jnp.where(t < n_left, cf[:, 0:1], cf[:, 1:2])

    def pair(t, carry):
        (ja, sa), (jb, sb) = far_tile(2 * t), far_tile(2 * t + 1)
        group(js=[ja, jb], shifts=[sa, sb])
        return carry

    lax.fori_loop(0, n_far // 2, pair, 0)

    @pl.when(n_far % 2 == 1)
    def _():
        j, sh = far_tile(n_far - 1)
        group(js=[j], shifts=[sh])

    for d in range(nband):
        j = j_lo + d

        @pl.when(jnp.logical_and(j >= 0, j < nk))
        def _():
            g = gband_ref[d]
            t = pltpu.roll(jnp.broadcast_to(g, (tk, g.shape[1])), tq + 1, 1, stride=1, stride_axis=0)
            group(js=[j], bias_tile=t[:, :tq])

    o = _osm_result(acc_ref, 0, dv) - lam_ref[...] * _osm_result(acc_ref, 1, dv)
    o = o * lax.rsqrt(jnp.mean(o * o, axis=0, keepdims=True) + EPS) * gd_ref[...]
    o_ref[...] = (o * out_scale).astype(o_ref.dtype)


def _attn_c(qt, k, vt, gband, cfar, lam, gd, out_scale, tq, tk):
    b, h, _, dqk, s = qt.shape
    dva = vt.shape[2]
    dv = gd.shape[0]
    nband = gband.shape[1]
    return pl.pallas_call(
        functools.partial(_attn_c_kernel, tk, out_scale),
        grid=(b, h, s // tq),
        in_specs=[pl.BlockSpec((None, None, 2, dqk, tq), lambda bi, hi, i: (bi, hi, 0, 0, i)),
                  pl.BlockSpec((None, None, 2, s, dqk), lambda bi, hi, i: (bi, hi, 0, 0, 0)),
                  pl.BlockSpec((None, None, dva, s), lambda bi, hi, i: (bi, hi, 0, 0)),
                  pl.BlockSpec((None, nband, 1, tq + tk), lambda bi, hi, i: (hi, 0, 0, 0)),
                  pl.BlockSpec((None, 1, 2), lambda bi, hi, i: (hi, 0, 0)),
                  _full(lam.shape), _full(gd.shape)],
        out_specs=pl.BlockSpec((None, None, dv, tq), lambda bi, hi, i: (bi, hi, 0, i)),
        out_shape=jax.ShapeDtypeStruct((b, h, dv, s), BF16),
        scratch_shapes=[pltpu.VMEM((2, 1, tq), F32), pltpu.VMEM((2, dva, tq), F32),
                        pltpu.VMEM((4, tk, tq), F32)],
        compiler_params=_cparams(("parallel", "parallel", "arbitrary")),
        name="attn_c",
    )(qt, k, vt, gband, cfar, lam, gd)


def _ssd_prep_kernel(nst, xc_ref, xp_ref, xn_ref, dt_ref, cw_ref, cb_ref, dtb_ref, xa_ref, dl_ref, ext_ref):
    i = pl.program_id(0)
    tm = xc_ref.shape[0]
    halo = V7X_SUBLANES
    pos = i % nst
    keep_prev = (pos != 0).astype(F32)
    keep_next = (pos != nst - 1).astype(F32)
    ext_ref[0:halo, :] = xp_ref[...] * keep_prev
    ext_ref[halo:halo + tm, :] = xc_ref[...]
    ext_ref[halo + tm:, :] = xn_ref[...] * keep_next
    pad_l = (SSM_CONV - 1) // 2
    cw = cw_ref[...]
    acc = jnp.zeros(xc_ref.shape, F32) + cb_ref[...]
    for t in range(SSM_CONV):
        acc = acc + ext_ref[halo - pad_l + t:halo - pad_l + t + tm, :] * cw[t:t + 1, :]
    xa_ref[...] = jax.nn.silu(acc)
    dl_ref[...] = jax.nn.softplus(dt_ref[...] + dtb_ref[...])


def _ssd_prep(xbc, dt, cw, cb, dtb, s, tm):
    t, c = xbc.shape
    nst = s // tm
    hb = tm // V7X_SUBLANES
    nhb = t // V7X_SUBLANES
    return pl.pallas_call(
        functools.partial(_ssd_prep_kernel, nst),
        grid=(t // tm,),
        in_specs=[pl.BlockSpec((tm, c), lambda i: (i, 0)),
                  pl.BlockSpec((V7X_SUBLANES, c), lambda i: (jnp.maximum(i * hb - 1, 0), 0)),
                  pl.BlockSpec((V7X_SUBLANES, c), lambda i: (jnp.minimum((i + 1) * hb, nhb - 1), 0)),
                  pl.BlockSpec((tm, dt.shape[1]), lambda i: (i, 0)),
                  _full(cw.shape), _full(cb.shape), _full(dtb.shape)],
        out_specs=[pl.BlockSpec((tm, c), lambda i: (i, 0)),
                   pl.BlockSpec((tm, dt.shape[1]), lambda i: (i, 0))],
        out_shape=[jax.ShapeDtypeStruct((t, c), F32), jax.ShapeDtypeStruct(dt.shape, F32)],
        scratch_shapes=[pltpu.VMEM((tm + 2 * V7X_SUBLANES, c), F32)],
        compiler_params=_cparams(("parallel",)),
        name="ssd_prep",
    )(xbc, xbc, xbc, dt, cw, cb, dtb)


def _ssd_chunk(reverse, xa, dl, alog, st_ref):
    ln = SSM_CHUNK
    row = lax.broadcasted_iota(jnp.int32, (ln, ln), 0)
    col = lax.broadcasted_iota(jnp.int32, (ln, ln), 1)
    tri = (col >= row) if reverse else (col <= row)
    eye = row == col
    last = 0 if reverse else ln - 1
    lane0 = SSM_HEADS if reverse else 0
    a_neg = -jnp.exp(alog)
    acs = jnp.dot(tri.astype(F32), dl * a_neg, precision=HIGHEST, preferred_element_type=F32)
    dx = SSM_HEADS * SSM_HEAD_DIM
    gw = SSM_STATE
    hpg = SSM_HEADS // SSM_GROUPS
    ys = []
    for g in range(SSM_GROUPS):
        bg = xa[:, dx + g * gw:dx + (g + 1) * gw].astype(BF16)
        cg = xa[:, dx + SSM_GROUPS * gw + g * gw:dx + SSM_GROUPS * gw + (g + 1) * gw].astype(BF16)
        cb = _dot_nt(cg, bg)
        for r in range(hpg):
            h = g * hpg + r
            ln_h = lane0 + h
            ac = acs[:, ln_h:ln_h + 1]
            ar = jnp.sum(jnp.where(eye, ac, 0.0), axis=0, keepdims=True)
            dec = jnp.exp(jnp.where(tri, ac - ar, NEG_BIG))
            xdt = xa[:, h * SSM_HEAD_DIM:(h + 1) * SSM_HEAD_DIM] * dl[:, ln_h:ln_h + 1]
            st = st_ref[h]
            y = _dot((cb * dec).astype(BF16), xdt.astype(BF16))
            y = y + _dot(cg, st.astype(BF16)) * jnp.exp(ac)
            a_last = acs[last:last + 1, ln_h:ln_h + 1]
            st_ref[h] = st * jnp.exp(a_last) + _dot_tn(bg, (xdt * jnp.exp(a_last - ac)).astype(BF16))
            ys.append(y)
    return jnp.concatenate(ys, axis=1)


def _ssd_fwd_kernel(cps, xa_ref, dl_ref, alog_ref, y_ref, st_ref):
    @pl.when(pl.program_id(1) == 0)
    def _():
        st_ref[...] = jnp.zeros(st_ref.shape, F32)

    for c in range(cps):
        sl = slice(c * SSM_CHUNK, (c + 1) * SSM_CHUNK)
        y_ref[sl, :] = _ssd_chunk(False, xa_ref[sl, :], dl_ref[sl, :], alog_ref[...], st_ref)


def _ssd_bwd_kernel(cps, xa_ref, dl_ref, alog_ref, yf_ref, z_ref, dsk_ref, g_ref, o_ref, st_ref):
    @pl.when(pl.program_id(1) == 0)
    def _():
        st_ref[...] = jnp.zeros(st_ref.shape, F32)

    dx = SSM_HEADS * SSM_HEAD_DIM
    for c in reversed(range(cps)):
        sl = slice(c * SSM_CHUNK, (c + 1) * SSM_CHUNK)
        xa = xa_ref[sl, :]
        yb = _ssd_chunk(True, xa, dl_ref[sl, :], alog_ref[...], st_ref)
        y = yf_ref[sl, :] + yb + dsk_ref[...] * xa[:, :dx]
        y = y * jax.nn.silu(z_ref[sl, :])
        o_ref[sl, :] = _rms(y, g_ref[...]).astype(o_ref.dtype)


def _ssd(xa, dl, alog, z, dsk, gssm, b, s, cps):
    t, c = xa.shape
    tm = cps * SSM_CHUNK
    nblk = s // tm
    dx = SSM_HEADS * SSM_HEAD_DIM
    st = pltpu.VMEM((SSM_HEADS, SSM_STATE, SSM_HEAD_DIM), F32)
    fw = lambda w: pl.BlockSpec((tm, w), lambda bi, ci: (bi * nblk + ci, 0))
    bw = lambda w: pl.BlockSpec((tm, w), lambda bi, ci: (bi * nblk + nblk - 1 - ci, 0))
    yf = pl.pallas_call(
        functools.partial(_ssd_fwd_kernel, cps),
        grid=(b, nblk),
        in_specs=[fw(c), fw(dl.shape[1]), _full(alog.shape)],
        out_specs=fw(dx),
        out_shape=jax.ShapeDtypeStruct((t, dx), F32),
        scratch_shapes=[st],
        compiler_params=_cparams(("parallel", "arbitrary")),
        name="ssd_fwd",
    )(xa, dl, alog)
    return pl.pallas_call(
        functools.partial(_ssd_bwd_kernel, cps),
        grid=(b, nblk),
        in_specs=[bw(c), bw(dl.shape[1]), _full(alog.shape), bw(dx), bw(dx), _full(dsk.shape),
                  _full(gssm.shape)],
        out_specs=bw(dx),
        out_shape=jax.ShapeDtypeStruct((t, dx), BF16),
        scratch_shapes=[st],
        compiler_params=_cparams(("parallel", "arbitrary")),
        name="ssd_bwd",
    )(xa, dl, alog, yf, z, dsk, gssm)


def _sgu_kernel(nck, u_ref, v_ref, w_ref, bias_ref, o_ref):
    for c in range(nck):
        sl = slice(c * SGU_CHUNK, (c + 1) * SGU_CHUNK)
        v = v_ref[sl, :]
        sv = [_dot(w_ref[g], v[:, g * SGU_GROUP_DIM:(g + 1) * SGU_GROUP_DIM]) for g in range(SGU_GROUPS)]
        o_ref[sl, :] = (u_ref[sl, :] * (jnp.concatenate(sv, axis=1) + bias_ref[...])).astype(o_ref.dtype)


def _sgu(u, v, w, bias, nck):
    t, c = u.shape
    tm = nck * SGU_CHUNK
    return pl.pallas_call(
        functools.partial(_sgu_kernel, nck),
        grid=(t // tm,),
        in_specs=[pl.BlockSpec((tm, c), lambda i: (i, 0)), pl.BlockSpec((tm, c), lambda i: (i, 0)),
                  _full(w.shape), _full(bias.shape)],
        out_specs=pl.BlockSpec((tm, c), lambda i: (i, 0)),
        out_shape=jax.ShapeDtypeStruct((t, c), BF16),
        compiler_params=_cparams(("parallel",)),
        name="sgu",
    )(u, v, w, bias)


def _merge_kernel(x_ref, gmix_ref, oa_ref, ob_ref, oc_ref, od_ref, wg_ref, wbr_ref, wo_ref, o_ref):
    x = x_ref[...]
    dm = x.shape[1]
    hb = _rms(x, gmix_ref[...]).astype(BF16)
    merged = jnp.zeros(x.shape, F32)
    for n, br in enumerate((oa_ref, ob_ref, oc_ref, od_ref)):
        gate = jax.nn.sigmoid(_dot(hb, wg_ref[:, n * dm:(n + 1) * dm]))
        merged = merged + gate * _dot(br[...], wbr_ref[n])
    o_ref[...] = x + _dot(merged.astype(BF16), wo_ref[...])


def _merge(x, gmix, oa, ob, oc, od, wg, wbr, wo, tm):
    t, dm = x.shape
    row = lambda w: pl.BlockSpec((tm, w), lambda i: (i, 0))
    return pl.pallas_call(
        _merge_kernel,
        grid=(t // tm,),
        in_specs=[row(dm), _full(gmix.shape), row(256), row(256), row(256), row(256),
                  _full(wg.shape), _full(wbr.shape), _full(wo.shape)],
        out_specs=row(dm),
        out_shape=jax.ShapeDtypeStruct((t, dm), F32),
        compiler_params=_cparams(("parallel",)),
        name="merge",
    )(x, gmix, oa, ob, oc, od, wg, wbr, wo)


def _router_kernel(x_ref, g_ref, wrt_ref, h_ref, aff_ref):
    h = _rms(x_ref[...], g_ref[...])
    h_ref[...] = h.astype(BF16)
    logits = lax.dot_general(wrt_ref[...], h, (((1,), (1,)), ((), ())), precision=HIGHEST,
                             preferred_element_type=F32)
    e = jnp.exp(logits - jnp.max(logits, axis=0, keepdims=True))
    aff_ref[...] = e / jnp.sum(e, axis=0, keepdims=True)


def _router(x, g, wrt, b, s, tm):
    t, dm = x.shape
    ne = wrt.shape[0]
    nst = s // tm
    return pl.pallas_call(
        _router_kernel,
        grid=(t // tm,),
        in_specs=[pl.BlockSpec((tm, dm), lambda i: (i, 0)), _full(g.shape), _full(wrt.shape)],
        out_specs=[pl.BlockSpec((tm, dm), lambda i: (i, 0)),
                   pl.BlockSpec((None, ne, tm), lambda i: (i // nst, 0, i % nst))],
        out_shape=[jax.ShapeDtypeStruct((t, dm), BF16), jax.ShapeDtypeStruct((b, ne, s), F32)],
        compiler_params=_cparams(("parallel",)),
        name="router",
    )(x, g, wrt)


def _select_kernel(cap, aff_ref, gate_ref):
    ne, s = aff_ref.shape
    bits = pltpu.bitcast(aff_ref[...], jnp.int32)
    capf = jnp.float32(cap)

    def bisect(i, thr):
        cand = thr | lax.shift_left(jnp.int32(1), 30 - i)
        cnt = jnp.sum((bits >= cand).astype(F32), axis=1, keepdims=True)
        return jnp.where(cnt >= capf, cand, thr)

    thr = lax.fori_loop(0, 31, bisect, jnp.zeros((ne, 1), jnp.int32))
    need = capf - jnp.sum((bits > thr).astype(F32), axis=1, keepdims=True)
    lanes = V7X_LANES
    r = lax.broadcasted_iota(jnp.int32, (lanes, lanes), 0)
    c = lax.broadcasted_iota(jnp.int32, (lanes, lanes), 1)
    before = (r < c).astype(BF16)

    def block(jb, carry):
        off = pl.multiple_of(jb * lanes, lanes)
        a = aff_ref[:, pl.ds(off, lanes)]
        bb = pltpu.bitcast(a, jnp.int32)
        eq = bb == thr
        eqf = eq.astype(F32)
        rank = carry + _dot(eqf.astype(BF16), before)
        keep = jnp.logical_or(bb > thr, jnp.logical_and(eq, rank < need))
        gate_ref[:, pl.ds(off, lanes)] = jnp.where(keep, a, 0.0)
        return carry + jnp.sum(eqf, axis=1, keepdims=True)

    lax.fori_loop(0, s // lanes, block, jnp.zeros((ne, 1), F32))


def _select(aff, cap):
    b, ne, s = aff.shape
    return pl.pallas_call(
        functools.partial(_select_kernel, cap),
        grid=(b,),
        in_specs=[pl.BlockSpec((None, ne, s), lambda i: (i, 0, 0))],
        out_specs=pl.BlockSpec((None, ne, s), lambda i: (i, 0, 0)),
        out_shape=jax.ShapeDtypeStruct(aff.shape, F32),
        compiler_params=_cparams(("parallel",)),
        name="moe_select",
    )(aff)


def _moe_kernel(x_ref, h_ref, gates_ref, wg_ref, wu_ref, wd_ref, o_ref):
    e = pl.program_id(1)

    @pl.when(e == 0)
    def _():
        o_ref[...] = x_ref[...]

    h = h_ref[...]
    lane = lax.broadcasted_iota(jnp.int32, gates_ref.shape, 1)
    gcol = jnp.sum(jnp.where(lane == e, gates_ref[...], 0.0), axis=1, keepdims=True)
    hid = jax.nn.silu(_dot(h, wg_ref[...])) * _dot(h, wu_ref[...])
    o_ref[...] += _dot(hid.astype(BF16), wd_ref[...]) * gcol


def _moe(x, h, gates, wg, wu, wd, tm):
    t, dm = x.shape
    ne, _, dff = wg.shape
    return pl.pallas_call(
        _moe_kernel,
        grid=(t // tm, ne),
        in_specs=[pl.BlockSpec((tm, dm), lambda i, e: (i, 0)),
                  pl.BlockSpec((tm, dm), lambda i, e: (i, 0)),
                  pl.BlockSpec((tm, ne), lambda i, e: (i, 0)),
                  pl.BlockSpec((None, dm, dff), lambda i, e: (e, 0, 0)),
                  pl.BlockSpec((None, dm, dff), lambda i, e: (e, 0, 0)),
                  pl.BlockSpec((None, dff, dm), lambda i, e: (e, 0, 0))],
        out_specs=pl.BlockSpec((tm, dm), lambda i, e: (i, 0)),
        out_shape=jax.ShapeDtypeStruct((t, dm), F32),
        compiler_params=_cparams(("parallel", "arbitrary")),
        name="moe_ffn",
    )(x, h, gates, wg, wu, wd)


def _ple_kernel(final, x_ref, p_ref, g_ref, wpg_ref, wp_ref, gf_ref, o_ref):
    x = x_ref[...]
    gate = jax.nn.sigmoid(_dot(_rms(x, g_ref[...]).astype(BF16), wpg_ref[...]))
    y = x + gate * _dot(p_ref[...].astype(BF16), wp_ref[...])
    o_ref[...] = _rms(y, gf_ref[...]) if final else y


def _ple(x, p, g, wpg, wp, gf, final, tm):
    t, dm = x.shape
    return pl.pallas_call(
        functools.partial(_ple_kernel, final),
        grid=(t // tm,),
        in_specs=[pl.BlockSpec((tm, dm), lambda i: (i, 0)), pl.BlockSpec((tm, p.shape[1]), lambda i: (i, 0)),
                  _full(g.shape), _full(wpg.shape), _full(wp.shape), _full(gf.shape)],
        out_specs=pl.BlockSpec((tm, dm), lambda i: (i, 0)),
        out_shape=jax.ShapeDtypeStruct((t, dm), F32),
        compiler_params=_cparams(("parallel",)),
        name="ple",
    )(x, p, g, wpg, wp, gf)


def _rope_tables(s):
    quarter = A_HEAD_DIM // 4
    half = A_HEAD_DIM // 2
    pos = jnp.arange(s, dtype=jnp.int32)
    freqs = ROPE_THETA ** (-jnp.arange(0, half, 2, dtype=F32) / half)
    ang_r = (pos // GRID_W).astype(F32)[:, None] * freqs[None, :]
    ang_c = (pos % GRID_W).astype(F32)[:, None] * freqs[None, :]
    zero = jnp.zeros((s, quarter), F32)
    cos = jnp.concatenate([jnp.cos(ang_r)] * 2 + [jnp.cos(ang_c)] * 2, axis=1)
    s1 = jnp.concatenate([-jnp.sin(ang_r), zero, -jnp.sin(ang_c), zero], axis=1)
    s2 = jnp.concatenate([zero, jnp.sin(ang_r), zero, jnp.sin(ang_c)], axis=1)
    two = lambda a: jnp.concatenate([a, a], axis=1)
    return two(cos), two(s1), two(s2)


def _t5_bucket(rel):
    nb = REL_BUCKETS // 2
    max_exact = nb // 2
    ret = jnp.where(rel > 0, nb, 0)
    r = jnp.abs(rel)
    rf = jnp.maximum(r, 1).astype(F32)
    large = max_exact + (jnp.log(rf / max_exact) / math.log(REL_MAX_DIST / max_exact)
                         * (nb - max_exact)).astype(jnp.int32)
    large = jnp.minimum(large, nb - 1)
    return ret + jnp.where(r < max_exact, r, large)


def _bias_tables(rel_bias, tq, tk):
    assert tk >= REL_MAX_DIST and tq % tk == 0
    rep = tq // tk
    offs = jnp.arange(-1, rep + 1, dtype=jnp.int32) * tk
    u = jnp.arange(tq + tk, dtype=jnp.int32)
    rel = offs[:, None] + (tk - 1) - u[None, :]
    gband = jnp.transpose(rel_bias[_t5_bucket(rel)], (2, 0, 1))[:, :, None, :].astype(F32) * LOG2E
    far = jnp.array([-REL_MAX_DIST, REL_MAX_DIST], dtype=jnp.int32)
    cfar = jnp.transpose(rel_bias[_t5_bucket(far)], (1, 0))[:, None, :].astype(F32) * LOG2E
    return gband, cfar


def _tile(n, pref):
    t = min(n, pref)
    assert n % t == 0
    return t


def _split(a, b, s, *dims):
    return a.reshape(b, s, *dims)


def _with_ones(vt):
    ones = jnp.ones(vt.shape[:-2] + (BF16_SUBLANE_PACK, vt.shape[-1]), vt.dtype)
    return jnp.concatenate([vt, ones], axis=-2)


def _tokens(ot):
    b, h, d, s = ot.shape
    return jnp.transpose(ot, (0, 3, 1, 2)).reshape(b * s, h * d)


def kernel(x, p, rel_bias, g_mix, w_in, g_qnorm, g_knorm, conv_w, conv_b, dt_bias_f, dt_bias_b, a_log_f, a_log_b, d_skip, g_ssm, lambda_q1, lambda_k1, lambda_q2, lambda_k2, g_diff, g_sgu, w_spatial, b_spatial, w_branch, w_branch_gate, w_out, g_moe, w_router, w_exp_gate, w_exp_up, w_exp_down, g_ple, w_ple_gate, w_ple, g_final):
    b, s, dm = x.shape
    depth = w_in.shape[0]
    t = b * s
    cap = EC_CAPACITY * s // N_EXPERTS

    tm_proj = _tile(s, 512)
    tm_prep = _tile(s, 1024)
    tq_a, tk_a = _tile(s, 512), _tile(s, 512)
    tq_c, tk_c = _tile(s, 512), _tile(s, 512)
    cps = _tile(s // SSM_CHUNK, 4)
    nck = _tile(s // SGU_CHUNK, 8)
    tm_moe = _tile(t, 1024)

    rc, rs1, rs2 = _rope_tables(s)
    gband, cfar = _bias_tables(rel_bias, tq_c, tk_c)
    hd = A_HEAD_DIM
    bd = (jnp.arange(A_Q_HEADS * hd)[:, None] // hd == jnp.arange(A_Q_HEADS * hd)[None, :] // hd).astype(F32) / hd

    sizes = (A_Q_HEADS * hd, A_KV_HEADS * hd, A_KV_HEADS * hd,
             SSM_HEADS * SSM_HEAD_DIM, SSM_HEADS * SSM_HEAD_DIM, SSM_GROUPS * SSM_STATE,
             SSM_GROUPS * SSM_STATE, SSM_HEADS, SSM_HEADS,
             DIFF_HEADS * 2 * DIFF_QK_DIM, DIFF_HEADS * 2 * DIFF_QK_DIM, DIFF_HEADS * DIFF_V_DIM,
             2 * SGU_GROUPS * SGU_GROUP_DIM)
    off = [0]
    for sz in sizes:
        off.append(off[-1] + sz)
    o_a0, o_b0, o_dt0, o_c0, o_d0, o_end = off[0], off[3], off[7], off[9], off[12], off[13]
    dt_pad = V7X_LANES - 2 * SSM_HEADS

    def lane_row(v, width=None):
        v = v.astype(F32).reshape(1, -1)
        if width is not None and v.shape[1] < width:
            v = jnp.pad(v, ((0, 0), (0, width - v.shape[1])))
        return v

    xt = x.reshape(t, dm)
    for i in range(depth):
        wi = w_in[i]
        wa = wi[:, o_a0:o_b0].astype(BF16)
        wb = jnp.pad(wi[:, o_b0:o_c0], ((0, 0), (0, dt_pad))).astype(BF16)
        wc = wi[:, o_c0:o_d0].astype(BF16)
        wd = wi[:, o_d0:o_end].astype(BF16)
        gmix = lane_row(g_mix[i])
        (qa, ka, va, z, xbc, dt, cq, ck, cv, du, dv) = _inproj(
            xt, gmix, wa, wb, wc, wd, rc, rs1, rs2,
            lane_row(jnp.tile(g_qnorm[i], A_Q_HEADS)), lane_row(jnp.tile(g_knorm[i], A_KV_HEADS)),
            bd, lane_row(g_sgu[i]), s, tm_proj)

        qt = jnp.transpose(_split(qa, b, s, A_Q_HEADS, hd), (0, 2, 3, 1))
        kh = jnp.transpose(_split(ka, b, s, A_KV_HEADS, hd), (0, 2, 1, 3))
        vt = jnp.transpose(_split(va, b, s, A_KV_HEADS, hd), (0, 2, 3, 1))
        o_a = _tokens(_attn_a(qt, kh, _with_ones(vt), tq_a, tk_a, 4))

        xa, dl = _ssd_prep(xbc, dt, conv_w[i].astype(F32), lane_row(conv_b[i]),
                           lane_row(jnp.concatenate([dt_bias_f[i], dt_bias_b[i]]), V7X_LANES), s, tm_prep)
        o_b = _ssd(xa, dl, lane_row(jnp.concatenate([a_log_f[i], a_log_b[i]]), V7X_LANES), z,
                   lane_row(jnp.repeat(d_skip[i], SSM_HEAD_DIM)), lane_row(g_ssm[i]), b, s, cps)

        lam_init = 0.8 - 0.6 * math.exp(-0.3 * i)
        lam = (jnp.exp(jnp.sum(lambda_q1[i].astype(F32) * lambda_k1[i].astype(F32)))
               - jnp.exp(jnp.sum(lambda_q2[i].astype(F32) * lambda_k2[i].astype(F32))) + lam_init).reshape(1, 1)
        cqt = jnp.transpose(_split(cq, b, s, DIFF_HEADS, 2, DIFF_QK_DIM), (0, 2, 3, 4, 1))
        ckh = jnp.transpose(_split(ck, b, s, DIFF_HEADS, 2, DIFF_QK_DIM), (0, 2, 3, 1, 4))
        cvt = jnp.transpose(_split(cv, b, s, DIFF_HEADS, DIFF_V_DIM), (0, 2, 3, 1))
        o_c = _tokens(_attn_c(cqt, ckh, _with_ones(cvt), gband, cfar, lam, g_diff[i].astype(F32).reshape(-1, 1),
                              1.0 - lam_init, tq_c, tk_c))

        sgu_bias = jnp.repeat(b_spatial[i].T.astype(F32), SGU_GROUP_DIM, axis=1)
        o_d = _sgu(du, dv, w_spatial[i].astype(BF16), sgu_bias, nck)

        xt = _merge(xt, gmix, o_a, o_b, o_c, o_d, w_branch_gate[i].astype(BF16),
                    w_branch[i].astype(BF16), w_out[i].astype(BF16), tm_proj)

        h2, aff = _router(xt, lane_row(g_moe[i]), w_router[i].T.astype(F32), b, s, tm_proj)
        gates = _select(aff, cap)
        gates_tm = jnp.transpose(gates, (0, 2, 1)).reshape(t, N_EXPERTS)
        xt = _moe(xt, h2, gates_tm, w_exp_gate[i].astype(BF16), w_exp_up[i].astype(BF16),
                  w_exp_down[i].astype(BF16), tm_moe)

        xt = _ple(xt, p[i].reshape(t, -1), lane_row(g_ple[i]), w_ple_gate[i].astype(BF16),
                  w_ple[i].astype(BF16), lane_row(g_final), i == depth - 1, tm_proj)

    return xt.reshape(b, s, dm)
```

```python
import functools
import math

import jax
import jax.numpy as jnp
from jax import lax
from jax.experimental import pallas as pl
from jax.experimental.pallas import tpu as pltpu

F32 = jnp.float32
BF16 = jnp.bfloat16
HIGHEST = lax.Precision.HIGHEST

EPS = 1e-6
GRID_W = 64
A_Q_HEADS, A_KV_HEADS, A_HEAD_DIM = 4, 2, 64
ROPE_THETA = 10000.0
SSM_HEADS, SSM_HEAD_DIM, SSM_GROUPS, SSM_STATE, SSM_CONV, SSM_CHUNK = 4, 64, 2, 128, 5, 128
DIFF_HEADS, DIFF_QK_DIM, DIFF_V_DIM = 4, 32, 64
REL_BUCKETS, REL_MAX_DIST = 32, 128
SGU_GROUPS, SGU_GROUP_DIM, SGU_CHUNK = 4, 64, 128
N_EXPERTS, EC_CAPACITY = 16, 2
N_BRANCHES, BRANCH_WIDTH = 4, 256

V7X_LANES = 128
V7X_SUBLANES = 8
BF16_SUBLANE_PACK = 16
V7X_VMEM_LIMIT_BYTES = 56 * 1024 * 1024

NEG_BIG = -1e30
LOG2E = math.log2(math.e)
ATTN_INNER = 4


def _cparams(sem):
    return pltpu.CompilerParams(dimension_semantics=sem, vmem_limit_bytes=V7X_VMEM_LIMIT_BYTES)


def _rms(x, g):
    return x * lax.rsqrt(jnp.mean(x * x, axis=-1, keepdims=True) + EPS) * g


def _dot(a, b):
    return jnp.dot(a, b, preferred_element_type=F32)


def _dot_nt(a, b):
    return lax.dot_general(a, b, (((1,), (1,)), ((), ())), preferred_element_type=F32)


def _dot_tn(a, b):
    return lax.dot_general(a, b, (((0,), (0,)), ((), ())), preferred_element_type=F32)


def _full(shape):
    n = len(shape)
    return pl.BlockSpec(shape, lambda *_: (0,) * n)


def _inproj_kernel(x_ref, gmix_ref, wa_ref, wb_ref, wc_ref, wd_ref, rc_ref, rs1_ref, rs2_ref,
                   gq_ref, gk_ref, bd_ref, gsgu_ref,
                   qa_ref, ka_ref, va_ref, z_ref, xbc_ref, dt_ref, cq_ref, ck_ref, cv_ref,
                   du_ref, dv_ref):
    hb = _rms(x_ref[...], gmix_ref[...]).astype(BF16)

    a = _dot(hb, wa_ref[...])
    nq = A_Q_HEADS * A_HEAD_DIM
    nk = A_KV_HEADS * A_HEAD_DIM
    q, k, v = a[:, :nq], a[:, nq:nq + nk], a[:, nq + nk:]
    bd = bd_ref[...]
    q = q * lax.rsqrt(jnp.dot(q * q, bd, precision=HIGHEST, preferred_element_type=F32) + EPS) * gq_ref[...]
    k = k * lax.rsqrt(jnp.dot(k * k, bd[:nk, :nk], precision=HIGHEST, preferred_element_type=F32) + EPS) * gk_ref[...]
    rc, rs1, rs2 = rc_ref[...], rs1_ref[...], rs2_ref[...]
    quarter = A_HEAD_DIM // 4

    def rope(t, c, s1, s2):
        w = t.shape[1]
        return t * c + pltpu.roll(t, w - quarter, 1) * s1 + pltpu.roll(t, quarter, 1) * s2

    k = rope(k, rc, rs1, rs2)
    q = rope(q, jnp.concatenate([rc, rc], axis=1), jnp.concatenate([rs1, rs1], axis=1),
             jnp.concatenate([rs2, rs2], axis=1))
    qa_ref[...] = (q * (A_HEAD_DIM ** -0.5 * LOG2E)).astype(BF16)
    ka_ref[...] = k.astype(BF16)
    va_ref[...] = v.astype(BF16)

    b = _dot(hb, wb_ref[...])
    dx = SSM_HEADS * SSM_HEAD_DIM
    conv_ch = dx + 2 * SSM_GROUPS * SSM_STATE
    z_ref[...] = b[:, :dx]
    xbc_ref[...] = b[:, dx:dx + conv_ch]
    dt_ref[...] = b[:, dx + conv_ch:]

    c = _dot(hb, wc_ref[...])
    nqc = DIFF_HEADS * 2 * DIFF_QK_DIM
    cq_ref[...] = (c[:, :nqc] * (DIFF_QK_DIM ** -0.5 * LOG2E)).astype(BF16)
    ck_ref[...] = c[:, nqc:2 * nqc].astype(BF16)
    cv_ref[...] = c[:, 2 * nqc:].astype(BF16)

    d = jax.nn.gelu(_dot(hb, wd_ref[...]))
    w = SGU_GROUPS * SGU_GROUP_DIM
    du_ref[...] = d[:, :w]
    dv_ref[...] = _rms(d[:, w:], gsgu_ref[...]).astype(BF16)


def _inproj(x, gmix, wa, wb, wc, wd, rc, rs1, rs2, gq, gk, bd, gsgu, s, tm):
    t, dm = x.shape
    nst = s // tm
    row = lambda w: pl.BlockSpec((tm, w), lambda i: (i, 0))
    tab = pl.BlockSpec((tm, rc.shape[1]), lambda i: (i % nst, 0))
    outs = [(256, BF16), (128, BF16), (128, BF16), (256, F32), (768, F32), (128, F32),
            (256, BF16), (256, BF16), (256, BF16), (256, F32), (256, BF16)]
    return pl.pallas_call(
        _inproj_kernel,
        grid=(t // tm,),
        in_specs=[row(dm), _full(gmix.shape), _full(wa.shape), _full(wb.shape), _full(wc.shape),
                  _full(wd.shape), tab, tab, tab, _full(gq.shape), _full(gk.shape), _full(bd.shape),
                  _full(gsgu.shape)],
        out_specs=[row(w) for w, _ in outs],
        out_shape=[jax.ShapeDtypeStruct((t, w), dt) for w, dt in outs],
        compiler_params=_cparams(("parallel",)),
        name="inproj",
    )(x, gmix, wa, wb, wc, wd, rc, rs1, rs2, gq, gk, bd, gsgu)


def _osm_tile(s, shift, vt, m_ref, acc_ref, c):
    m_old = m_ref[c]
    mt = jnp.max(s, axis=0, keepdims=True)
    if shift is not None:
        mt = mt + shift
    m_new = jnp.maximum(m_old, mt)
    p = jnp.exp2(s - (m_new if shift is None else m_new - shift))
    acc_ref[c] = jnp.exp2(m_old - m_new) * acc_ref[c] + _dot(vt, p.astype(BF16))
    m_ref[c] = m_new


def _osm_init(m_ref, acc_ref):
    m_ref[...] = jnp.full(m_ref.shape, NEG_BIG, F32)
    acc_ref[...] = jnp.zeros(acc_ref.shape, F32)


def _osm_result(acc_ref, c, dv):
    acc = acc_ref[c]
    return acc[:dv] / acc[dv:dv + 1]


def _attn_scores(tk, js, qs, k_of, s_ref, bias_tile=None):
    for u, j in enumerate(js):
        off = pl.multiple_of(j * tk, tk)
        for c, q in enumerate(qs):
            s = _dot(k_of(c, off), q)
            s_ref[u * len(qs) + c] = s if bias_tile is None else s + bias_tile


def _attn_values(tk, js, nmaps, vt_ref, s_ref, m_ref, acc_ref, shifts=None):
    for u, j in enumerate(js):
        off = pl.multiple_of(j * tk, tk)
        vt = vt_ref[:, pl.ds(off, tk)]
        for c in range(nmaps):
            _osm_tile(s_ref[u * nmaps + c], None if shifts is None else shifts[u], vt, m_ref, acc_ref, c)


def _pipelined(n, scores, values, s_refs, inner):
    last = jnp.maximum(n - 1, 0)
    span = 2 * inner
    scores(jnp.minimum(0, last), s_refs[0])

    def double(g):
        scores(jnp.minimum(g + 1, last), s_refs[1])
        values(g, s_refs[0])
        scores(jnp.minimum(g + 2, last), s_refs[0])
        values(g + 1, s_refs[1])

    def unrolled(t, carry):
        for w in range(inner):
            double(span * t + 2 * w)
        return carry

    lax.fori_loop(0, n // span, unrolled, 0)
    base = (n // span) * span

    def rolled(t, carry):
        double(base + 2 * t)
        return carry

    lax.fori_loop(0, (n - base) // 2, rolled, 0)

    @pl.when(n % 2 == 1)
    def _():
        values(n - 1, s_refs[0])


def _attn_a_kernel(tk, inner, qt_ref, k_ref, vt_ref, o_ref, m_ref, acc_ref, sa_ref, sb_ref):
    _osm_init(m_ref, acc_ref)
    qs = (qt_ref[...],)
    k_of = lambda c, off: k_ref[pl.ds(off, tk), :]
    _pipelined(k_ref.shape[0] // tk,
               lambda j, s_ref: _attn_scores(tk, [j], qs, k_of, s_ref),
               lambda j, s_ref: _attn_values(tk, [j], 1, vt_ref, s_ref, m_ref, acc_ref),
               (sa_ref, sb_ref), inner)
    o_ref[...] = _osm_result(acc_ref, 0, o_ref.shape[0]).astype(o_ref.dtype)


def _attn_a(qt, k, vt, tq, tk, inner):
    b, hq, d, s = qt.shape
    rep = hq // k.shape[1]
    da = vt.shape[2]
    return pl.pallas_call(
        functools.partial(_attn_a_kernel, tk, inner),
        grid=(b, hq, s // tq),
        in_specs=[pl.BlockSpec((None, None, d, tq), lambda bi, h, i: (bi, h, 0, i)),
                  pl.BlockSpec((None, None, s, d), lambda bi, h, i: (bi, h // rep, 0, 0)),
                  pl.BlockSpec((None, None, da, s), lambda bi, h, i: (bi, h // rep, 0, 0))],
        out_specs=pl.BlockSpec((None, None, d, tq), lambda bi, h, i: (bi, h, 0, i)),
        out_shape=jax.ShapeDtypeStruct(qt.shape, BF16),
        scratch_shapes=[pltpu.VMEM((1, 1, tq), F32), pltpu.VMEM((1, da, tq), F32),
                        pltpu.VMEM((1, tk, tq), F32), pltpu.VMEM((1, tk, tq), F32)],
        compiler_params=_cparams(("parallel", "parallel", "arbitrary")),
        name="attn_a",
    )(qt, k, vt)


def _attn_c_kernel(tk, inner, out_scale, qt_ref, k_ref, vt_ref, gband_ref, cfar_ref, lam_ref, gd_ref, o_ref,
                   m_ref, acc_ref, sa_ref, sb_ref):
    i = pl.program_id(2)
    tq = qt_ref.shape[2]
    nk = k_ref.shape[1] // tk
    nband = gband_ref.shape[0]
    dv = o_ref.shape[0]
    _osm_init(m_ref, acc_ref)
    qs = (qt_ref[0], qt_ref[1])
    k_of = lambda c, off: k_ref[c, pl.ds(off, tk), :]

    j_lo = (tq // tk) * i - 1
    n_left = jnp.clip(j_lo, 0, nk)
    j_hi = jnp.clip(j_lo + nband, 0, nk)
    n_far = n_left + nk - j_hi
    cf = cfar_ref[...]

    def far_tile(t):
        j = jnp.where(t < n_left, t, t - n_left + j_hi)
        return jnp.clip(j, 0, nk - 1), jnp.where(t < n_left, cf[:, 0:1], cf[:, 1:2])

    def far_scores(t, s_ref):
        _attn_scores(tk, [far_tile(t)[0]], qs, k_of, s_ref)

    def far_values(t, s_ref):
        j, sh = far_tile(t)
        _attn_values(tk, [j], 2, vt_ref, s_ref, m_ref, acc_ref, shifts=[sh])

    _pipelined(n_far, far_scores, far_values, (sa_ref, sb_ref), inner)

    for d in range(nband):
        j = j_lo + d

        @pl.when(jnp.logical_and(j >= 0, j < nk))
        def _():
            g = gband_ref[d]
            t = pltpu.roll(jnp.broadcast_to(g, (tk, g.shape[1])), tq + 1, 1, stride=1, stride_axis=0)
            _attn_scores(tk, [j], qs, k_of, sa_ref, bias_tile=t[:, :tq])
            _attn_values(tk, [j], 2, vt_ref, sa_ref, m_ref, acc_ref)

    o = _osm_result(acc_ref, 0, dv) - lam_ref[...] * _osm_result(acc_ref, 1, dv)
    o = o * lax.rsqrt(jnp.mean(o * o, axis=0, keepdims=True) + EPS) * gd_ref[...]
    o_ref[...] = (o * out_scale).astype(o_ref.dtype)


def _attn_c(qt, k, vt, gband, cfar, lam, gd, out_scale, tq, tk, inner):
    b, h, _, dqk, s = qt.shape
    dva = vt.shape[2]
    dv = gd.shape[0]
    nband = gband.shape[1]
    return pl.pallas_call(
        functools.partial(_attn_c_kernel, tk, inner, out_scale),
        grid=(b, h, s // tq),
        in_specs=[pl.BlockSpec((None, None, 2, dqk, tq), lambda bi, hi, i: (bi, hi, 0, 0, i)),
                  pl.BlockSpec((None, None, 2, s, dqk), lambda bi, hi, i: (bi, hi, 0, 0, 0)),
                  pl.BlockSpec((None, None, dva, s), lambda bi, hi, i: (bi, hi, 0, 0)),
                  pl.BlockSpec((None, nband, 1, tq + tk), lambda bi, hi, i: (hi, 0, 0, 0)),
                  pl.BlockSpec((None, 1, 2), lambda bi, hi, i: (hi, 0, 0)),
                  _full(lam.shape), _full(gd.shape)],
        out_specs=pl.BlockSpec((None, None, dv, tq), lambda bi, hi, i: (bi, hi, 0, i)),
        out_shape=jax.ShapeDtypeStruct((b, h, dv, s), BF16),
        scratch_shapes=[pltpu.VMEM((2, 1, tq), F32), pltpu.VMEM((2, dva, tq), F32),
                        pltpu.VMEM((2, tk, tq), F32), pltpu.VMEM((2, tk, tq), F32)],
        compiler_params=_cparams(("parallel", "parallel", "arbitrary")),
        name="attn_c",
    )(qt, k, vt, gband, cfar, lam, gd)


def _ssd_prep_kernel(nst, xc_ref, xp_ref, xn_ref, dt_ref, cw_ref, cb_ref, dtb_ref, xa_ref, dl_ref, ext_ref):
    i = pl.program_id(0)
    tm = xc_ref.shape[0]
    halo = V7X_SUBLANES
    pos = i % nst
    keep_prev = (pos != 0).astype(F32)
    keep_next = (pos != nst - 1).astype(F32)
    ext_ref[0:halo, :] = xp_ref[...] * keep_prev
    ext_ref[halo:halo + tm, :] = xc_ref[...]
    ext_ref[halo + tm:, :] = xn_ref[...] * keep_next
    pad_l = (SSM_CONV - 1) // 2
    cw = cw_ref[...]
    acc = jnp.zeros(xc_ref.shape, F32) + cb_ref[...]
    for t in range(SSM_CONV):
        acc = acc + ext_ref[halo - pad_l + t:halo - pad_l + t + tm, :] * cw[t:t + 1, :]
    xa_ref[...] = jax.nn.silu(acc)
    dl_ref[...] = jax.nn.softplus(dt_ref[...] + dtb_ref[...])


def _ssd_prep(xbc, dt, cw, cb, dtb, s, tm):
    t, c = xbc.shape
    nst = s // tm
    hb = tm // V7X_SUBLANES
    nhb = t // V7X_SUBLANES
    return pl.pallas_call(
        functools.partial(_ssd_prep_kernel, nst),
        grid=(t // tm,),
        in_specs=[pl.BlockSpec((tm, c), lambda i: (i, 0)),
                  pl.BlockSpec((V7X_SUBLANES, c), lambda i: (jnp.maximum(i * hb - 1, 0), 0)),
                  pl.BlockSpec((V7X_SUBLANES, c), lambda i: (jnp.minimum((i + 1) * hb, nhb - 1), 0)),
                  pl.BlockSpec((tm, dt.shape[1]), lambda i: (i, 0)),
                  _full(cw.shape), _full(cb.shape), _full(dtb.shape)],
        out_specs=[pl.BlockSpec((tm, c), lambda i: (i, 0)),
                   pl.BlockSpec((tm, dt.shape[1]), lambda i: (i, 0))],
        out_shape=[jax.ShapeDtypeStruct((t, c), F32), jax.ShapeDtypeStruct(dt.shape, F32)],
        scratch_shapes=[pltpu.VMEM((tm + 2 * V7X_SUBLANES, c), F32)],
        compiler_params=_cparams(("parallel",)),
        name="ssd_prep",
    )(xbc, xbc, xbc, dt, cw, cb, dtb)


def _ssd_chunk(reverse, xa, dl, alog, st_ref):
    ln = SSM_CHUNK
    row = lax.broadcasted_iota(jnp.int32, (ln, ln), 0)
    col = lax.broadcasted_iota(jnp.int32, (ln, ln), 1)
    tri = (col >= row) if reverse else (col <= row)
    eye = row == col
    last = 0 if reverse else ln - 1
    lane0 = SSM_HEADS if reverse else 0
    a_neg = -jnp.exp(alog)
    acs = jnp.dot(tri.astype(F32), dl * a_neg, precision=HIGHEST, preferred_element_type=F32)
    dx = SSM_HEADS * SSM_HEAD_DIM
    gw = SSM_STATE
    hpg = SSM_HEADS // SSM_GROUPS
    ys = []
    for g in range(SSM_GROUPS):
        bg = xa[:, dx + g * gw:dx + (g + 1) * gw].astype(BF16)
        cg = xa[:, dx + SSM_GROUPS * gw + g * gw:dx + SSM_GROUPS * gw + (g + 1) * gw].astype(BF16)
        cb = _dot_nt(cg, bg)
        for r in range(hpg):
            h = g * hpg + r
            ln_h = lane0 + h
            ac = acs[:, ln_h:ln_h + 1]
            ar = jnp.sum(jnp.where(eye, ac, 0.0), axis=0, keepdims=True)
            dec = jnp.exp(jnp.where(tri, ac - ar, NEG_BIG))
            xdt = xa[:, h * SSM_HEAD_DIM:(h + 1) * SSM_HEAD_DIM] * dl[:, ln_h:ln_h + 1]
            st = st_ref[h]
            y = _dot((cb * dec).astype(BF16), xdt.astype(BF16))
            y = y + _dot(cg, st.astype(BF16)) * jnp.exp(ac)
            a_last = acs[last:last + 1, ln_h:ln_h + 1]
            st_ref[h] = st * jnp.exp(a_last) + _dot_tn(bg, (xdt * jnp.exp(a_last - ac)).astype(BF16))
            ys.append(y)
    return jnp.concatenate(ys, axis=1)


def _ssd_fwd_kernel(cps, xa_ref, dl_ref, alog_ref, y_ref, st_ref):
    @pl.when(pl.program_id(1) == 0)
    def _():
        st_ref[...] = jnp.zeros(st_ref.shape, F32)

    for c in range(cps):
        sl = slice(c * SSM_CHUNK, (c + 1) * SSM_CHUNK)
        y_ref[sl, :] = _ssd_chunk(False, xa_ref[sl, :], dl_ref[sl, :], alog_ref[...], st_ref)


def _ssd_bwd_kernel(cps, xa_ref, dl_ref, alog_ref, yf_ref, z_ref, dsk_ref, g_ref, o_ref, st_ref):
    @pl.when(pl.program_id(1) == 0)
    def _():
        st_ref[...] = jnp.zeros(st_ref.shape, F32)

    dx = SSM_HEADS * SSM_HEAD_DIM
    for c in reversed(range(cps)):
        sl = slice(c * SSM_CHUNK, (c + 1) * SSM_CHUNK)
        xa = xa_ref[sl, :]
        yb = _ssd_chunk(True, xa, dl_ref[sl, :], alog_ref[...], st_ref)
        y = yf_ref[sl, :] + yb + dsk_ref[...] * xa[:, :dx]
        y = y * jax.nn.silu(z_ref[sl, :])
        o_ref[sl, :] = _rms(y, g_ref[...]).astype(o_ref.dtype)


def _ssd(xa, dl, alog, z, dsk, gssm, b, s, cps):
    t, c = xa.shape
    tm = cps * SSM_CHUNK
    nblk = s // tm
    dx = SSM_HEADS * SSM_HEAD_DIM
    st = pltpu.VMEM((SSM_HEADS, SSM_STATE, SSM_HEAD_DIM), F32)
    fw = lambda w: pl.BlockSpec((tm, w), lambda bi, ci: (bi * nblk + ci, 0))
    bw = lambda w: pl.BlockSpec((tm, w), lambda bi, ci: (bi * nblk + nblk - 1 - ci, 0))
    yf = pl.pallas_call(
        functools.partial(_ssd_fwd_kernel, cps),
        grid=(b, nblk),
        in_specs=[fw(c), fw(dl.shape[1]), _full(alog.shape)],
        out_specs=fw(dx),
        out_shape=jax.ShapeDtypeStruct((t, dx), F32),
        scratch_shapes=[st],
        compiler_params=_cparams(("parallel", "arbitrary")),
        name="ssd_fwd",
    )(xa, dl, alog)
    return pl.pallas_call(
        functools.partial(_ssd_bwd_kernel, cps),
        grid=(b, nblk),
        in_specs=[bw(c), bw(dl.shape[1]), _full(alog.shape), bw(dx), bw(dx), _full(dsk.shape),
                  _full(gssm.shape)],
        out_specs=bw(dx),
        out_shape=jax.ShapeDtypeStruct((t, dx), BF16),
        scratch_shapes=[st],
        compiler_params=_cparams(("parallel", "arbitrary")),
        name="ssd_bwd",
    )(xa, dl, alog, yf, z, dsk, gssm)


def _sgu_kernel(nck, u_ref, v_ref, w_ref, bias_ref, o_ref):
    for c in range(nck):
        sl = slice(c * SGU_CHUNK, (c + 1) * SGU_CHUNK)
        v = v_ref[sl, :]
        sv = [_dot(w_ref[g], v[:, g * SGU_GROUP_DIM:(g + 1) * SGU_GROUP_DIM]) for g in range(SGU_GROUPS)]
        o_ref[sl, :] = (u_ref[sl, :] * (jnp.concatenate(sv, axis=1) + bias_ref[...])).astype(o_ref.dtype)


def _sgu(u, v, w, bias, nck):
    t, c = u.shape
    tm = nck * SGU_CHUNK
    return pl.pallas_call(
        functools.partial(_sgu_kernel, nck),
        grid=(t // tm,),
        in_specs=[pl.BlockSpec((tm, c), lambda i: (i, 0)), pl.BlockSpec((tm, c), lambda i: (i, 0)),
                  _full(w.shape), _full(bias.shape)],
        out_specs=pl.BlockSpec((tm, c), lambda i: (i, 0)),
        out_shape=jax.ShapeDtypeStruct((t, c), BF16),
        compiler_params=_cparams(("parallel",)),
        name="sgu",
    )(u, v, w, bias)


def _merge_kernel(x_ref, gmix_ref, oa_ref, ob_ref, oc_ref, od_ref, wg_ref, wbr_ref, wo_ref, o_ref):
    x = x_ref[...]
    dm = x.shape[1]
    hb = _rms(x, gmix_ref[...]).astype(BF16)
    merged = jnp.zeros(x.shape, F32)
    for n, br in enumerate((oa_ref, ob_ref, oc_ref, od_ref)):
        gate = jax.nn.sigmoid(_dot(hb, wg_ref[:, n * dm:(n + 1) * dm]))
        merged = merged + gate * _dot(br[...], wbr_ref[n])
    o_ref[...] = x + _dot(merged.astype(BF16), wo_ref[...])


def _merge(x, gmix, oa, ob, oc, od, wg, wbr, wo, tm):
    t, dm = x.shape
    row = lambda w: pl.BlockSpec((tm, w), lambda i: (i, 0))
    return pl.pallas_call(
        _merge_kernel,
        grid=(t // tm,),
        in_specs=[row(dm), _full(gmix.shape), row(256), row(256), row(256), row(256),
                  _full(wg.shape), _full(wbr.shape), _full(wo.shape)],
        out_specs=row(dm),
        out_shape=jax.ShapeDtypeStruct((t, dm), F32),
        compiler_params=_cparams(("parallel",)),
        name="merge",
    )(x, gmix, oa, ob, oc, od, wg, wbr, wo)


def _router_kernel(x_ref, g_ref, wrt_ref, h_ref, aff_ref):
    h = _rms(x_ref[...], g_ref[...])
    h_ref[...] = h.astype(BF16)
    logits = lax.dot_general(wrt_ref[...], h, (((1,), (1,)), ((), ())), precision=HIGHEST,
                             preferred_element_type=F32)
    e = jnp.exp(logits - jnp.max(logits, axis=0, keepdims=True))
    aff_ref[...] = e / jnp.sum(e, axis=0, keepdims=True)


def _router(x, g, wrt, b, s, tm):
    t, dm = x.shape
    ne = wrt.shape[0]
    nst = s // tm
    return pl.pallas_call(
        _router_kernel,
        grid=(t // tm,),
        in_specs=[pl.BlockSpec((tm, dm), lambda i: (i, 0)), _full(g.shape), _full(wrt.shape)],
        out_specs=[pl.BlockSpec((tm, dm), lambda i: (i, 0)),
                   pl.BlockSpec((None, ne, tm), lambda i: (i // nst, 0, i % nst))],
        out_shape=[jax.ShapeDtypeStruct((t, dm), BF16), jax.ShapeDtypeStruct((b, ne, s), F32)],
        compiler_params=_cparams(("parallel",)),
        name="router",
    )(x, g, wrt)


def _select_kernel(cap, aff_ref, gate_ref):
    ne, s = aff_ref.shape
    bits = pltpu.bitcast(aff_ref[...], jnp.int32)
    capf = jnp.float32(cap)

    def bisect(i, thr):
        cand = thr | lax.shift_left(jnp.int32(1), 30 - i)
        cnt = jnp.sum((bits >= cand).astype(F32), axis=1, keepdims=True)
        return jnp.where(cnt >= capf, cand, thr)

    thr = lax.fori_loop(0, 31, bisect, jnp.zeros((ne, 1), jnp.int32))
    need = capf - jnp.sum((bits > thr).astype(F32), axis=1, keepdims=True)
    lanes = V7X_LANES
    r = lax.broadcasted_iota(jnp.int32, (lanes, lanes), 0)
    c = lax.broadcasted_iota(jnp.int32, (lanes, lanes), 1)
    before = (r < c).astype(BF16)

    def block(jb, carry):
        off = pl.multiple_of(jb * lanes, lanes)
        a = aff_ref[:, pl.ds(off, lanes)]
        bb = pltpu.bitcast(a, jnp.int32)
        eq = bb == thr
        eqf = eq.astype(F32)
        rank = carry + _dot(eqf.astype(BF16), before)
        keep = jnp.logical_or(bb > thr, jnp.logical_and(eq, rank < need))
        gate_ref[:, pl.ds(off, lanes)] = jnp.where(keep, a, 0.0)
        return carry + jnp.sum(eqf, axis=1, keepdims=True)

    lax.fori_loop(0, s // lanes, block, jnp.zeros((ne, 1), F32))


def _select(aff, cap):
    b, ne, s = aff.shape
    return pl.pallas_call(
        functools.partial(_select_kernel, cap),
        grid=(b,),
        in_specs=[pl.BlockSpec((None, ne, s), lambda i: (i, 0, 0))],
        out_specs=pl.BlockSpec((None, ne, s), lambda i: (i, 0, 0)),
        out_shape=jax.ShapeDtypeStruct(aff.shape, F32),
        compiler_params=_cparams(("parallel",)),
        name="moe_select",
    )(aff)


def _moe_kernel(x_ref, h_ref, gates_ref, wg_ref, wu_ref, wd_ref, o_ref):
    e = pl.program_id(1)

    @pl.when(e == 0)
    def _():
        o_ref[...] = x_ref[...]

    h = h_ref[...]
    lane = lax.broadcasted_iota(jnp.int32, gates_ref.shape, 1)
    gcol = jnp.sum(jnp.where(lane == e, gates_ref[...], 0.0), axis=1, keepdims=True)
    hid = jax.nn.silu(_dot(h, wg_ref[...])) * _dot(h, wu_ref[...])
    o_ref[...] += _dot(hid.astype(BF16), wd_ref[...]) * gcol


def _moe(x, h, gates, wg, wu, wd, tm):
    t, dm = x.shape
    ne, _, dff = wg.shape
    return pl.pallas_call(
        _moe_kernel,
        grid=(t // tm, ne),
        in_specs=[pl.BlockSpec((tm, dm), lambda i, e: (i, 0)),
                  pl.BlockSpec((tm, dm), lambda i, e: (i, 0)),
                  pl.BlockSpec((tm, ne), lambda i, e: (i, 0)),
                  pl.BlockSpec((None, dm, dff), lambda i, e: (e, 0, 0)),
                  pl.BlockSpec((None, dm, dff), lambda i, e: (e, 0, 0)),
                  pl.BlockSpec((None, dff, dm), lambda i, e: (e, 0, 0))],
        out_specs=pl.BlockSpec((tm, dm), lambda i, e: (i, 0)),
        out_shape=jax.ShapeDtypeStruct((t, dm), F32),
        compiler_params=_cparams(("parallel", "arbitrary")),
        name="moe_ffn",
    )(x, h, gates, wg, wu, wd)


def _ple_kernel(final, x_ref, p_ref, g_ref, wpg_ref, wp_ref, gf_ref, o_ref):
    x = x_ref[...]
    gate = jax.nn.sigmoid(_dot(_rms(x, g_ref[...]).astype(BF16), wpg_ref[...]))
    y = x + gate * _dot(p_ref[...].astype(BF16), wp_ref[...])
    o_ref[...] = _rms(y, gf_ref[...]) if final else y


def _ple(x, p, g, wpg, wp, gf, final, tm):
    t, dm = x.shape
    return pl.pallas_call(
        functools.partial(_ple_kernel, final),
        grid=(t // tm,),
        in_specs=[pl.BlockSpec((tm, dm), lambda i: (i, 0)), pl.BlockSpec((tm, p.shape[1]), lambda i: (i, 0)),
                  _full(g.shape), _full(wpg.shape), _full(wp.shape), _full(gf.shape)],
        out_specs=pl.BlockSpec((tm, dm), lambda i: (i, 0)),
        out_shape=jax.ShapeDtypeStruct((t, dm), F32),
        compiler_params=_cparams(("parallel",)),
        name="ple",
    )(x, p, g, wpg, wp, gf)


def _rope_tables(s):
    quarter = A_HEAD_DIM // 4
    half = A_HEAD_DIM // 2
    pos = jnp.arange(s, dtype=jnp.int32)
    freqs = ROPE_THETA ** (-jnp.arange(0, half, 2, dtype=F32) / half)
    ang_r = (pos // GRID_W).astype(F32)[:, None] * freqs[None, :]
    ang_c = (pos % GRID_W).astype(F32)[:, None] * freqs[None, :]
    zero = jnp.zeros((s, quarter), F32)
    cos = jnp.concatenate([jnp.cos(ang_r)] * 2 + [jnp.cos(ang_c)] * 2, axis=1)
    s1 = jnp.concatenate([-jnp.sin(ang_r), zero, -jnp.sin(ang_c), zero], axis=1)
    s2 = jnp.concatenate([zero, jnp.sin(ang_r), zero, jnp.sin(ang_c)], axis=1)
    two = lambda a: jnp.concatenate([a, a], axis=1)
    return two(cos), two(s1), two(s2)


def _t5_bucket(rel):
    nb = REL_BUCKETS // 2
    max_exact = nb // 2
    ret = jnp.where(rel > 0, nb, 0)
    r = jnp.abs(rel)
    rf = jnp.maximum(r, 1).astype(F32)
    large = max_exact + (jnp.log(rf / max_exact) / math.log(REL_MAX_DIST / max_exact)
                         * (nb - max_exact)).astype(jnp.int32)
    large = jnp.minimum(large, nb - 1)
    return ret + jnp.where(r < max_exact, r, large)


def _bias_tables(rel_bias, tq, tk):
    assert tk >= REL_MAX_DIST and tq % tk == 0
    rep = tq // tk
    offs = jnp.arange(-1, rep + 1, dtype=jnp.int32) * tk
    u = jnp.arange(tq + tk, dtype=jnp.int32)
    rel = offs[:, None] + (tk - 1) - u[None, :]
    gband = jnp.transpose(rel_bias[_t5_bucket(rel)], (2, 0, 1))[:, :, None, :].astype(F32) * LOG2E
    far = jnp.array([-REL_MAX_DIST, REL_MAX_DIST], dtype=jnp.int32)
    cfar = jnp.transpose(rel_bias[_t5_bucket(far)], (1, 0))[:, None, :].astype(F32) * LOG2E
    return gband, cfar


def _tile(n, pref):
    t = min(n, pref)
    assert n % t == 0
    return t


def _split(a, b, s, *dims):
    return a.reshape(b, s, *dims)


def _with_ones(vt):
    ones = jnp.ones(vt.shape[:-2] + (BF16_SUBLANE_PACK, vt.shape[-1]), vt.dtype)
    return jnp.concatenate([vt, ones], axis=-2)


def _tokens(ot):
    b, h, d, s = ot.shape
    return jnp.transpose(ot, (0, 3, 1, 2)).reshape(b * s, h * d)


def kernel(x, p, rel_bias, g_mix, w_in, g_qnorm, g_knorm, conv_w, conv_b, dt_bias_f, dt_bias_b, a_log_f, a_log_b, d_skip, g_ssm, lambda_q1, lambda_k1, lambda_q2, lambda_k2, g_diff, g_sgu, w_spatial, b_spatial, w_branch, w_branch_gate, w_out, g_moe, w_router, w_exp_gate, w_exp_up, w_exp_down, g_ple, w_ple_gate, w_ple, g_final):
    b, s, dm = x.shape
    depth = w_in.shape[0]
    t = b * s
    cap = EC_CAPACITY * s // N_EXPERTS

    tm_proj = _tile(s, 512)
    tm_prep = _tile(s, 1024)
    tq_a, tk_a = _tile(s, 512), _tile(s, 512)
    tq_c, tk_c = _tile(s, 512), _tile(s, 512)
    cps = _tile(s // SSM_CHUNK, 4)
    nck = _tile(s // SGU_CHUNK, 8)
    tm_moe = _tile(t, 1024)

    rc, rs1, rs2 = _rope_tables(s)
    gband, cfar = _bias_tables(rel_bias, tq_c, tk_c)
    hd = A_HEAD_DIM
    bd = (jnp.arange(A_Q_HEADS * hd)[:, None] // hd == jnp.arange(A_Q_HEADS * hd)[None, :] // hd).astype(F32) / hd

    sizes = (A_Q_HEADS * hd, A_KV_HEADS * hd, A_KV_HEADS * hd,
             SSM_HEADS * SSM_HEAD_DIM, SSM_HEADS * SSM_HEAD_DIM, SSM_GROUPS * SSM_STATE,
             SSM_GROUPS * SSM_STATE, SSM_HEADS, SSM_HEADS,
             DIFF_HEADS * 2 * DIFF_QK_DIM, DIFF_HEADS * 2 * DIFF_QK_DIM, DIFF_HEADS * DIFF_V_DIM,
             2 * SGU_GROUPS * SGU_GROUP_DIM)
    off = [0]
    for sz in sizes:
        off.append(off[-1] + sz)
    o_a0, o_b0, o_dt0, o_c0, o_d0, o_end = off[0], off[3], off[7], off[9], off[12], off[13]
    dt_pad = V7X_LANES - 2 * SSM_HEADS

    def lane_row(v, width=None):
        v = v.astype(F32).reshape(1, -1)
        if width is not None and v.shape[1] < width:
            v = jnp.pad(v, ((0, 0), (0, width - v.shape[1])))
        return v

    xt = x.reshape(t, dm)
    for i in range(depth):
        wi = w_in[i]
        wa = wi[:, o_a0:o_b0].astype(BF16)
        wb = jnp.pad(wi[:, o_b0:o_c0], ((0, 0), (0, dt_pad))).astype(BF16)
        wc = wi[:, o_c0:o_d0].astype(BF16)
        wd = wi[:, o_d0:o_end].astype(BF16)
        gmix = lane_row(g_mix[i])
        (qa, ka, va, z, xbc, dt, cq, ck, cv, du, dv) = _inproj(
            xt, gmix, wa, wb, wc, wd, rc, rs1, rs2,
            lane_row(jnp.tile(g_qnorm[i], A_Q_HEADS)), lane_row(jnp.tile(g_knorm[i], A_KV_HEADS)),
            bd, lane_row(g_sgu[i]), s, tm_proj)

        qt = jnp.transpose(_split(qa, b, s, A_Q_HEADS, hd), (0, 2, 3, 1))
        kh = jnp.transpose(_split(ka, b, s, A_KV_HEADS, hd), (0, 2, 1, 3))
        vt = jnp.transpose(_split(va, b, s, A_KV_HEADS, hd), (0, 2, 3, 1))
        o_a = _tokens(_attn_a(qt, kh, _with_ones(vt), tq_a, tk_a, ATTN_INNER))

        xa, dl = _ssd_prep(xbc, dt, conv_w[i].astype(F32), lane_row(conv_b[i]),
                           lane_row(jnp.concatenate([dt_bias_f[i], dt_bias_b[i]]), V7X_LANES), s, tm_prep)
        o_b = _ssd(xa, dl, lane_row(jnp.concatenate([a_log_f[i], a_log_b[i]]), V7X_LANES), z,
                   lane_row(jnp.repeat(d_skip[i], SSM_HEAD_DIM)), lane_row(g_ssm[i]), b, s, cps)

        lam_init = 0.8 - 0.6 * math.exp(-0.3 * i)
        lam = (jnp.exp(jnp.sum(lambda_q1[i].astype(F32) * lambda_k1[i].astype(F32)))
               - jnp.exp(jnp.sum(lambda_q2[i].astype(F32) * lambda_k2[i].astype(F32))) + lam_init).reshape(1, 1)
        cqt = jnp.transpose(_split(cq, b, s, DIFF_HEADS, 2, DIFF_QK_DIM), (0, 2, 3, 4, 1))
        ckh = jnp.transpose(_split(ck, b, s, DIFF_HEADS, 2, DIFF_QK_DIM), (0, 2, 3, 1, 4))
        cvt = jnp.transpose(_split(cv, b, s, DIFF_HEADS, DIFF_V_DIM), (0, 2, 3, 1))
        o_c = _tokens(_attn_c(cqt, ckh, _with_ones(cvt), gband, cfar, lam, g_diff[i].astype(F32).reshape(-1, 1),
                              1.0 - lam_init, tq_c, tk_c, ATTN_INNER))

        sgu_bias = jnp.repeat(b_spatial[i].T.astype(F32), SGU_GROUP_DIM, axis=1)
        o_d = _sgu(du, dv, w_spatial[i].astype(BF16), sgu_bias, nck)

        xt = _merge(xt, gmix, o_a, o_b, o_c, o_d, w_branch_gate[i].astype(BF16),
                    w_branch[i].astype(BF16), w_out[i].astype(BF16), tm_proj)

        h2, aff = _router(xt, lane_row(g_moe[i]), w_router[i].T.astype(F32), b, s, tm_proj)
        gates = _select(aff, cap)
        gates_tm = jnp.transpose(gates, (0, 2, 1)).reshape(t, N_EXPERTS)
        xt = _moe(xt, h2, gates_tm, w_exp_gate[i].astype(BF16), w_exp_up[i].astype(BF16),
                  w_exp_down[i].astype(BF16), tm_moe)

        xt = _ple(xt, p[i].reshape(t, -1), lane_row(g_ple[i]), w_ple_gate[i].astype(BF16),
                  w_ple[i].astype(BF16), lane_row(g_final), i == depth - 1, tm_proj)

    return xt.reshape(b, s, dm)
```

```python
import functools
import math

import jax
import jax.numpy as jnp
from jax import lax
from jax.experimental import pallas as pl
from jax.experimental.pallas import tpu as pltpu

F32 = jnp.float32
BF16 = jnp.bfloat16
HIGHEST = lax.Precision.HIGHEST

EPS = 1e-6
GRID_W = 64
A_Q_HEADS, A_KV_HEADS, A_HEAD_DIM = 4, 2, 64
ROPE_THETA = 10000.0
SSM_HEADS, SSM_HEAD_DIM, SSM_GROUPS, SSM_STATE, SSM_CONV, SSM_CHUNK = 4, 64, 2, 128, 5, 128
DIFF_HEADS, DIFF_QK_DIM, DIFF_V_DIM = 4, 32, 64
REL_BUCKETS, REL_MAX_DIST = 32, 128
SGU_GROUPS, SGU_GROUP_DIM, SGU_CHUNK = 4, 64, 128
N_EXPERTS, EC_CAPACITY = 16, 2
N_BRANCHES, BRANCH_WIDTH = 4, 256

V7X_LANES = 128
V7X_SUBLANES = 8
BF16_SUBLANE_PACK = 16
V7X_VMEM_LIMIT_BYTES = 56 * 1024 * 1024

NEG_BIG = -1e30
LOG2E = math.log2(math.e)
MOE_ROW_BLOCK = 128
ATTN_INNER = 4


def _cparams(sem):
    return pltpu.CompilerParams(dimension_semantics=sem, vmem_limit_bytes=V7X_VMEM_LIMIT_BYTES)


def _rms(x, g):
    return x * lax.rsqrt(jnp.mean(x * x, axis=-1, keepdims=True) + EPS) * g


def _dot(a, b):
    return jnp.dot(a, b, preferred_element_type=F32)


def _dot_nt(a, b):
    return lax.dot_general(a, b, (((1,), (1,)), ((), ())), preferred_element_type=F32)


def _dot_tn(a, b):
    return lax.dot_general(a, b, (((0,), (0,)), ((), ())), preferred_element_type=F32)


def _full(shape):
    n = len(shape)
    return pl.BlockSpec(shape, lambda *_: (0,) * n)


def _inproj_kernel(x_ref, gmix_ref, wa_ref, wb_ref, wc_ref, wd_ref, rc_ref, rs1_ref, rs2_ref,
                   gq_ref, gk_ref, bd_ref, gsgu_ref,
                   qa_ref, ka_ref, va_ref, z_ref, xbc_ref, dt_ref, cq_ref, ck_ref, cv_ref,
                   du_ref, dv_ref):
    hb = _rms(x_ref[...], gmix_ref[...]).astype(BF16)

    a = _dot(hb, wa_ref[...])
    nq = A_Q_HEADS * A_HEAD_DIM
    nk = A_KV_HEADS * A_HEAD_DIM
    q, k, v = a[:, :nq], a[:, nq:nq + nk], a[:, nq + nk:]
    bd = bd_ref[...]
    q = q * lax.rsqrt(jnp.dot(q * q, bd, precision=HIGHEST, preferred_element_type=F32) + EPS) * gq_ref[...]
    k = k * lax.rsqrt(jnp.dot(k * k, bd[:nk, :nk], precision=HIGHEST, preferred_element_type=F32) + EPS) * gk_ref[...]
    rc, rs1, rs2 = rc_ref[...], rs1_ref[...], rs2_ref[...]
    quarter = A_HEAD_DIM // 4

    def rope(t, c, s1, s2):
        w = t.shape[1]
        return t * c + pltpu.roll(t, w - quarter, 1) * s1 + pltpu.roll(t, quarter, 1) * s2

    k = rope(k, rc, rs1, rs2)
    q = rope(q, jnp.concatenate([rc, rc], axis=1), jnp.concatenate([rs1, rs1], axis=1),
             jnp.concatenate([rs2, rs2], axis=1))
    qa_ref[...] = (q * (A_HEAD_DIM ** -0.5 * LOG2E)).astype(BF16)
    ka_ref[...] = k.astype(BF16)
    va_ref[...] = v.astype(BF16)

    b = _dot(hb, wb_ref[...])
    dx = SSM_HEADS * SSM_HEAD_DIM
    conv_ch = dx + 2 * SSM_GROUPS * SSM_STATE
    z_ref[...] = b[:, :dx]
    xbc_ref[...] = b[:, dx:dx + conv_ch]
    dt_ref[...] = b[:, dx + conv_ch:]

    c = _dot(hb, wc_ref[...])
    nqc = DIFF_HEADS * 2 * DIFF_QK_DIM
    cq_ref[...] = (c[:, :nqc] * (DIFF_QK_DIM ** -0.5 * LOG2E)).astype(BF16)
    ck_ref[...] = c[:, nqc:2 * nqc].astype(BF16)
    cv_ref[...] = c[:, 2 * nqc:].astype(BF16)

    d = jax.nn.gelu(_dot(hb, wd_ref[...]))
    w = SGU_GROUPS * SGU_GROUP_DIM
    du_ref[...] = d[:, :w]
    dv_ref[...] = _rms(d[:, w:], gsgu_ref[...]).astype(BF16)


def _inproj(x, gmix, wa, wb, wc, wd, rc, rs1, rs2, gq, gk, bd, gsgu, s, tm):
    t, dm = x.shape
    nst = s // tm
    row = lambda w: pl.BlockSpec((tm, w), lambda i: (i, 0))
    tab = pl.BlockSpec((tm, rc.shape[1]), lambda i: (i % nst, 0))
    outs = [(256, BF16), (128, BF16), (128, BF16), (256, F32), (768, F32), (128, F32),
            (256, BF16), (256, BF16), (256, BF16), (256, F32), (256, BF16)]
    return pl.pallas_call(
        _inproj_kernel,
        grid=(t // tm,),
        in_specs=[row(dm), _full(gmix.shape), _full(wa.shape), _full(wb.shape), _full(wc.shape),
                  _full(wd.shape), tab, tab, tab, _full(gq.shape), _full(gk.shape), _full(bd.shape),
                  _full(gsgu.shape)],
        out_specs=[row(w) for w, _ in outs],
        out_shape=[jax.ShapeDtypeStruct((t, w), dt) for w, dt in outs],
        compiler_params=_cparams(("parallel",)),
        name="inproj",
    )(x, gmix, wa, wb, wc, wd, rc, rs1, rs2, gq, gk, bd, gsgu)


def _osm_tile(s, shift, vt, m_ref, acc_ref, c):
    m_old = m_ref[c]
    mt = jnp.max(s, axis=0, keepdims=True)
    if shift is not None:
        mt = mt + shift
    m_new = jnp.maximum(m_old, mt)
    p = jnp.exp2(s - (m_new if shift is None else m_new - shift))
    acc_ref[c] = jnp.exp2(m_old - m_new) * acc_ref[c] + _dot(vt, p.astype(BF16))
    m_ref[c] = m_new


def _osm_init(m_ref, acc_ref):
    m_ref[...] = jnp.full(m_ref.shape, NEG_BIG, F32)
    acc_ref[...] = jnp.zeros(acc_ref.shape, F32)


def _osm_result(acc_ref, c, dv):
    acc = acc_ref[c]
    return acc[:dv] / acc[dv:dv + 1]


def _attn_scores(tk, js, qs, k_of, s_ref, bias_tile=None):
    for u, j in enumerate(js):
        off = pl.multiple_of(j * tk, tk)
        for c, q in enumerate(qs):
            s = _dot(k_of(c, off), q)
            s_ref[u * len(qs) + c] = s if bias_tile is None else s + bias_tile


def _attn_values(tk, js, nmaps, vt_ref, s_ref, m_ref, acc_ref, shifts=None):
    for u, j in enumerate(js):
        off = pl.multiple_of(j * tk, tk)
        vt = vt_ref[:, pl.ds(off, tk)]
        for c in range(nmaps):
            _osm_tile(s_ref[u * nmaps + c], None if shifts is None else shifts[u], vt, m_ref, acc_ref, c)


def _pipelined(n, scores, values, s_refs, inner):
    last = jnp.maximum(n - 1, 0)
    span = 2 * inner
    scores(jnp.minimum(0, last), s_refs[0])

    def double(g):
        scores(jnp.minimum(g + 1, last), s_refs[1])
        values(g, s_refs[0])
        scores(jnp.minimum(g + 2, last), s_refs[0])
        values(g + 1, s_refs[1])

    def unrolled(t, carry):
        for w in range(inner):
            double(span * t + 2 * w)
        return carry

    lax.fori_loop(0, n // span, unrolled, 0)
    base = (n // span) * span

    def rolled(t, carry):
        double(base + 2 * t)
        return carry

    lax.fori_loop(0, (n - base) // 2, rolled, 0)

    @pl.when(n % 2 == 1)
    def _():
        values(n - 1, s_refs[0])


def _attn_a_kernel(tk, inner, qt_ref, k_ref, vt_ref, o_ref, m_ref, acc_ref, sa_ref, sb_ref):
    _osm_init(m_ref, acc_ref)
    qs = (qt_ref[...],)
    k_of = lambda c, off: k_ref[pl.ds(off, tk), :]
    _pipelined(k_ref.shape[0] // tk,
               lambda j, s_ref: _attn_scores(tk, [j], qs, k_of, s_ref),
               lambda j, s_ref: _attn_values(tk, [j], 1, vt_ref, s_ref, m_ref, acc_ref),
               (sa_ref, sb_ref), inner)
    o_ref[...] = _osm_result(acc_ref, 0, o_ref.shape[0]).astype(o_ref.dtype)


def _attn_a(qt, k, vt, tq, tk, inner):
    b, hq, d, s = qt.shape
    rep = hq // k.shape[1]
    da = vt.shape[2]
    return pl.pallas_call(
        functools.partial(_attn_a_kernel, tk, inner),
        grid=(b, hq, s // tq),
        in_specs=[pl.BlockSpec((None, None, d, tq), lambda bi, h, i: (bi, h, 0, i)),
                  pl.BlockSpec((None, None, s, d), lambda bi, h, i: (bi, h // rep, 0, 0)),
                  pl.BlockSpec((None, None, da, s), lambda bi, h, i: (bi, h // rep, 0, 0))],
        out_specs=pl.BlockSpec((None, None, d, tq), lambda bi, h, i: (bi, h, 0, i)),
        out_shape=jax.ShapeDtypeStruct(qt.shape, BF16),
        scratch_shapes=[pltpu.VMEM((1, 1, tq), F32), pltpu.VMEM((1, da, tq), F32),
                        pltpu.VMEM((1, tk, tq), F32), pltpu.VMEM((1, tk, tq), F32)],
        compiler_params=_cparams(("parallel", "parallel", "arbitrary")),
        name="attn_a",
    )(qt, k, vt)


def _attn_c_kernel(tk, inner, out_scale, qt_ref, k_ref, vt_ref, gband_ref, cfar_ref, lam_ref, gd_ref, o_ref,
                   m_ref, acc_ref, sa_ref, sb_ref):
    i = pl.program_id(2)
    tq = qt_ref.shape[2]
    nk = k_ref.shape[1] // tk
    nband = gband_ref.shape[0]
    dv = o_ref.shape[0]
    _osm_init(m_ref, acc_ref)
    qs = (qt_ref[0], qt_ref[1])
    k_of = lambda c, off: k_ref[c, pl.ds(off, tk), :]

    j_lo = (tq // tk) * i - 1
    n_left = jnp.clip(j_lo, 0, nk)
    j_hi = jnp.clip(j_lo + nband, 0, nk)
    n_far = n_left + nk - j_hi
    cf = cfar_ref[...]

    def far_tile(t):
        j = jnp.where(t < n_left, t, t - n_left + j_hi)
        return jnp.clip(j, 0, nk - 1), jnp.where(t < n_left, cf[:, 0:1], cf[:, 1:2])

    def far_scores(t, s_ref):
        _attn_scores(tk, [far_tile(t)[0]], qs, k_of, s_ref)

    def far_values(t, s_ref):
        j, sh = far_tile(t)
        _attn_values(tk, [j], 2, vt_ref, s_ref, m_ref, acc_ref, shifts=[sh])

    _pipelined(n_far, far_scores, far_values, (sa_ref, sb_ref), inner)

    for d in range(nband):
        j = j_lo + d

        @pl.when(jnp.logical_and(j >= 0, j < nk))
        def _():
            g = gband_ref[d]
            t = pltpu.roll(jnp.broadcast_to(g, (tk, g.shape[1])), tq + 1, 1, stride=1, stride_axis=0)
            _attn_scores(tk, [j], qs, k_of, sa_ref, bias_tile=t[:, :tq])
            _attn_values(tk, [j], 2, vt_ref, sa_ref, m_ref, acc_ref)

    o = _osm_result(acc_ref, 0, dv) - lam_ref[...] * _osm_result(acc_ref, 1, dv)
    o = o * lax.rsqrt(jnp.mean(o * o, axis=0, keepdims=True) + EPS) * gd_ref[...]
    o_ref[...] = (o * out_scale).astype(o_ref.dtype)


def _attn_c(qt, k, vt, gband, cfar, lam, gd, out_scale, tq, tk, inner):
    b, h, _, dqk, s = qt.shape
    dva = vt.shape[2]
    dv = gd.shape[0]
    nband = gband.shape[1]
    return pl.pallas_call(
        functools.partial(_attn_c_kernel, tk, inner, out_scale),
        grid=(b, h, s // tq),
        in_specs=[pl.BlockSpec((None, None, 2, dqk, tq), lambda bi, hi, i: (bi, hi, 0, 0, i)),
                  pl.BlockSpec((None, None, 2, s, dqk), lambda bi, hi, i: (bi, hi, 0, 0, 0)),
                  pl.BlockSpec((None, None, dva, s), lambda bi, hi, i: (bi, hi, 0, 0)),
                  pl.BlockSpec((None, nband, 1, tq + tk), lambda bi, hi, i: (hi, 0, 0, 0)),
                  pl.BlockSpec((None, 1, 2), lambda bi, hi, i: (hi, 0, 0)),
                  _full(lam.shape), _full(gd.shape)],
        out_specs=pl.BlockSpec((None, None, dv, tq), lambda bi, hi, i: (bi, hi, 0, i)),
        out_shape=jax.ShapeDtypeStruct((b, h, dv, s), BF16),
        scratch_shapes=[pltpu.VMEM((2, 1, tq), F32), pltpu.VMEM((2, dva, tq), F32),
                        pltpu.VMEM((2, tk, tq), F32), pltpu.VMEM((2, tk, tq), F32)],
        compiler_params=_cparams(("parallel", "parallel", "arbitrary")),
        name="attn_c",
    )(qt, k, vt, gband, cfar, lam, gd)


def _ssd_prep_kernel(nst, xc_ref, xp_ref, xn_ref, dt_ref, cw_ref, cb_ref, dtb_ref, xa_ref, dl_ref, ext_ref):
    i = pl.program_id(0)
    tm = xc_ref.shape[0]
    halo = V7X_SUBLANES
    pos = i % nst
    keep_prev = (pos != 0).astype(F32)
    keep_next = (pos != nst - 1).astype(F32)
    ext_ref[0:halo, :] = xp_ref[...] * keep_prev
    ext_ref[halo:halo + tm, :] = xc_ref[...]
    ext_ref[halo + tm:, :] = xn_ref[...] * keep_next
    pad_l = (SSM_CONV - 1) // 2
    cw = cw_ref[...]
    acc = jnp.zeros(xc_ref.shape, F32) + cb_ref[...]
    for t in range(SSM_CONV):
        acc = acc + ext_ref[halo - pad_l + t:halo - pad_l + t + tm, :] * cw[t:t + 1, :]
    xa_ref[...] = jax.nn.silu(acc)
    dl_ref[...] = jax.nn.softplus(dt_ref[...] + dtb_ref[...])


def _ssd_prep(xbc, dt, cw, cb, dtb, s, tm):
    t, c = xbc.shape
    nst = s // tm
    hb = tm // V7X_SUBLANES
    nhb = t // V7X_SUBLANES
    return pl.pallas_call(
        functools.partial(_ssd_prep_kernel, nst),
        grid=(t // tm,),
        in_specs=[pl.BlockSpec((tm, c), lambda i: (i, 0)),
                  pl.BlockSpec((V7X_SUBLANES, c), lambda i: (jnp.maximum(i * hb - 1, 0), 0)),
                  pl.BlockSpec((V7X_SUBLANES, c), lambda i: (jnp.minimum((i + 1) * hb, nhb - 1), 0)),
                  pl.BlockSpec((tm, dt.shape[1]), lambda i: (i, 0)),
                  _full(cw.shape), _full(cb.shape), _full(dtb.shape)],
        out_specs=[pl.BlockSpec((tm, c), lambda i: (i, 0)),
                   pl.BlockSpec((tm, dt.shape[1]), lambda i: (i, 0))],
        out_shape=[jax.ShapeDtypeStruct((t, c), F32), jax.ShapeDtypeStruct(dt.shape, F32)],
        scratch_shapes=[pltpu.VMEM((tm + 2 * V7X_SUBLANES, c), F32)],
        compiler_params=_cparams(("parallel",)),
        name="ssd_prep",
    )(xbc, xbc, xbc, dt, cw, cb, dtb)


def _ssd_chunk(reverse, xa, dl, alog, st_ref):
    ln = SSM_CHUNK
    row = lax.broadcasted_iota(jnp.int32, (ln, ln), 0)
    col = lax.broadcasted_iota(jnp.int32, (ln, ln), 1)
    tri = (col >= row) if reverse else (col <= row)
    eye = row == col
    last = 0 if reverse else ln - 1
    lane0 = SSM_HEADS if reverse else 0
    a_neg = -jnp.exp(alog)
    acs = jnp.dot(tri.astype(F32), dl * a_neg, precision=HIGHEST, preferred_element_type=F32)
    dx = SSM_HEADS * SSM_HEAD_DIM
    gw = SSM_STATE
    hpg = SSM_HEADS // SSM_GROUPS
    ys = []
    for g in range(SSM_GROUPS):
        bg = xa[:, dx + g * gw:dx + (g + 1) * gw].astype(BF16)
        cg = xa[:, dx + SSM_GROUPS * gw + g * gw:dx + SSM_GROUPS * gw + (g + 1) * gw].astype(BF16)
        cb = _dot_nt(cg, bg)
        for r in range(hpg):
            h = g * hpg + r
            ln_h = lane0 + h
            ac = acs[:, ln_h:ln_h + 1]
            ar = jnp.sum(jnp.where(eye, ac, 0.0), axis=0, keepdims=True)
            dec = jnp.exp(jnp.where(tri, ac - ar, NEG_BIG))
            xdt = xa[:, h * SSM_HEAD_DIM:(h + 1) * SSM_HEAD_DIM] * dl[:, ln_h:ln_h + 1]
            st = st_ref[h]
            y = _dot((cb * dec).astype(BF16), xdt.astype(BF16))
            y = y + _dot(cg, st.astype(BF16)) * jnp.exp(ac)
            a_last = acs[last:last + 1, ln_h:ln_h + 1]
            st_ref[h] = st * jnp.exp(a_last) + _dot_tn(bg, (xdt * jnp.exp(a_last - ac)).astype(BF16))
            ys.append(y)
    return jnp.concatenate(ys, axis=1)


def _ssd_fwd_kernel(cps, xa_ref, dl_ref, alog_ref, y_ref, st_ref):
    @pl.when(pl.program_id(1) == 0)
    def _():
        st_ref[...] = jnp.zeros(st_ref.shape, F32)

    for c in range(cps):
        sl = slice(c * SSM_CHUNK, (c + 1) * SSM_CHUNK)
        y_ref[sl, :] = _ssd_chunk(False, xa_ref[sl, :], dl_ref[sl, :], alog_ref[...], st_ref)


def _ssd_bwd_kernel(cps, xa_ref, dl_ref, alog_ref, yf_ref, z_ref, dsk_ref, g_ref, o_ref, st_ref):
    @pl.when(pl.program_id(1) == 0)
    def _():
        st_ref[...] = jnp.zeros(st_ref.shape, F32)

    dx = SSM_HEADS * SSM_HEAD_DIM
    for c in reversed(range(cps)):
        sl = slice(c * SSM_CHUNK, (c + 1) * SSM_CHUNK)
        xa = xa_ref[sl, :]
        yb = _ssd_chunk(True, xa, dl_ref[sl, :], alog_ref[...], st_ref)
        y = yf_ref[sl, :] + yb + dsk_ref[...] * xa[:, :dx]
        y = y * jax.nn.silu(z_ref[sl, :])
        o_ref[sl, :] = _rms(y, g_ref[...]).astype(o_ref.dtype)


def _ssd(xa, dl, alog, z, dsk, gssm, b, s, cps):
    t, c = xa.shape
    tm = cps * SSM_CHUNK
    nblk = s // tm
    dx = SSM_HEADS * SSM_HEAD_DIM
    st = pltpu.VMEM((SSM_HEADS, SSM_STATE, SSM_HEAD_DIM), F32)
    fw = lambda w: pl.BlockSpec((tm, w), lambda bi, ci: (bi * nblk + ci, 0))
    bw = lambda w: pl.BlockSpec((tm, w), lambda bi, ci: (bi * nblk + nblk - 1 - ci, 0))
    yf = pl.pallas_call(
        functools.partial(_ssd_fwd_kernel, cps),
        grid=(b, nblk),
        in_specs=[fw(c), fw(dl.shape[1]), _full(alog.shape)],
        out_specs=fw(dx),
        out_shape=jax.ShapeDtypeStruct((t, dx), F32),
        scratch_shapes=[st],
        compiler_params=_cparams(("parallel", "arbitrary")),
        name="ssd_fwd",
    )(xa, dl, alog)
    return pl.pallas_call(
        functools.partial(_ssd_bwd_kernel, cps),
        grid=(b, nblk),
        in_specs=[bw(c), bw(dl.shape[1]), _full(alog.shape), bw(dx), bw(dx), _full(dsk.shape),
                  _full(gssm.shape)],
        out_specs=bw(dx),
        out_shape=jax.ShapeDtypeStruct((t, dx), BF16),
        scratch_shapes=[st],
        compiler_params=_cparams(("parallel", "arbitrary")),
        name="ssd_bwd",
    )(xa, dl, alog, yf, z, dsk, gssm)


def _sgu_kernel(nck, u_ref, v_ref, w_ref, bias_ref, o_ref):
    for c in range(nck):
        sl = slice(c * SGU_CHUNK, (c + 1) * SGU_CHUNK)
        v = v_ref[sl, :]
        sv = [_dot(w_ref[g], v[:, g * SGU_GROUP_DIM:(g + 1) * SGU_GROUP_DIM]) for g in range(SGU_GROUPS)]
        o_ref[sl, :] = (u_ref[sl, :] * (jnp.concatenate(sv, axis=1) + bias_ref[...])).astype(o_ref.dtype)


def _sgu(u, v, w, bias, nck):
    t, c = u.shape
    tm = nck * SGU_CHUNK
    return pl.pallas_call(
        functools.partial(_sgu_kernel, nck),
        grid=(t // tm,),
        in_specs=[pl.BlockSpec((tm, c), lambda i: (i, 0)), pl.BlockSpec((tm, c), lambda i: (i, 0)),
                  _full(w.shape), _full(bias.shape)],
        out_specs=pl.BlockSpec((tm, c), lambda i: (i, 0)),
        out_shape=jax.ShapeDtypeStruct((t, c), BF16),
        compiler_params=_cparams(("parallel",)),
        name="sgu",
    )(u, v, w, bias)


def _merge_kernel(x_ref, gmix_ref, oa_ref, ob_ref, oc_ref, od_ref, wg_ref, wbr_ref, wo_ref, o_ref):
    x = x_ref[...]
    dm = x.shape[1]
    hb = _rms(x, gmix_ref[...]).astype(BF16)
    merged = jnp.zeros(x.shape, F32)
    for n, br in enumerate((oa_ref, ob_ref, oc_ref, od_ref)):
        gate = jax.nn.sigmoid(_dot(hb, wg_ref[:, n * dm:(n + 1) * dm]))
        merged = merged + gate * _dot(br[...], wbr_ref[n])
    o_ref[...] = x + _dot(merged.astype(BF16), wo_ref[...])


def _merge(x, gmix, oa, ob, oc, od, wg, wbr, wo, tm):
    t, dm = x.shape
    row = lambda w: pl.BlockSpec((tm, w), lambda i: (i, 0))
    return pl.pallas_call(
        _merge_kernel,
        grid=(t // tm,),
        in_specs=[row(dm), _full(gmix.shape), row(256), row(256), row(256), row(256),
                  _full(wg.shape), _full(wbr.shape), _full(wo.shape)],
        out_specs=row(dm),
        out_shape=jax.ShapeDtypeStruct((t, dm), F32),
        compiler_params=_cparams(("parallel",)),
        name="merge",
    )(x, gmix, oa, ob, oc, od, wg, wbr, wo)


def _router_kernel(x_ref, g_ref, wrt_ref, h_ref, aff_ref):
    h = _rms(x_ref[...], g_ref[...])
    h_ref[...] = h.astype(BF16)
    logits = lax.dot_general(wrt_ref[...], h, (((1,), (1,)), ((), ())), precision=HIGHEST,
                             preferred_element_type=F32)
    e = jnp.exp(logits - jnp.max(logits, axis=0, keepdims=True))
    aff_ref[...] = e / jnp.sum(e, axis=0, keepdims=True)


def _router(x, g, wrt, b, s, tm):
    t, dm = x.shape
    ne = wrt.shape[0]
    nst = s // tm
    return pl.pallas_call(
        _router_kernel,
        grid=(t // tm,),
        in_specs=[pl.BlockSpec((tm, dm), lambda i: (i, 0)), _full(g.shape), _full(wrt.shape)],
        out_specs=[pl.BlockSpec((tm, dm), lambda i: (i, 0)),
                   pl.BlockSpec((None, ne, tm), lambda i: (i // nst, 0, i % nst))],
        out_shape=[jax.ShapeDtypeStruct((t, dm), BF16), jax.ShapeDtypeStruct((b, ne, s), F32)],
        compiler_params=_cparams(("parallel",)),
        name="router",
    )(x, g, wrt)


def _select_kernel(cap, aff_ref, gate_ref, pos_ref):
    ne, s = aff_ref.shape
    bits = pltpu.bitcast(aff_ref[...], jnp.int32)
    capf = jnp.float32(cap)

    def bisect(i, thr):
        cand = thr | lax.shift_left(jnp.int32(1), 30 - i)
        cnt = jnp.sum((bits >= cand).astype(F32), axis=1, keepdims=True)
        return jnp.where(cnt >= capf, cand, thr)

    thr = lax.fori_loop(0, 31, bisect, jnp.zeros((ne, 1), jnp.int32))
    need = capf - jnp.sum((bits > thr).astype(F32), axis=1, keepdims=True)
    lanes = V7X_LANES
    r = lax.broadcasted_iota(jnp.int32, (lanes, lanes), 0)
    c = lax.broadcasted_iota(jnp.int32, (lanes, lanes), 1)
    before = (r < c).astype(BF16)

    def block(jb, carry):
        n_eq, n_keep = carry
        off = pl.multiple_of(jb * lanes, lanes)
        a = aff_ref[:, pl.ds(off, lanes)]
        bb = pltpu.bitcast(a, jnp.int32)
        eq = bb == thr
        eqf = eq.astype(F32)
        keep = jnp.logical_or(bb > thr, jnp.logical_and(eq, n_eq + _dot(eqf.astype(BF16), before) < need))
        keepf = keep.astype(F32)
        gate_ref[:, pl.ds(off, lanes)] = jnp.where(keep, a, 0.0)
        pos_ref[:, pl.ds(off, lanes)] = jnp.where(keep, n_keep + _dot(keepf.astype(BF16), before), -1.0)
        return n_eq + jnp.sum(eqf, axis=1, keepdims=True), n_keep + jnp.sum(keepf, axis=1, keepdims=True)

    zero = jnp.zeros((ne, 1), F32)
    lax.fori_loop(0, s // lanes, block, (zero, zero))


def _select(aff, cap):
    b, ne, s = aff.shape
    spec = pl.BlockSpec((None, ne, s), lambda i: (i, 0, 0))
    return pl.pallas_call(
        functools.partial(_select_kernel, cap),
        grid=(b,),
        in_specs=[spec],
        out_specs=[spec, spec],
        out_shape=[jax.ShapeDtypeStruct(aff.shape, F32), jax.ShapeDtypeStruct(aff.shape, F32)],
        compiler_params=_cparams(("parallel",)),
        name="moe_select",
    )(aff)


def _moe_kernel(rb, cnt_ref, base_ref, h_ref, posr_ref, posc_ref, gates_ref, wg_ref, wu_ref, wd_ref, o_ref):
    i = pl.program_id(0)
    e = pl.program_id(1)

    @pl.when(e == 0)
    def _():
        o_ref[...] = jnp.zeros(o_ref.shape, F32)

    tt = h_ref.shape[0]
    base = base_ref[i, e].astype(F32)
    lane = lax.broadcasted_iota(jnp.int32, gates_ref.shape, 1)
    col = lambda ref: jnp.sum(jnp.where(lane == e, ref[...], 0.0), axis=1, keepdims=True)
    gcol = col(gates_ref)
    posc = col(posc_ref) - base
    posr = posr_ref[...] - base
    h = h_ref[...]

    def block(bi, carry):
        lo = (bi * rb).astype(F32)
        slot_r = lax.broadcasted_iota(jnp.int32, (rb, 1), 0).astype(F32) + lo
        slot_c = lax.broadcasted_iota(jnp.int32, (1, rb), 1).astype(F32) + lo
        xc = _dot((posr == slot_r).astype(BF16), h).astype(BF16)
        hid = jax.nn.silu(_dot(xc, wg_ref[...])) * _dot(xc, wu_ref[...])
        y = _dot(hid.astype(BF16), wd_ref[...])
        o_ref[...] += _dot((posc == slot_c).astype(BF16), y.astype(BF16)) * gcol
        return carry

    lax.fori_loop(0, (cnt_ref[i, e] + rb - 1) // rb, block, 0)


def _moe(h, pos_lane, pos_tm, gates_tm, cnt, base, wg, wu, wd, tt, rb):
    t, dm = h.shape
    ne, _, dff = wg.shape
    s = pos_lane.shape[2]
    nst = s // tt
    return pl.pallas_call(
        functools.partial(_moe_kernel, rb),
        grid_spec=pltpu.PrefetchScalarGridSpec(
            num_scalar_prefetch=2,
            grid=(t // tt, ne),
            in_specs=[pl.BlockSpec((tt, dm), lambda i, e, *_: (i, 0)),
                      pl.BlockSpec((None, 1, tt), lambda i, e, *_: ((i // nst) * ne + e, 0, i % nst)),
                      pl.BlockSpec((tt, ne), lambda i, e, *_: (i, 0)),
                      pl.BlockSpec((tt, ne), lambda i, e, *_: (i, 0)),
                      pl.BlockSpec((None, dm, dff), lambda i, e, *_: (e, 0, 0)),
                      pl.BlockSpec((None, dm, dff), lambda i, e, *_: (e, 0, 0)),
                      pl.BlockSpec((None, dff, dm), lambda i, e, *_: (e, 0, 0))],
            out_specs=pl.BlockSpec((tt, dm), lambda i, e, *_: (i, 0))),
        out_shape=jax.ShapeDtypeStruct((t, dm), F32),
        compiler_params=_cparams(("parallel", "arbitrary")),
        name="moe_ffn",
    )(cnt, base, h, pos_lane, pos_tm, gates_tm, wg, wu, wd)


def _ple_kernel(final, x_ref, moe_ref, p_ref, g_ref, wpg_ref, wp_ref, gf_ref, o_ref):
    x = x_ref[...] + moe_ref[...]
    gate = jax.nn.sigmoid(_dot(_rms(x, g_ref[...]).astype(BF16), wpg_ref[...]))
    y = x + gate * _dot(p_ref[...].astype(BF16), wp_ref[...])
    o_ref[...] = _rms(y, gf_ref[...]) if final else y


def _ple(x, moe, p, g, wpg, wp, gf, final, tm):
    t, dm = x.shape
    return pl.pallas_call(
        functools.partial(_ple_kernel, final),
        grid=(t // tm,),
        in_specs=[pl.BlockSpec((tm, dm), lambda i: (i, 0)), pl.BlockSpec((tm, dm), lambda i: (i, 0)),
                  pl.BlockSpec((tm, p.shape[1]), lambda i: (i, 0)),
                  _full(g.shape), _full(wpg.shape), _full(wp.shape), _full(gf.shape)],
        out_specs=pl.BlockSpec((tm, dm), lambda i: (i, 0)),
        out_shape=jax.ShapeDtypeStruct((t, dm), F32),
        compiler_params=_cparams(("parallel",)),
        name="ple",
    )(x, moe, p, g, wpg, wp, gf)


def _rope_tables(s):
    quarter = A_HEAD_DIM // 4
    half = A_HEAD_DIM // 2
    pos = jnp.arange(s, dtype=jnp.int32)
    freqs = ROPE_THETA ** (-jnp.arange(0, half, 2, dtype=F32) / half)
    ang_r = (pos // GRID_W).astype(F32)[:, None] * freqs[None, :]
    ang_c = (pos % GRID_W).astype(F32)[:, None] * freqs[None, :]
    zero = jnp.zeros((s, quarter), F32)
    cos = jnp.concatenate([jnp.cos(ang_r)] * 2 + [jnp.cos(ang_c)] * 2, axis=1)
    s1 = jnp.concatenate([-jnp.sin(ang_r), zero, -jnp.sin(ang_c), zero], axis=1)
    s2 = jnp.concatenate([zero, jnp.sin(ang_r), zero, jnp.sin(ang_c)], axis=1)
    two = lambda a: jnp.concatenate([a, a], axis=1)
    return two(cos), two(s1), two(s2)


def _t5_bucket(rel):
    nb = REL_BUCKETS // 2
    max_exact = nb // 2
    ret = jnp.where(rel > 0, nb, 0)
    r = jnp.abs(rel)
    rf = jnp.maximum(r, 1).astype(F32)
    large = max_exact + (jnp.log(rf / max_exact) / math.log(REL_MAX_DIST / max_exact)
                         * (nb - max_exact)).astype(jnp.int32)
    large = jnp.minimum(large, nb - 1)
    return ret + jnp.where(r < max_exact, r, large)


def _bias_tables(rel_bias, tq, tk):
    assert tk >= REL_MAX_DIST and tq % tk == 0
    rep = tq // tk
    offs = jnp.arange(-1, rep + 1, dtype=jnp.int32) * tk
    u = jnp.arange(tq + tk, dtype=jnp.int32)
    rel = offs[:, None] + (tk - 1) - u[None, :]
    gband = jnp.transpose(rel_bias[_t5_bucket(rel)], (2, 0, 1))[:, :, None, :].astype(F32) * LOG2E
    far = jnp.array([-REL_MAX_DIST, REL_MAX_DIST], dtype=jnp.int32)
    cfar = jnp.transpose(rel_bias[_t5_bucket(far)], (1, 0))[:, None, :].astype(F32) * LOG2E
    return gband, cfar


def _tile(n, pref):
    t = min(n, pref)
    assert n % t == 0
    return t


def _split(a, b, s, *dims):
    return a.reshape(b, s, *dims)


def _with_ones(vt):
    ones = jnp.ones(vt.shape[:-2] + (BF16_SUBLANE_PACK, vt.shape[-1]), vt.dtype)
    return jnp.concatenate([vt, ones], axis=-2)


def _tokens(ot):
    b, h, d, s = ot.shape
    return jnp.transpose(ot, (0, 3, 1, 2)).reshape(b * s, h * d)


def kernel(x, p, rel_bias, g_mix, w_in, g_qnorm, g_knorm, conv_w, conv_b, dt_bias_f, dt_bias_b, a_log_f, a_log_b, d_skip, g_ssm, lambda_q1, lambda_k1, lambda_q2, lambda_k2, g_diff, g_sgu, w_spatial, b_spatial, w_branch, w_branch_gate, w_out, g_moe, w_router, w_exp_gate, w_exp_up, w_exp_down, g_ple, w_ple_gate, w_ple, g_final):
    b, s, dm = x.shape
    depth = w_in.shape[0]
    t = b * s
    cap = EC_CAPACITY * s // N_EXPERTS

    tm_proj = _tile(s, 512)
    tm_prep = _tile(s, 1024)
    tq_a, tk_a = _tile(s, 512), _tile(s, 512)
    tq_c, tk_c = _tile(s, 512), _tile(s, 512)
    cps = _tile(s // SSM_CHUNK, 4)
    nck = _tile(s // SGU_CHUNK, 8)
    tt_moe = _tile(s, 1024)

    rc, rs1, rs2 = _rope_tables(s)
    gband, cfar = _bias_tables(rel_bias, tq_c, tk_c)
    hd = A_HEAD_DIM
    bd = (jnp.arange(A_Q_HEADS * hd)[:, None] // hd == jnp.arange(A_Q_HEADS * hd)[None, :] // hd).astype(F32) / hd

    sizes = (A_Q_HEADS * hd, A_KV_HEADS * hd, A_KV_HEADS * hd,
             SSM_HEADS * SSM_HEAD_DIM, SSM_HEADS * SSM_HEAD_DIM, SSM_GROUPS * SSM_STATE,
             SSM_GROUPS * SSM_STATE, SSM_HEADS, SSM_HEADS,
             DIFF_HEADS * 2 * DIFF_QK_DIM, DIFF_HEADS * 2 * DIFF_QK_DIM, DIFF_HEADS * DIFF_V_DIM,
             2 * SGU_GROUPS * SGU_GROUP_DIM)
    off = [0]
    for sz in sizes:
        off.append(off[-1] + sz)
    o_a0, o_b0, o_dt0, o_c0, o_d0, o_end = off[0], off[3], off[7], off[9], off[12], off[13]
    dt_pad = V7X_LANES - 2 * SSM_HEADS

    def lane_row(v, width=None):
        v = v.astype(F32).reshape(1, -1)
        if width is not None and v.shape[1] < width:
            v = jnp.pad(v, ((0, 0), (0, width - v.shape[1])))
        return v

    xt = x.reshape(t, dm)
    for i in range(depth):
        wi = w_in[i]
        wa = wi[:, o_a0:o_b0].astype(BF16)
        wb = jnp.pad(wi[:, o_b0:o_c0], ((0, 0), (0, dt_pad))).astype(BF16)
        wc = wi[:, o_c0:o_d0].astype(BF16)
        wd = wi[:, o_d0:o_end].astype(BF16)
        gmix = lane_row(g_mix[i])
        (qa, ka, va, z, xbc, dt, cq, ck, cv, du, dv) = _inproj(
            xt, gmix, wa, wb, wc, wd, rc, rs1, rs2,
            lane_row(jnp.tile(g_qnorm[i], A_Q_HEADS)), lane_row(jnp.tile(g_knorm[i], A_KV_HEADS)),
            bd, lane_row(g_sgu[i]), s, tm_proj)

        qt = jnp.transpose(_split(qa, b, s, A_Q_HEADS, hd), (0, 2, 3, 1))
        kh = jnp.transpose(_split(ka, b, s, A_KV_HEADS, hd), (0, 2, 1, 3))
        vt = jnp.transpose(_split(va, b, s, A_KV_HEADS, hd), (0, 2, 3, 1))
        o_a = _tokens(_attn_a(qt, kh, _with_ones(vt), tq_a, tk_a, ATTN_INNER))

        xa, dl = _ssd_prep(xbc, dt, conv_w[i].astype(F32), lane_row(conv_b[i]),
                           lane_row(jnp.concatenate([dt_bias_f[i], dt_bias_b[i]]), V7X_LANES), s, tm_prep)
        o_b = _ssd(xa, dl, lane_row(jnp.concatenate([a_log_f[i], a_log_b[i]]), V7X_LANES), z,
                   lane_row(jnp.repeat(d_skip[i], SSM_HEAD_DIM)), lane_row(g_ssm[i]), b, s, cps)

        lam_init = 0.8 - 0.6 * math.exp(-0.3 * i)
        lam = (jnp.exp(jnp.sum(lambda_q1[i].astype(F32) * lambda_k1[i].astype(F32)))
               - jnp.exp(jnp.sum(lambda_q2[i].astype(F32) * lambda_k2[i].astype(F32))) + lam_init).reshape(1, 1)
        cqt = jnp.transpose(_split(cq, b, s, DIFF_HEADS, 2, DIFF_QK_DIM), (0, 2, 3, 4, 1))
        ckh = jnp.transpose(_split(ck, b, s, DIFF_HEADS, 2, DIFF_QK_DIM), (0, 2, 3, 1, 4))
        cvt = jnp.transpose(_split(cv, b, s, DIFF_HEADS, DIFF_V_DIM), (0, 2, 3, 1))
        o_c = _tokens(_attn_c(cqt, ckh, _with_ones(cvt), gband, cfar, lam, g_diff[i].astype(F32).reshape(-1, 1),
                              1.0 - lam_init, tq_c, tk_c, ATTN_INNER))

        sgu_bias = jnp.repeat(b_spatial[i].T.astype(F32), SGU_GROUP_DIM, axis=1)
        o_d = _sgu(du, dv, w_spatial[i].astype(BF16), sgu_bias, nck)

        xt = _merge(xt, gmix, o_a, o_b, o_c, o_d, w_branch_gate[i].astype(BF16),
                    w_branch[i].astype(BF16), w_out[i].astype(BF16), tm_proj)

        h2, aff = _router(xt, lane_row(g_moe[i]), w_router[i].T.astype(F32), b, s, tm_proj)
        gates, pos = _select(aff, cap)
        token_major = lambda a: jnp.transpose(a, (0, 2, 1)).reshape(t, N_EXPERTS)
        cnt = jnp.sum((pos >= 0).reshape(b, N_EXPERTS, s // tt_moe, tt_moe), axis=3, dtype=jnp.int32)
        tile_major = lambda a: jnp.transpose(a, (0, 2, 1)).reshape(t // tt_moe, N_EXPERTS)
        moe = _moe(h2, pos.reshape(b * N_EXPERTS, 1, s), token_major(pos), token_major(gates),
                   tile_major(cnt), tile_major(jnp.cumsum(cnt, axis=2) - cnt),
                   w_exp_gate[i].astype(BF16), w_exp_up[i].astype(BF16), w_exp_down[i].astype(BF16),
                   tt_moe, MOE_ROW_BLOCK)

        xt = _ple(xt, moe, p[i].reshape(t, -1), lane_row(g_ple[i]), w_ple_gate[i].astype(BF16),
                  w_ple[i].astype(BF16), lane_row(g_final), i == depth - 1, tm_proj)

    return xt.reshape(b, s, dm)
```

```python
import functools
import math

import jax
import jax.numpy as jnp
from jax import lax
from jax.experimental import pallas as pl
from jax.experimental.pallas import tpu as pltpu

F32 = jnp.float32
BF16 = jnp.bfloat16
HIGHEST = lax.Precision.HIGHEST

EPS = 1e-6
GRID_W = 64
A_Q_HEADS, A_KV_HEADS, A_HEAD_DIM = 4, 2, 64
ROPE_THETA = 10000.0
SSM_HEADS, SSM_HEAD_DIM, SSM_GROUPS, SSM_STATE, SSM_CONV, SSM_CHUNK = 4, 64, 2, 128, 5, 128
DIFF_HEADS, DIFF_QK_DIM, DIFF_V_DIM = 4, 32, 64
REL_BUCKETS, REL_MAX_DIST = 32, 128
SGU_GROUPS, SGU_GROUP_DIM, SGU_CHUNK = 4, 64, 128
N_EXPERTS, EC_CAPACITY = 16, 2
N_BRANCHES, BRANCH_WIDTH = 4, 256

V7X_LANES = 128
V7X_SUBLANES = 8
BF16_SUBLANE_PACK = 16
V7X_VMEM_LIMIT_BYTES = 56 * 1024 * 1024

NEG_BIG = -1e30
LOG2E = math.log2(math.e)
MOE_ROW_BLOCK = 128
ATTN_INNER = 4


def _cparams(sem):
    return pltpu.CompilerParams(dimension_semantics=sem, vmem_limit_bytes=V7X_VMEM_LIMIT_BYTES)


def _rms(x, g):
    return x * lax.rsqrt(jnp.mean(x * x, axis=-1, keepdims=True) + EPS) * g


def _dot(a, b):
    return jnp.dot(a, b, preferred_element_type=F32)


def _dot_nt(a, b):
    return lax.dot_general(a, b, (((1,), (1,)), ((), ())), preferred_element_type=F32)


def _dot_tn(a, b):
    return lax.dot_general(a, b, (((0,), (0,)), ((), ())), preferred_element_type=F32)


def _full(shape):
    n = len(shape)
    return pl.BlockSpec(shape, lambda *_: (0,) * n)


def _inproj_kernel(x_ref, gmix_ref, wa_ref, wb_ref, wc_ref, wd_ref, rc_ref, rs1_ref, rs2_ref,
                   gq_ref, gk_ref, bd_ref, gsgu_ref,
                   qa_ref, ka_ref, va_ref, z_ref, xbc_ref, dt_ref, cq_ref, ck_ref, cv_ref,
                   du_ref, dv_ref):
    hb = _rms(x_ref[...], gmix_ref[...]).astype(BF16)

    a = _dot(hb, wa_ref[...])
    nq = A_Q_HEADS * A_HEAD_DIM
    nk = A_KV_HEADS * A_HEAD_DIM
    q, k, v = a[:, :nq], a[:, nq:nq + nk], a[:, nq + nk:]
    bd = bd_ref[...]
    q = q * lax.rsqrt(jnp.dot(q * q, bd, precision=HIGHEST, preferred_element_type=F32) + EPS) * gq_ref[...]
    k = k * lax.rsqrt(jnp.dot(k * k, bd[:nk, :nk], precision=HIGHEST, preferred_element_type=F32) + EPS) * gk_ref[...]
    rc, rs1, rs2 = rc_ref[...], rs1_ref[...], rs2_ref[...]
    quarter = A_HEAD_DIM // 4

    def rope(t, c, s1, s2):
        w = t.shape[1]
        return t * c + pltpu.roll(t, w - quarter, 1) * s1 + pltpu.roll(t, quarter, 1) * s2

    k = rope(k, rc, rs1, rs2)
    q = rope(q, jnp.concatenate([rc, rc], axis=1), jnp.concatenate([rs1, rs1], axis=1),
             jnp.concatenate([rs2, rs2], axis=1))
    qa_ref[...] = (q * (A_HEAD_DIM ** -0.5 * LOG2E)).astype(BF16)
    ka_ref[...] = k.astype(BF16)
    va_ref[...] = v.astype(BF16)

    b = _dot(hb, wb_ref[...])
    dx = SSM_HEADS * SSM_HEAD_DIM
    conv_ch = dx + 2 * SSM_GROUPS * SSM_STATE
    z_ref[...] = b[:, :dx]
    xbc_ref[...] = b[:, dx:dx + conv_ch]
    dt_ref[...] = b[:, dx + conv_ch:]

    c = _dot(hb, wc_ref[...])
    nqc = DIFF_HEADS * 2 * DIFF_QK_DIM
    cq_ref[...] = (c[:, :nqc] * (DIFF_QK_DIM ** -0.5 * LOG2E)).astype(BF16)
    ck_ref[...] = c[:, nqc:2 * nqc].astype(BF16)
    cv_ref[...] = c[:, 2 * nqc:].astype(BF16)

    d = jax.nn.gelu(_dot(hb, wd_ref[...]))
    w = SGU_GROUPS * SGU_GROUP_DIM
    du_ref[...] = d[:, :w]
    dv_ref[...] = _rms(d[:, w:], gsgu_ref[...]).astype(BF16)


def _inproj(x, gmix, wa, wb, wc, wd, rc, rs1, rs2, gq, gk, bd, gsgu, s, tm):
    t, dm = x.shape
    nst = s // tm
    row = lambda w: pl.BlockSpec((tm, w), lambda i: (i, 0))
    tab = pl.BlockSpec((tm, rc.shape[1]), lambda i: (i % nst, 0))
    outs = [(256, BF16), (128, BF16), (128, BF16), (256, F32), (768, F32), (128, F32),
            (256, BF16), (256, BF16), (256, BF16), (256, F32), (256, BF16)]
    return pl.pallas_call(
        _inproj_kernel,
        grid=(t // tm,),
        in_specs=[row(dm), _full(gmix.shape), _full(wa.shape), _full(wb.shape), _full(wc.shape),
                  _full(wd.shape), tab, tab, tab, _full(gq.shape), _full(gk.shape), _full(bd.shape),
                  _full(gsgu.shape)],
        out_specs=[row(w) for w, _ in outs],
        out_shape=[jax.ShapeDtypeStruct((t, w), dt) for w, dt in outs],
        compiler_params=_cparams(("parallel",)),
        name="inproj",
    )(x, gmix, wa, wb, wc, wd, rc, rs1, rs2, gq, gk, bd, gsgu)


def _osm_tile(s, mt, shift, vt, m_ref, acc_ref, c):
    m_old = m_ref[c]
    if shift is not None:
        mt = mt + shift
    m_new = jnp.maximum(m_old, mt)
    p = jnp.exp2(s - (m_new if shift is None else m_new - shift))
    acc_ref[c] = jnp.exp2(m_old - m_new) * acc_ref[c] + _dot(vt, p.astype(BF16))
    m_ref[c] = m_new


def _osm_init(m_ref, acc_ref):
    m_ref[...] = jnp.full(m_ref.shape, NEG_BIG, F32)
    acc_ref[...] = jnp.zeros(acc_ref.shape, F32)


def _osm_result(acc_ref, c, dv):
    acc = acc_ref[c]
    return acc[:dv] / acc[dv:dv + 1]


def _attn_scores(tk, j, qt_ref, k_of, buf, bias_tile=None):
    s_ref, mx_ref = buf
    off = pl.multiple_of(j * tk, tk)
    for c in range(qt_ref.shape[0]):
        s = _dot(k_of(c, off), qt_ref[c])
        if bias_tile is not None:
            s = s + bias_tile
        s_ref[c] = s
        mx_ref[c] = jnp.max(s, axis=0, keepdims=True)


def _attn_values(tk, j, nmaps, vt_ref, buf, m_ref, acc_ref, shift=None):
    s_ref, mx_ref = buf
    vt = vt_ref[:, pl.ds(pl.multiple_of(j * tk, tk), tk)]
    for c in range(nmaps):
        _osm_tile(s_ref[c], mx_ref[c], shift, vt, m_ref, acc_ref, c)


def _pipelined(n, scores, values, s_refs, inner):
    last = jnp.maximum(n - 1, 0)
    span = 2 * inner
    scores(jnp.minimum(0, last), s_refs[0])

    def double(g):
        scores(jnp.minimum(g + 1, last), s_refs[1])
        values(g, s_refs[0])
        scores(jnp.minimum(g + 2, last), s_refs[0])
        values(g + 1, s_refs[1])

    def unrolled(t, carry):
        for w in range(inner):
            double(span * t + 2 * w)
        return carry

    lax.fori_loop(0, n // span, unrolled, 0)
    base = (n // span) * span

    def rolled(t, carry):
        double(base + 2 * t)
        return carry

    lax.fori_loop(0, (n - base) // 2, rolled, 0)

    @pl.when(n % 2 == 1)
    def _():
        values(n - 1, s_refs[0])


def _attn_a_kernel(tk, inner, qt_ref, k_ref, vt_ref, o_ref, m_ref, acc_ref, sa_ref, sb_ref, mxa_ref, mxb_ref):
    _osm_init(m_ref, acc_ref)
    nq = qt_ref.shape[0]
    k_of = lambda c, off: k_ref[pl.ds(off, tk), :]
    _pipelined(k_ref.shape[0] // tk,
               lambda j, buf: _attn_scores(tk, j, qt_ref, k_of, buf),
               lambda j, buf: _attn_values(tk, j, nq, vt_ref, buf, m_ref, acc_ref),
               ((sa_ref, mxa_ref), (sb_ref, mxb_ref)), inner)
    for c in range(nq):
        o_ref[c] = _osm_result(acc_ref, c, o_ref.shape[1]).astype(o_ref.dtype)


def _attn_a(qt, k, vt, tq, tk, inner):
    b, hq, d, s = qt.shape
    hkv = k.shape[1]
    rep = hq // hkv
    da = vt.shape[2]
    qspec = pl.BlockSpec((None, rep, d, tq), lambda bi, g, i: (bi, g, 0, i))
    return pl.pallas_call(
        functools.partial(_attn_a_kernel, tk, inner),
        grid=(b, hkv, s // tq),
        in_specs=[qspec,
                  pl.BlockSpec((None, None, s, d), lambda bi, g, i: (bi, g, 0, 0)),
                  pl.BlockSpec((None, None, da, s), lambda bi, g, i: (bi, g, 0, 0))],
        out_specs=qspec,
        out_shape=jax.ShapeDtypeStruct(qt.shape, BF16),
        scratch_shapes=[pltpu.VMEM((rep, 1, tq), F32), pltpu.VMEM((rep, da, tq), F32),
                        pltpu.VMEM((rep, tk, tq), F32), pltpu.VMEM((rep, tk, tq), F32),
                        pltpu.VMEM((rep, 1, tq), F32), pltpu.VMEM((rep, 1, tq), F32)],
        compiler_params=_cparams(("parallel", "parallel", "arbitrary")),
        name="attn_a",
    )(qt, k, vt)


def _attn_c_kernel(tk, inner, out_scale, qt_ref, k_ref, vt_ref, gband_ref, cfar_ref, lam_ref, gd_ref, o_ref,
                   m_ref, acc_ref, sa_ref, sb_ref, mxa_ref, mxb_ref):
    i = pl.program_id(2)
    tq = qt_ref.shape[2]
    nk = k_ref.shape[1] // tk
    nband = gband_ref.shape[0]
    dv = o_ref.shape[0]
    _osm_init(m_ref, acc_ref)
    k_of = lambda c, off: k_ref[c, pl.ds(off, tk), :]

    j_lo = (tq // tk) * i - 1
    n_left = jnp.clip(j_lo, 0, nk)
    j_hi = jnp.clip(j_lo + nband, 0, nk)
    n_far = n_left + nk - j_hi
    cf = cfar_ref[...]

    def far_tile(t):
        j = jnp.where(t < n_left, t, t - n_left + j_hi)
        return jnp.clip(j, 0, nk - 1), jnp.where(t < n_left, cf[:, 0:1], cf[:, 1:2])

    def far_scores(t, buf):
        _attn_scores(tk, far_tile(t)[0], qt_ref, k_of, buf)

    def far_values(t, buf):
        j, sh = far_tile(t)
        _attn_values(tk, j, 2, vt_ref, buf, m_ref, acc_ref, shift=sh)

    _pipelined(n_far, far_scores, far_values, ((sa_ref, mxa_ref), (sb_ref, mxb_ref)), inner)

    for d in range(nband):
        j = j_lo + d

        @pl.when(jnp.logical_and(j >= 0, j < nk))
        def _():
            g = gband_ref[d]
            t = pltpu.roll(jnp.broadcast_to(g, (tk, g.shape[1])), tq + 1, 1, stride=1, stride_axis=0)
            _attn_scores(tk, j, qt_ref, k_of, (sa_ref, mxa_ref), bias_tile=t[:, :tq])
            _attn_values(tk, j, 2, vt_ref, (sa_ref, mxa_ref), m_ref, acc_ref)

    o = _osm_result(acc_ref, 0, dv) - lam_ref[...] * _osm_result(acc_ref, 1, dv)
    o = o * lax.rsqrt(jnp.mean(o * o, axis=0, keepdims=True) + EPS) * gd_ref[...]
    o_ref[...] = (o * out_scale).astype(o_ref.dtype)


def _attn_c(qt, k, vt, gband, cfar, lam, gd, out_scale, tq, tk, inner):
    b, h, _, dqk, s = qt.shape
    dva = vt.shape[2]
    dv = gd.shape[0]
    nband = gband.shape[1]
    return pl.pallas_call(
        functools.partial(_attn_c_kernel, tk, inner, out_scale),
        grid=(b, h, s // tq),
        in_specs=[pl.BlockSpec((None, None, 2, dqk, tq), lambda bi, hi, i: (bi, hi, 0, 0, i)),
                  pl.BlockSpec((None, None, 2, s, dqk), lambda bi, hi, i: (bi, hi, 0, 0, 0)),
                  pl.BlockSpec((None, None, dva, s), lambda bi, hi, i: (bi, hi, 0, 0)),
                  pl.BlockSpec((None, nband, 1, tq + tk), lambda bi, hi, i: (hi, 0, 0, 0)),
                  pl.BlockSpec((None, 1, 2), lambda bi, hi, i: (hi, 0, 0)),
                  _full(lam.shape), _full(gd.shape)],
        out_specs=pl.BlockSpec((None, None, dv, tq), lambda bi, hi, i: (bi, hi, 0, i)),
        out_shape=jax.ShapeDtypeStruct((b, h, dv, s), BF16),
        scratch_shapes=[pltpu.VMEM((2, 1, tq), F32), pltpu.VMEM((2, dva, tq), F32),
                        pltpu.VMEM((2, tk, tq), F32), pltpu.VMEM((2, tk, tq), F32),
                        pltpu.VMEM((2, 1, tq), F32), pltpu.VMEM((2, 1, tq), F32)],
        compiler_params=_cparams(("parallel", "parallel", "arbitrary")),
        name="attn_c",
    )(qt, k, vt, gband, cfar, lam, gd)


def _ssd_prep_kernel(nst, xc_ref, xp_ref, xn_ref, dt_ref, cw_ref, cb_ref, dtb_ref, xa_ref, dl_ref, ext_ref):
    i = pl.program_id(0)
    tm = xc_ref.shape[0]
    halo = V7X_SUBLANES
    pos = i % nst
    keep_prev = (pos != 0).astype(F32)
    keep_next = (pos != nst - 1).astype(F32)
    ext_ref[0:halo, :] = xp_ref[...] * keep_prev
    ext_ref[halo:halo + tm, :] = xc_ref[...]
    ext_ref[halo + tm:, :] = xn_ref[...] * keep_next
    pad_l = (SSM_CONV - 1) // 2
    cw = cw_ref[...]
    acc = jnp.zeros(xc_ref.shape, F32) + cb_ref[...]
    for t in range(SSM_CONV):
        acc = acc + ext_ref[halo - pad_l + t:halo - pad_l + t + tm, :] * cw[t:t + 1, :]
    xa_ref[...] = jax.nn.silu(acc)
    dl_ref[...] = jax.nn.softplus(dt_ref[...] + dtb_ref[...])


def _ssd_prep(xbc, dt, cw, cb, dtb, s, tm):
    t, c = xbc.shape
    nst = s // tm
    hb = tm // V7X_SUBLANES
    nhb = t // V7X_SUBLANES
    return pl.pallas_call(
        functools.partial(_ssd_prep_kernel, nst),
        grid=(t // tm,),
        in_specs=[pl.BlockSpec((tm, c), lambda i: (i, 0)),
                  pl.BlockSpec((V7X_SUBLANES, c), lambda i: (jnp.maximum(i * hb - 1, 0), 0)),
                  pl.BlockSpec((V7X_SUBLANES, c), lambda i: (jnp.minimum((i + 1) * hb, nhb - 1), 0)),
                  pl.BlockSpec((tm, dt.shape[1]), lambda i: (i, 0)),
                  _full(cw.shape), _full(cb.shape), _full(dtb.shape)],
        out_specs=[pl.BlockSpec((tm, c), lambda i: (i, 0)),
                   pl.BlockSpec((tm, dt.shape[1]), lambda i: (i, 0))],
        out_shape=[jax.ShapeDtypeStruct((t, c), F32), jax.ShapeDtypeStruct(dt.shape, F32)],
        scratch_shapes=[pltpu.VMEM((tm + 2 * V7X_SUBLANES, c), F32)],
        compiler_params=_cparams(("parallel",)),
        name="ssd_prep",
    )(xbc, xbc, xbc, dt, cw, cb, dtb)


def _ssd_chunk(reverse, xa, dl, alog, st_ref):
    ln = SSM_CHUNK
    row = lax.broadcasted_iota(jnp.int32, (ln, ln), 0)
    col = lax.broadcasted_iota(jnp.int32, (ln, ln), 1)
    tri = (col >= row) if reverse else (col <= row)
    eye = row == col
    last = 0 if reverse else ln - 1
    lane0 = SSM_HEADS if reverse else 0
    a_neg = -jnp.exp(alog)
    acs = jnp.dot(tri.astype(F32), dl * a_neg, precision=HIGHEST, preferred_element_type=F32)
    dx = SSM_HEADS * SSM_HEAD_DIM
    gw = SSM_STATE
    hpg = SSM_HEADS // SSM_GROUPS
    ys = []
    for g in range(SSM_GROUPS):
        bg = xa[:, dx + g * gw:dx + (g + 1) * gw].astype(BF16)
        cg = xa[:, dx + SSM_GROUPS * gw + g * gw:dx + SSM_GROUPS * gw + (g + 1) * gw].astype(BF16)
        cb = _dot_nt(cg, bg)
        for r in range(hpg):
            h = g * hpg + r
            ln_h = lane0 + h
            ac = acs[:, ln_h:ln_h + 1]
            ar = jnp.sum(jnp.where(eye, ac, 0.0), axis=0, keepdims=True)
            dec = jnp.exp(jnp.where(tri, ac - ar, NEG_BIG))
            xdt = xa[:, h * SSM_HEAD_DIM:(h + 1) * SSM_HEAD_DIM] * dl[:, ln_h:ln_h + 1]
            st = st_ref[h]
            y = _dot((cb * dec).astype(BF16), xdt.astype(BF16))
            y = y + _dot(cg, st.astype(BF16)) * jnp.exp(ac)
            a_last = acs[last:last + 1, ln_h:ln_h + 1]
            st_ref[h] = st * jnp.exp(a_last) + _dot_tn(bg, (xdt * jnp.exp(a_last - ac)).astype(BF16))
            ys.append(y)
    return jnp.concatenate(ys, axis=1)


def _ssd_fwd_kernel(cps, xa_ref, dl_ref, alog_ref, y_ref, st_ref):
    @pl.when(pl.program_id(1) == 0)
    def _():
        st_ref[...] = jnp.zeros(st_ref.shape, F32)

    for c in range(cps):
        sl = slice(c * SSM_CHUNK, (c + 1) * SSM_CHUNK)
        y_ref[sl, :] = _ssd_chunk(False, xa_ref[sl, :], dl_ref[sl, :], alog_ref[...], st_ref)


def _ssd_bwd_kernel(cps, xa_ref, dl_ref, alog_ref, yf_ref, z_ref, dsk_ref, g_ref, o_ref, st_ref):
    @pl.when(pl.program_id(1) == 0)
    def _():
        st_ref[...] = jnp.zeros(st_ref.shape, F32)

    dx = SSM_HEADS * SSM_HEAD_DIM
    for c in reversed(range(cps)):
        sl = slice(c * SSM_CHUNK, (c + 1) * SSM_CHUNK)
        xa = xa_ref[sl, :]
        yb = _ssd_chunk(True, xa, dl_ref[sl, :], alog_ref[...], st_ref)
        y = yf_ref[sl, :] + yb + dsk_ref[...] * xa[:, :dx]
        y = y * jax.nn.silu(z_ref[sl, :])
        o_ref[sl, :] = _rms(y, g_ref[...]).astype(o_ref.dtype)


def _ssd(xa, dl, alog, z, dsk, gssm, b, s, cps):
    t, c = xa.shape
    tm = cps * SSM_CHUNK
    nblk = s // tm
    dx = SSM_HEADS * SSM_HEAD_DIM
    st = pltpu.VMEM((SSM_HEADS, SSM_STATE, SSM_HEAD_DIM), F32)
    fw = lambda w: pl.BlockSpec((tm, w), lambda bi, ci: (bi * nblk + ci, 0))
    bw = lambda w: pl.BlockSpec((tm, w), lambda bi, ci: (bi * nblk + nblk - 1 - ci, 0))
    yf = pl.pallas_call(
        functools.partial(_ssd_fwd_kernel, cps),
        grid=(b, nblk),
        in_specs=[fw(c), fw(dl.shape[1]), _full(alog.shape)],
        out_specs=fw(dx),
        out_shape=jax.ShapeDtypeStruct((t, dx), F32),
        scratch_shapes=[st],
        compiler_params=_cparams(("parallel", "arbitrary")),
        name="ssd_fwd",
    )(xa, dl, alog)
    return pl.pallas_call(
        functools.partial(_ssd_bwd_kernel, cps),
        grid=(b, nblk),
        in_specs=[bw(c), bw(dl.shape[1]), _full(alog.shape), bw(dx), bw(dx), _full(dsk.shape),
                  _full(gssm.shape)],
        out_specs=bw(dx),
        out_shape=jax.ShapeDtypeStruct((t, dx), BF16),
        scratch_shapes=[st],
        compiler_params=_cparams(("parallel", "arbitrary")),
        name="ssd_bwd",
    )(xa, dl, alog, yf, z, dsk, gssm)


def _sgu_kernel(nck, u_ref, v_ref, w_ref, bias_ref, o_ref):
    for c in range(nck):
        sl = slice(c * SGU_CHUNK, (c + 1) * SGU_CHUNK)
        v = v_ref[sl, :]
        sv = [_dot(w_ref[g], v[:, g * SGU_GROUP_DIM:(g + 1) * SGU_GROUP_DIM]) for g in range(SGU_GROUPS)]
        o_ref[sl, :] = (u_ref[sl, :] * (jnp.concatenate(sv, axis=1) + bias_ref[...])).astype(o_ref.dtype)


def _sgu(u, v, w, bias, nck):
    t, c = u.shape
    tm = nck * SGU_CHUNK
    return pl.pallas_call(
        functools.partial(_sgu_kernel, nck),
        grid=(t // tm,),
        in_specs=[pl.BlockSpec((tm, c), lambda i: (i, 0)), pl.BlockSpec((tm, c), lambda i: (i, 0)),
                  _full(w.shape), _full(bias.shape)],
        out_specs=pl.BlockSpec((tm, c), lambda i: (i, 0)),
        out_shape=jax.ShapeDtypeStruct((t, c), BF16),
        compiler_params=_cparams(("parallel",)),
        name="sgu",
    )(u, v, w, bias)


def _merge_kernel(x_ref, gmix_ref, oa_ref, ob_ref, oc_ref, od_ref, wg_ref, wbr_ref, wo_ref, o_ref):
    x = x_ref[...]
    dm = x.shape[1]
    hb = _rms(x, gmix_ref[...]).astype(BF16)
    merged = jnp.zeros(x.shape, F32)
    for n, br in enumerate((oa_ref, ob_ref, oc_ref, od_ref)):
        gate = jax.nn.sigmoid(_dot(hb, wg_ref[:, n * dm:(n + 1) * dm]))
        merged = merged + gate * _dot(br[...], wbr_ref[n])
    o_ref[...] = x + _dot(merged.astype(BF16), wo_ref[...])


def _merge(x, gmix, oa, ob, oc, od, wg, wbr, wo, tm):
    t, dm = x.shape
    row = lambda w: pl.BlockSpec((tm, w), lambda i: (i, 0))
    return pl.pallas_call(
        _merge_kernel,
        grid=(t // tm,),
        in_specs=[row(dm), _full(gmix.shape), row(256), row(256), row(256), row(256),
                  _full(wg.shape), _full(wbr.shape), _full(wo.shape)],
        out_specs=row(dm),
        out_shape=jax.ShapeDtypeStruct((t, dm), F32),
        compiler_params=_cparams(("parallel",)),
        name="merge",
    )(x, gmix, oa, ob, oc, od, wg, wbr, wo)


def _router_kernel(x_ref, g_ref, wrt_ref, h_ref, aff_ref):
    h = _rms(x_ref[...], g_ref[...])
    h_ref[...] = h.astype(BF16)
    logits = lax.dot_general(wrt_ref[...], h, (((1,), (1,)), ((), ())), precision=HIGHEST,
                             preferred_element_type=F32)
    e = jnp.exp(logits - jnp.max(logits, axis=0, keepdims=True))
    aff_ref[...] = e / jnp.sum(e, axis=0, keepdims=True)


def _router(x, g, wrt, b, s, tm):
    t, dm = x.shape
    ne = wrt.shape[0]
    nst = s // tm
    return pl.pallas_call(
        _router_kernel,
        grid=(t // tm,),
        in_specs=[pl.BlockSpec((tm, dm), lambda i: (i, 0)), _full(g.shape), _full(wrt.shape)],
        out_specs=[pl.BlockSpec((tm, dm), lambda i: (i, 0)),
                   pl.BlockSpec((None, ne, tm), lambda i: (i // nst, 0, i % nst))],
        out_shape=[jax.ShapeDtypeStruct((t, dm), BF16), jax.ShapeDtypeStruct((b, ne, s), F32)],
        compiler_params=_cparams(("parallel",)),
        name="router",
    )(x, g, wrt)


def _select_kernel(cap, aff_ref, gate_ref, pos_ref):
    ne, s = aff_ref.shape
    bits = pltpu.bitcast(aff_ref[...], jnp.int32)
    capf = jnp.float32(cap)

    def bisect(i, thr):
        cand = thr | lax.shift_left(jnp.int32(1), 30 - i)
        cnt = jnp.sum((bits >= cand).astype(F32), axis=1, keepdims=True)
        return jnp.where(cnt >= capf, cand, thr)

    thr = lax.fori_loop(0, 31, bisect, jnp.zeros((ne, 1), jnp.int32))
    need = capf - jnp.sum((bits > thr).astype(F32), axis=1, keepdims=True)
    lanes = V7X_LANES
    r = lax.broadcasted_iota(jnp.int32, (lanes, lanes), 0)
    c = lax.broadcasted_iota(jnp.int32, (lanes, lanes), 1)
    before = (r < c).astype(BF16)

    def block(jb, carry):
        n_eq, n_keep = carry
        off = pl.multiple_of(jb * lanes, lanes)
        a = aff_ref[:, pl.ds(off, lanes)]
        bb = pltpu.bitcast(a, jnp.int32)
        eq = bb == thr
        eqf = eq.astype(F32)
        keep = jnp.logical_or(bb > thr, jnp.logical_and(eq, n_eq + _dot(eqf.astype(BF16), before) < need))
        keepf = keep.astype(F32)
        gate_ref[:, pl.ds(off, lanes)] = jnp.where(keep, a, 0.0)
        pos_ref[:, pl.ds(off, lanes)] = jnp.where(keep, n_keep + _dot(keepf.astype(BF16), before), -1.0)
        return n_eq + jnp.sum(eqf, axis=1, keepdims=True), n_keep + jnp.sum(keepf, axis=1, keepdims=True)

    zero = jnp.zeros((ne, 1), F32)
    lax.fori_loop(0, s // lanes, block, (zero, zero))


def _select(aff, cap):
    b, ne, s = aff.shape
    spec = pl.BlockSpec((None, ne, s), lambda i: (i, 0, 0))
    return pl.pallas_call(
        functools.partial(_select_kernel, cap),
        grid=(b,),
        in_specs=[spec],
        out_specs=[spec, spec],
        out_shape=[jax.ShapeDtypeStruct(aff.shape, F32), jax.ShapeDtypeStruct(aff.shape, F32)],
        compiler_params=_cparams(("parallel",)),
        name="moe_select",
    )(aff)


def _moe_kernel(rb, cnt_ref, base_ref, h_ref, posr_ref, posc_ref, gates_ref, wg_ref, wu_ref, wd_ref, o_ref):
    i = pl.program_id(0)
    e = pl.program_id(1)

    @pl.when(e == 0)
    def _():
        o_ref[...] = jnp.zeros(o_ref.shape, F32)

    tt = h_ref.shape[0]
    base = base_ref[i, e].astype(F32)
    lane = lax.broadcasted_iota(jnp.int32, gates_ref.shape, 1)
    col = lambda ref: jnp.sum(jnp.where(lane == e, ref[...], 0.0), axis=1, keepdims=True)
    gcol = col(gates_ref)
    posc = col(posc_ref) - base
    posr = posr_ref[...] - base
    h = h_ref[...]

    def block(bi, carry):
        lo = (bi * rb).astype(F32)
        slot_r = lax.broadcasted_iota(jnp.int32, (rb, 1), 0).astype(F32) + lo
        slot_c = lax.broadcasted_iota(jnp.int32, (1, rb), 1).astype(F32) + lo
        xc = _dot((posr == slot_r).astype(BF16), h).astype(BF16)
        hid = jax.nn.silu(_dot(xc, wg_ref[...])) * _dot(xc, wu_ref[...])
        y = _dot(hid.astype(BF16), wd_ref[...])
        o_ref[...] += _dot((posc == slot_c).astype(BF16), y.astype(BF16)) * gcol
        return carry

    lax.fori_loop(0, (cnt_ref[i, e] + rb - 1) // rb, block, 0)


def _moe(h, pos_lane, pos_tm, gates_tm, cnt, base, wg, wu, wd, tt, rb):
    t, dm = h.shape
    ne, _, dff = wg.shape
    s = pos_lane.shape[2]
    nst = s // tt
    return pl.pallas_call(
        functools.partial(_moe_kernel, rb),
        grid_spec=pltpu.PrefetchScalarGridSpec(
            num_scalar_prefetch=2,
            grid=(t // tt, ne),
            in_specs=[pl.BlockSpec((tt, dm), lambda i, e, *_: (i, 0)),
                      pl.BlockSpec((None, 1, tt), lambda i, e, *_: ((i // nst) * ne + e, 0, i % nst)),
                      pl.BlockSpec((tt, ne), lambda i, e, *_: (i, 0)),
                      pl.BlockSpec((tt, ne), lambda i, e, *_: (i, 0)),
                      pl.BlockSpec((None, dm, dff), lambda i, e, *_: (e, 0, 0)),
                      pl.BlockSpec((None, dm, dff), lambda i, e, *_: (e, 0, 0)),
                      pl.BlockSpec((None, dff, dm), lambda i, e, *_: (e, 0, 0))],
            out_specs=pl.BlockSpec((tt, dm), lambda i, e, *_: (i, 0))),
        out_shape=jax.ShapeDtypeStruct((t, dm), F32),
        compiler_params=_cparams(("parallel", "arbitrary")),
        name="moe_ffn",
    )(cnt, base, h, pos_lane, pos_tm, gates_tm, wg, wu, wd)


def _ple_kernel(final, x_ref, moe_ref, p_ref, g_ref, wpg_ref, wp_ref, gf_ref, o_ref):
    x = x_ref[...] + moe_ref[...]
    gate = jax.nn.sigmoid(_dot(_rms(x, g_ref[...]).astype(BF16), wpg_ref[...]))
    y = x + gate * _dot(p_ref[...].astype(BF16), wp_ref[...])
    o_ref[...] = _rms(y, gf_ref[...]) if final else y


def _ple(x, moe, p, g, wpg, wp, gf, final, tm):
    t, dm = x.shape
    return pl.pallas_call(
        functools.partial(_ple_kernel, final),
        grid=(t // tm,),
        in_specs=[pl.BlockSpec((tm, dm), lambda i: (i, 0)), pl.BlockSpec((tm, dm), lambda i: (i, 0)),
                  pl.BlockSpec((tm, p.shape[1]), lambda i: (i, 0)),
                  _full(g.shape), _full(wpg.shape), _full(wp.shape), _full(gf.shape)],
        out_specs=pl.BlockSpec((tm, dm), lambda i: (i, 0)),
        out_shape=jax.ShapeDtypeStruct((t, dm), F32),
        compiler_params=_cparams(("parallel",)),
        name="ple",
    )(x, moe, p, g, wpg, wp, gf)


def _rope_tables(s):
    quarter = A_HEAD_DIM // 4
    half = A_HEAD_DIM // 2
    pos = jnp.arange(s, dtype=jnp.int32)
    freqs = ROPE_THETA ** (-jnp.arange(0, half, 2, dtype=F32) / half)
    ang_r = (pos // GRID_W).astype(F32)[:, None] * freqs[None, :]
    ang_c = (pos % GRID_W).astype(F32)[:, None] * freqs[None, :]
    zero = jnp.zeros((s, quarter), F32)
    cos = jnp.concatenate([jnp.cos(ang_r)] * 2 + [jnp.cos(ang_c)] * 2, axis=1)
    s1 = jnp.concatenate([-jnp.sin(ang_r), zero, -jnp.sin(ang_c), zero], axis=1)
    s2 = jnp.concatenate([zero, jnp.sin(ang_r), zero, jnp.sin(ang_c)], axis=1)
    two = lambda a: jnp.concatenate([a, a], axis=1)
    return two(cos), two(s1), two(s2)


def _t5_bucket(rel):
    nb = REL_BUCKETS // 2
    max_exact = nb // 2
    ret = jnp.where(rel > 0, nb, 0)
    r = jnp.abs(rel)
    rf = jnp.maximum(r, 1).astype(F32)
    large = max_exact + (jnp.log(rf / max_exact) / math.log(REL_MAX_DIST / max_exact)
                         * (nb - max_exact)).astype(jnp.int32)
    large = jnp.minimum(large, nb - 1)
    return ret + jnp.where(r < max_exact, r, large)


def _bias_tables(rel_bias, tq, tk):
    assert tk >= REL_MAX_DIST and tq % tk == 0
    rep = tq // tk
    offs = jnp.arange(-1, rep + 1, dtype=jnp.int32) * tk
    u = jnp.arange(tq + tk, dtype=jnp.int32)
    rel = offs[:, None] + (tk - 1) - u[None, :]
    gband = jnp.transpose(rel_bias[_t5_bucket(rel)], (2, 0, 1))[:, :, None, :].astype(F32) * LOG2E
    far = jnp.array([-REL_MAX_DIST, REL_MAX_DIST], dtype=jnp.int32)
    cfar = jnp.transpose(rel_bias[_t5_bucket(far)], (1, 0))[:, None, :].astype(F32) * LOG2E
    return gband, cfar


def _tile(n, pref):
    t = min(n, pref)
    assert n % t == 0
    return t


def _split(a, b, s, *dims):
    return a.reshape(b, s, *dims)


def _with_ones(vt):
    ones = jnp.ones(vt.shape[:-2] + (BF16_SUBLANE_PACK, vt.shape[-1]), vt.dtype)
    return jnp.concatenate([vt, ones], axis=-2)


def _tokens(ot):
    b, h, d, s = ot.shape
    return jnp.transpose(ot, (0, 3, 1, 2)).reshape(b * s, h * d)


def kernel(x, p, rel_bias, g_mix, w_in, g_qnorm, g_knorm, conv_w, conv_b, dt_bias_f, dt_bias_b, a_log_f, a_log_b, d_skip, g_ssm, lambda_q1, lambda_k1, lambda_q2, lambda_k2, g_diff, g_sgu, w_spatial, b_spatial, w_branch, w_branch_gate, w_out, g_moe, w_router, w_exp_gate, w_exp_up, w_exp_down, g_ple, w_ple_gate, w_ple, g_final):
    b, s, dm = x.shape
    depth = w_in.shape[0]
    t = b * s
    cap = EC_CAPACITY * s // N_EXPERTS

    tm_proj = _tile(s, 512)
    tm_prep = _tile(s, 1024)
    tq_a, tk_a = _tile(s, 512), _tile(s, 512)
    tq_c, tk_c = _tile(s, 512), _tile(s, 512)
    cps = _tile(s // SSM_CHUNK, 4)
    nck = _tile(s // SGU_CHUNK, 8)
    tt_moe = _tile(s, 1024)

    rc, rs1, rs2 = _rope_tables(s)
    gband, cfar = _bias_tables(rel_bias, tq_c, tk_c)
    hd = A_HEAD_DIM
    bd = (jnp.arange(A_Q_HEADS * hd)[:, None] // hd == jnp.arange(A_Q_HEADS * hd)[None, :] // hd).astype(F32) / hd

    sizes = (A_Q_HEADS * hd, A_KV_HEADS * hd, A_KV_HEADS * hd,
             SSM_HEADS * SSM_HEAD_DIM, SSM_HEADS * SSM_HEAD_DIM, SSM_GROUPS * SSM_STATE,
             SSM_GROUPS * SSM_STATE, SSM_HEADS, SSM_HEADS,
             DIFF_HEADS * 2 * DIFF_QK_DIM, DIFF_HEADS * 2 * DIFF_QK_DIM, DIFF_HEADS * DIFF_V_DIM,
             2 * SGU_GROUPS * SGU_GROUP_DIM)
    off = [0]
    for sz in sizes:
        off.append(off[-1] + sz)
    o_a0, o_b0, o_dt0, o_c0, o_d0, o_end = off[0], off[3], off[7], off[9], off[12], off[13]
    dt_pad = V7X_LANES - 2 * SSM_HEADS

    def lane_row(v, width=None):
        v = v.astype(F32).reshape(1, -1)
        if width is not None and v.shape[1] < width:
            v = jnp.pad(v, ((0, 0), (0, width - v.shape[1])))
        return v

    xt = x.reshape(t, dm)
    for i in range(depth):
        wi = w_in[i]
        wa = wi[:, o_a0:o_b0].astype(BF16)
        wb = jnp.pad(wi[:, o_b0:o_c0], ((0, 0), (0, dt_pad))).astype(BF16)
        wc = wi[:, o_c0:o_d0].astype(BF16)
        wd = wi[:, o_d0:o_end].astype(BF16)
        gmix = lane_row(g_mix[i])
        (qa, ka, va, z, xbc, dt, cq, ck, cv, du, dv) = _inproj(
            xt, gmix, wa, wb, wc, wd, rc, rs1, rs2,
            lane_row(jnp.tile(g_qnorm[i], A_Q_HEADS)), lane_row(jnp.tile(g_knorm[i], A_KV_HEADS)),
            bd, lane_row(g_sgu[i]), s, tm_proj)

        qt = jnp.transpose(_split(qa, b, s, A_Q_HEADS, hd), (0, 2, 3, 1))
        kh = jnp.transpose(_split(ka, b, s, A_KV_HEADS, hd), (0, 2, 1, 3))
        vt = jnp.transpose(_split(va, b, s, A_KV_HEADS, hd), (0, 2, 3, 1))
        o_a = _tokens(_attn_a(qt, kh, _with_ones(vt), tq_a, tk_a, ATTN_INNER))

        xa, dl = _ssd_prep(xbc, dt, conv_w[i].astype(F32), lane_row(conv_b[i]),
                           lane_row(jnp.concatenate([dt_bias_f[i], dt_bias_b[i]]), V7X_LANES), s, tm_prep)
        o_b = _ssd(xa, dl, lane_row(jnp.concatenate([a_log_f[i], a_log_b[i]]), V7X_LANES), z,
                   lane_row(jnp.repeat(d_skip[i], SSM_HEAD_DIM)), lane_row(g_ssm[i]), b, s, cps)

        lam_init = 0.8 - 0.6 * math.exp(-0.3 * i)
        lam = (jnp.exp(jnp.sum(lambda_q1[i].astype(F32) * lambda_k1[i].astype(F32)))
               - jnp.exp(jnp.sum(lambda_q2[i].astype(F32) * lambda_k2[i].astype(F32))) + lam_init).reshape(1, 1)
        cqt = jnp.transpose(_split(cq, b, s, DIFF_HEADS, 2, DIFF_QK_DIM), (0, 2, 3, 4, 1))
        ckh = jnp.transpose(_split(ck, b, s, DIFF_HEADS, 2, DIFF_QK_DIM), (0, 2, 3, 1, 4))
        cvt = jnp.transpose(_split(cv, b, s, DIFF_HEADS, DIFF_V_DIM), (0, 2, 3, 1))
        o_c = _tokens(_attn_c(cqt, ckh, _with_ones(cvt), gband, cfar, lam, g_diff[i].astype(F32).reshape(-1, 1),
                              1.0 - lam_init, tq_c, tk_c, ATTN_INNER))

        sgu_bias = jnp.repeat(b_spatial[i].T.astype(F32), SGU_GROUP_DIM, axis=1)
        o_d = _sgu(du, dv, w_spatial[i].astype(BF16), sgu_bias, nck)

        xt = _merge(xt, gmix, o_a, o_b, o_c, o_d, w_branch_gate[i].astype(BF16),
                    w_branch[i].astype(BF16), w_out[i].astype(BF16), tm_proj)

        h2, aff = _router(xt, lane_row(g_moe[i]), w_router[i].T.astype(F32), b, s, tm_proj)
        gates, pos = _select(aff, cap)
        token_major = lambda a: jnp.transpose(a, (0, 2, 1)).reshape(t, N_EXPERTS)
        cnt = jnp.sum((pos >= 0).reshape(b, N_EXPERTS, s // tt_moe, tt_moe), axis=3, dtype=jnp.int32)
        tile_major = lambda a: jnp.transpose(a, (0, 2, 1)).reshape(t // tt_moe, N_EXPERTS)
        moe = _moe(h2, pos.reshape(b * N_EXPERTS, 1, s), token_major(pos), token_major(gates),
                   tile_major(cnt), tile_major(jnp.cumsum(cnt, axis=2) - cnt),
                   w_exp_gate[i].astype(BF16), w_exp_up[i].astype(BF16), w_exp_down[i].astype(BF16),
                   tt_moe, MOE_ROW_BLOCK)

        xt = _ple(xt, moe, p[i].reshape(t, -1), lane_row(g_ple[i]), w_ple_gate[i].astype(BF16),
                  w_ple[i].astype(BF16), lane_row(g_final), i == depth - 1, tm_proj)

    return xt.reshape(b, s, dm)
```

```python
import functools
import math

import jax
import jax.numpy as jnp
from jax import lax
from jax.experimental import pallas as pl
from jax.experimental.pallas import tpu as pltpu

F32 = jnp.float32
BF16 = jnp.bfloat16
HIGHEST = lax.Precision.HIGHEST

EPS = 1e-6
GRID_W = 64
A_Q_HEADS, A_KV_HEADS, A_HEAD_DIM = 4, 2, 64
ROPE_THETA = 10000.0
SSM_HEADS, SSM_HEAD_DIM, SSM_GROUPS, SSM_STATE, SSM_CONV, SSM_CHUNK = 4, 64, 2, 128, 5, 128
DIFF_HEADS, DIFF_QK_DIM, DIFF_V_DIM = 4, 32, 64
REL_BUCKETS, REL_MAX_DIST = 32, 128
SGU_GROUPS, SGU_GROUP_DIM, SGU_CHUNK = 4, 64, 128
N_EXPERTS, EC_CAPACITY = 16, 2
N_BRANCHES, BRANCH_WIDTH = 4, 256

V7X_LANES = 128
V7X_SUBLANES = 8
BF16_SUBLANE_PACK = 16
V7X_VMEM_LIMIT_BYTES = 56 * 1024 * 1024

NEG_BIG = -1e30
LOG2E = math.log2(math.e)
MOE_ROW_BLOCK = 128
ATTN_INNER = 4


def _cparams(sem):
    return pltpu.CompilerParams(dimension_semantics=sem, vmem_limit_bytes=V7X_VMEM_LIMIT_BYTES)


def _rms(x, g):
    return x * lax.rsqrt(jnp.mean(x * x, axis=-1, keepdims=True) + EPS) * g


def _dot(a, b):
    return jnp.dot(a, b, preferred_element_type=F32)


def _dot_nt(a, b):
    return lax.dot_general(a, b, (((1,), (1,)), ((), ())), preferred_element_type=F32)


def _dot_tn(a, b):
    return lax.dot_general(a, b, (((0,), (0,)), ((), ())), preferred_element_type=F32)


def _full(shape):
    n = len(shape)
    return pl.BlockSpec(shape, lambda *_: (0,) * n)


def _norm_rope_t(x, g_col, tab):
    xn = x * lax.rsqrt(jnp.mean(x * x, axis=0, keepdims=True) + EPS) * g_col
    qd = A_HEAD_DIM // 4
    cr, sr, cc, sc = (tab[i * qd:(i + 1) * qd] for i in range(4))
    r1, r2, c1, c2 = (xn[i * qd:(i + 1) * qd] for i in range(4))
    return jnp.concatenate([r1 * cr - r2 * sr, r1 * sr + r2 * cr, c1 * cc - c2 * sc, c1 * sc + c2 * cc], axis=0)


def _inproj_kernel(x_ref, gmix_ref, wat_ref, wb_ref, wct_ref, wd_ref, tab_ref, gq_ref, gk_ref, gsgu_ref,
                   qa_ref, ka_ref, va_ref, z_ref, xbc_ref, dt_ref, cq_ref, ck_ref, cv_ref,
                   du_ref, dv_ref):
    hb = _rms(x_ref[...], gmix_ref[...]).astype(BF16)
    tm = hb.shape[0]
    hd = A_HEAD_DIM
    ones = jnp.ones((BF16_SUBLANE_PACK, tm), BF16)

    at = _dot_nt(wat_ref[...], hb)
    tab = tab_ref[...]
    q = [_norm_rope_t(at[h * hd:(h + 1) * hd], gq_ref[...], tab) for h in range(A_Q_HEADS)]
    qa_ref[...] = (jnp.concatenate(q, axis=0) * (hd ** -0.5 * LOG2E)).astype(BF16)
    k0 = A_Q_HEADS * hd
    v0 = k0 + A_KV_HEADS * hd
    for g in range(A_KV_HEADS):
        kt = _norm_rope_t(at[k0 + g * hd:k0 + (g + 1) * hd], gk_ref[...], tab)
        ka_ref[g] = jnp.transpose(kt).astype(BF16)
        va_ref[g, :hd] = at[v0 + g * hd:v0 + (g + 1) * hd].astype(BF16)
        va_ref[g, hd:] = ones

    b = _dot(hb, wb_ref[...])
    dx = SSM_HEADS * SSM_HEAD_DIM
    conv_ch = dx + 2 * SSM_GROUPS * SSM_STATE
    z_ref[...] = b[:, :dx]
    xbc_ref[...] = b[:, dx:dx + conv_ch]
    dt_ref[...] = b[:, dx + conv_ch:]

    ct = _dot_nt(wct_ref[...], hb)
    nqc = DIFF_HEADS * 2 * DIFF_QK_DIM
    cq_ref[...] = (ct[:nqc] * (DIFF_QK_DIM ** -0.5 * LOG2E)).astype(BF16)
    ck = jnp.transpose(ct[nqc:2 * nqc]).astype(BF16)
    for h in range(DIFF_HEADS):
        for c in range(2):
            lo = (2 * h + c) * DIFF_QK_DIM
            ck_ref[h, c] = ck[:, lo:lo + DIFF_QK_DIM]
        cv_ref[h, :DIFF_V_DIM] = ct[2 * nqc + h * DIFF_V_DIM:2 * nqc + (h + 1) * DIFF_V_DIM].astype(BF16)
        cv_ref[h, DIFF_V_DIM:] = ones

    d = jax.nn.gelu(_dot(hb, wd_ref[...]))
    w = SGU_GROUPS * SGU_GROUP_DIM
    du_ref[...] = d[:, :w]
    dv_ref[...] = _rms(d[:, w:], gsgu_ref[...]).astype(BF16)


def _inproj(x, gmix, wat, wb, wct, wd, tab, gq, gk, gsgu, b, s, tm):
    t, dm = x.shape
    nst = s // tm
    row = lambda w: pl.BlockSpec((tm, w), lambda i: (i, 0))
    lanes = lambda *lead: pl.BlockSpec((None,) + lead + (tm,), lambda i: (i // nst,) + (0,) * len(lead) + (i % nst,))
    hd, pk = A_HEAD_DIM, BF16_SUBLANE_PACK
    specs_shapes = [
        (lanes(A_Q_HEADS * hd), (b, A_Q_HEADS * hd, s), BF16),
        (pl.BlockSpec((None, A_KV_HEADS, tm, hd), lambda i: (i // nst, 0, i % nst, 0)), (b, A_KV_HEADS, s, hd), BF16),
        (lanes(A_KV_HEADS, hd + pk), (b, A_KV_HEADS, hd + pk, s), BF16),
        (row(256), (t, 256), F32), (row(768), (t, 768), F32), (row(128), (t, 128), F32),
        (lanes(DIFF_HEADS * 2 * DIFF_QK_DIM), (b, DIFF_HEADS * 2 * DIFF_QK_DIM, s), BF16),
        (pl.BlockSpec((None, DIFF_HEADS, 2, tm, DIFF_QK_DIM), lambda i: (i // nst, 0, 0, i % nst, 0)),
         (b, DIFF_HEADS, 2, s, DIFF_QK_DIM), BF16),
        (lanes(DIFF_HEADS, DIFF_V_DIM + pk), (b, DIFF_HEADS, DIFF_V_DIM + pk, s), BF16),
        (row(256), (t, 256), F32), (row(256), (t, 256), BF16)]
    return pl.pallas_call(
        _inproj_kernel,
        grid=(t // tm,),
        in_specs=[row(dm), _full(gmix.shape), _full(wat.shape), _full(wb.shape), _full(wct.shape),
                  _full(wd.shape), pl.BlockSpec((tab.shape[0], tm), lambda i: (0, i % nst)),
                  _full(gq.shape), _full(gk.shape), _full(gsgu.shape)],
        out_specs=[sp for sp, _, _ in specs_shapes],
        out_shape=[jax.ShapeDtypeStruct(shp, dt) for _, shp, dt in specs_shapes],
        compiler_params=_cparams(("parallel",)),
        name="inproj",
    )(x, gmix, wat, wb, wct, wd, tab, gq, gk, gsgu)


def _osm_tile(s, mt, shift, vt, m_ref, acc_ref, c):
    m_old = m_ref[c]
    if shift is not None:
        mt = mt + shift
    m_new = jnp.maximum(m_old, mt)
    p = jnp.exp2(s - (m_new if shift is None else m_new - shift))
    acc_ref[c] = jnp.exp2(m_old - m_new) * acc_ref[c] + _dot(vt, p.astype(BF16))
    m_ref[c] = m_new


def _osm_init(m_ref, acc_ref):
    m_ref[...] = jnp.full(m_ref.shape, NEG_BIG, F32)
    acc_ref[...] = jnp.zeros(acc_ref.shape, F32)


def _osm_result(acc_ref, c, dv):
    acc = acc_ref[c]
    return acc[:dv] / acc[dv:dv + 1]


def _attn_scores(tk, j, qt_ref, k_of, buf, bias_tile=None):
    s_ref, mx_ref = buf
    off = pl.multiple_of(j * tk, tk)
    for c in range(qt_ref.shape[0]):
        s = _dot(k_of(c, off), qt_ref[c])
        if bias_tile is not None:
            s = s + bias_tile
        s_ref[c] = s
        mx_ref[c] = jnp.max(s, axis=0, keepdims=True)


def _attn_values(tk, j, nmaps, vt_ref, buf, m_ref, acc_ref, shift=None):
    s_ref, mx_ref = buf
    vt = vt_ref[:, pl.ds(pl.multiple_of(j * tk, tk), tk)]
    for c in range(nmaps):
        _osm_tile(s_ref[c], mx_ref[c], shift, vt, m_ref, acc_ref, c)


def _pipelined(n, scores, values, s_refs, inner):
    last = jnp.maximum(n - 1, 0)
    span = 2 * inner
    scores(jnp.minimum(0, last), s_refs[0])

    def double(g):
        scores(jnp.minimum(g + 1, last), s_refs[1])
        values(g, s_refs[0])
        scores(jnp.minimum(g + 2, last), s_refs[0])
        values(g + 1, s_refs[1])

    def unrolled(t, carry):
        for w in range(inner):
            double(span * t + 2 * w)
        return carry

    lax.fori_loop(0, n // span, unrolled, 0)
    base = (n // span) * span

    def rolled(t, carry):
        double(base + 2 * t)
        return carry

    lax.fori_loop(0, (n - base) // 2, rolled, 0)

    @pl.when(n % 2 == 1)
    def _():
        values(n - 1, s_refs[0])


def _attn_a_kernel(tk, inner, qt_ref, k_ref, vt_ref, o_ref, m_ref, acc_ref, sa_ref, sb_ref, mxa_ref, mxb_ref):
    _osm_init(m_ref, acc_ref)
    nq = qt_ref.shape[0]
    k_of = lambda c, off: k_ref[pl.ds(off, tk), :]
    _pipelined(k_ref.shape[0] // tk,
               lambda j, buf: _attn_scores(tk, j, qt_ref, k_of, buf),
               lambda j, buf: _attn_values(tk, j, nq, vt_ref, buf, m_ref, acc_ref),
               ((sa_ref, mxa_ref), (sb_ref, mxb_ref)), inner)
    for c in range(nq):
        o_ref[c] = _osm_result(acc_ref, c, o_ref.shape[1]).astype(o_ref.dtype)


def _attn_a(qt, k, vt, tq, tk, inner):
    b, hq, d, s = qt.shape
    hkv = k.shape[1]
    rep = hq // hkv
    da = vt.shape[2]
    qspec = pl.BlockSpec((None, rep, d, tq), lambda bi, g, i: (bi, g, 0, i))
    return pl.pallas_call(
        functools.partial(_attn_a_kernel, tk, inner),
        grid=(b, hkv, s // tq),
        in_specs=[qspec,
                  pl.BlockSpec((None, None, s, d), lambda bi, g, i: (bi, g, 0, 0)),
                  pl.BlockSpec((None, None, da, s), lambda bi, g, i: (bi, g, 0, 0))],
        out_specs=qspec,
        out_shape=jax.ShapeDtypeStruct(qt.shape, BF16),
        scratch_shapes=[pltpu.VMEM((rep, 1, tq), F32), pltpu.VMEM((rep, da, tq), F32),
                        pltpu.VMEM((rep, tk, tq), F32), pltpu.VMEM((rep, tk, tq), F32),
                        pltpu.VMEM((rep, 1, tq), F32), pltpu.VMEM((rep, 1, tq), F32)],
        compiler_params=_cparams(("parallel", "parallel", "arbitrary")),
        name="attn_a",
    )(qt, k, vt)


def _attn_c_kernel(tk, inner, out_scale, qt_ref, k_ref, vt_ref, gband_ref, cfar_ref, lam_ref, gd_ref, o_ref,
                   m_ref, acc_ref, sa_ref, sb_ref, mxa_ref, mxb_ref):
    i = pl.program_id(2)
    tq = qt_ref.shape[2]
    nk = k_ref.shape[1] // tk
    nband = gband_ref.shape[0]
    dv = o_ref.shape[0]
    _osm_init(m_ref, acc_ref)
    k_of = lambda c, off: k_ref[c, pl.ds(off, tk), :]

    j_lo = (tq // tk) * i - 1
    n_left = jnp.clip(j_lo, 0, nk)
    j_hi = jnp.clip(j_lo + nband, 0, nk)
    n_far = n_left + nk - j_hi
    cf = cfar_ref[...]

    def far_tile(t):
        j = jnp.where(t < n_left, t, t - n_left + j_hi)
        return jnp.clip(j, 0, nk - 1), jnp.where(t < n_left, cf[:, 0:1], cf[:, 1:2])

    def far_scores(t, buf):
        _attn_scores(tk, far_tile(t)[0], qt_ref, k_of, buf)

    def far_values(t, buf):
        j, sh = far_tile(t)
        _attn_values(tk, j, 2, vt_ref, buf, m_ref, acc_ref, shift=sh)

    _pipelined(n_far, far_scores, far_values, ((sa_ref, mxa_ref), (sb_ref, mxb_ref)), inner)

    d_lo = n_left - j_lo

    def band_scores(t, buf):
        d = jnp.clip(d_lo + t, 0, nband - 1)
        g = gband_ref[d]
        tile = pltpu.roll(jnp.broadcast_to(g, (tk, g.shape[1])), tq + 1, 1, stride=1, stride_axis=0)
        _attn_scores(tk, jnp.clip(j_lo + d, 0, nk - 1), qt_ref, k_of, buf, bias_tile=tile[:, :tq])

    def band_values(t, buf):
        _attn_values(tk, j_lo + d_lo + t, 2, vt_ref, buf, m_ref, acc_ref)

    _pipelined(j_hi - n_left, band_scores, band_values, ((sa_ref, mxa_ref), (sb_ref, mxb_ref)), 1)

    o = _osm_result(acc_ref, 0, dv) - lam_ref[...] * _osm_result(acc_ref, 1, dv)
    o = o * lax.rsqrt(jnp.mean(o * o, axis=0, keepdims=True) + EPS) * gd_ref[...]
    o_ref[...] = (o * out_scale).astype(o_ref.dtype)


def _attn_c(qt, k, vt, gband, cfar, lam, gd, out_scale, tq, tk, inner):
    b, h, _, dqk, s = qt.shape
    dva = vt.shape[2]
    dv = gd.shape[0]
    nband = gband.shape[1]
    return pl.pallas_call(
        functools.partial(_attn_c_kernel, tk, inner, out_scale),
        grid=(b, h, s // tq),
        in_specs=[pl.BlockSpec((None, None, 2, dqk, tq), lambda bi, hi, i: (bi, hi, 0, 0, i)),
                  pl.BlockSpec((None, None, 2, s, dqk), lambda bi, hi, i: (bi, hi, 0, 0, 0)),
                  pl.BlockSpec((None, None, dva, s), lambda bi, hi, i: (bi, hi, 0, 0)),
                  pl.BlockSpec((None, nband, 1, tq + tk), lambda bi, hi, i: (hi, 0, 0, 0)),
                  pl.BlockSpec((None, 1, 2), lambda bi, hi, i: (hi, 0, 0)),
                  _full(lam.shape), _full(gd.shape)],
        out_specs=pl.BlockSpec((None, None, dv, tq), lambda bi, hi, i: (bi, hi, 0, i)),
        out_shape=jax.ShapeDtypeStruct((b, h, dv, s), BF16),
        scratch_shapes=[pltpu.VMEM((2, 1, tq), F32), pltpu.VMEM((2, dva, tq), F32),
                        pltpu.VMEM((2, tk, tq), F32), pltpu.VMEM((2, tk, tq), F32),
                        pltpu.VMEM((2, 1, tq), F32), pltpu.VMEM((2, 1, tq), F32)],
        compiler_params=_cparams(("parallel", "parallel", "arbitrary")),
        name="attn_c",
    )(qt, k, vt, gband, cfar, lam, gd)


def _ssd_prep_kernel(nst, xc_ref, xp_ref, xn_ref, dt_ref, cw_ref, cb_ref, dtb_ref, xa_ref, dl_ref, ext_ref):
    i = pl.program_id(0)
    tm = xc_ref.shape[0]
    halo = V7X_SUBLANES
    pos = i % nst
    keep_prev = (pos != 0).astype(F32)
    keep_next = (pos != nst - 1).astype(F32)
    ext_ref[0:halo, :] = xp_ref[...] * keep_prev
    ext_ref[halo:halo + tm, :] = xc_ref[...]
    ext_ref[halo + tm:, :] = xn_ref[...] * keep_next
    pad_l = (SSM_CONV - 1) // 2
    cw = cw_ref[...]
    acc = jnp.zeros(xc_ref.shape, F32) + cb_ref[...]
    for t in range(SSM_CONV):
        acc = acc + ext_ref[halo - pad_l + t:halo - pad_l + t + tm, :] * cw[t:t + 1, :]
    xa_ref[...] = jax.nn.silu(acc)
    dl_ref[...] = jax.nn.softplus(dt_ref[...] + dtb_ref[...])


def _ssd_prep(xbc, dt, cw, cb, dtb, s, tm):
    t, c = xbc.shape
    nst = s // tm
    hb = tm // V7X_SUBLANES
    nhb = t // V7X_SUBLANES
    return pl.pallas_call(
        functools.partial(_ssd_prep_kernel, nst),
        grid=(t // tm,),
        in_specs=[pl.BlockSpec((tm, c), lambda i: (i, 0)),
                  pl.BlockSpec((V7X_SUBLANES, c), lambda i: (jnp.maximum(i * hb - 1, 0), 0)),
                  pl.BlockSpec((V7X_SUBLANES, c), lambda i: (jnp.minimum((i + 1) * hb, nhb - 1), 0)),
                  pl.BlockSpec((tm, dt.shape[1]), lambda i: (i, 0)),
                  _full(cw.shape), _full(cb.shape), _full(dtb.shape)],
        out_specs=[pl.BlockSpec((tm, c), lambda i: (i, 0)),
                   pl.BlockSpec((tm, dt.shape[1]), lambda i: (i, 0))],
        out_shape=[jax.ShapeDtypeStruct((t, c), F32), jax.ShapeDtypeStruct(dt.shape, F32)],
        scratch_shapes=[pltpu.VMEM((tm + 2 * V7X_SUBLANES, c), F32)],
        compiler_params=_cparams(("parallel",)),
        name="ssd_prep",
    )(xbc, xbc, xbc, dt, cw, cb, dtb)


def _ssd_chunk(reverse, xa, dl, alog, st_ref):
    ln = SSM_CHUNK
    row = lax.broadcasted_iota(jnp.int32, (ln, ln), 0)
    col = lax.broadcasted_iota(jnp.int32, (ln, ln), 1)
    tri = (col >= row) if reverse else (col <= row)
    eye = row == col
    last = 0 if reverse else ln - 1
    lane0 = SSM_HEADS if reverse else 0
    a_neg = -jnp.exp(alog)
    acs = jnp.dot(tri.astype(F32), dl * a_neg, precision=HIGHEST, preferred_element_type=F32)
    dx = SSM_HEADS * SSM_HEAD_DIM
    gw = SSM_STATE
    hpg = SSM_HEADS // SSM_GROUPS
    ys = []
    for g in range(SSM_GROUPS):
        bg = xa[:, dx + g * gw:dx + (g + 1) * gw].astype(BF16)
        cg = xa[:, dx + SSM_GROUPS * gw + g * gw:dx + SSM_GROUPS * gw + (g + 1) * gw].astype(BF16)
        cb = _dot_nt(cg, bg)
        for r in range(hpg):
            h = g * hpg + r
            ln_h = lane0 + h
            ac = acs[:, ln_h:ln_h + 1]
            ar = jnp.sum(jnp.where(eye, ac, 0.0), axis=0, keepdims=True)
            dec = jnp.exp(jnp.where(tri, ac - ar, NEG_BIG))
            xdt = xa[:, h * SSM_HEAD_DIM:(h + 1) * SSM_HEAD_DIM] * dl[:, ln_h:ln_h + 1]
            st = st_ref[h]
            y = _dot((cb * dec).astype(BF16), xdt.astype(BF16))
            y = y + _dot(cg, st.astype(BF16)) * jnp.exp(ac)
            a_last = acs[last:last + 1, ln_h:ln_h + 1]
            st_ref[h] = st * jnp.exp(a_last) + _dot_tn(bg, (xdt * jnp.exp(a_last - ac)).astype(BF16))
            ys.append(y)
    return jnp.concatenate(ys, axis=1)


def _ssd_fwd_kernel(cps, xa_ref, dl_ref, alog_ref, y_ref, st_ref):
    @pl.when(pl.program_id(1) == 0)
    def _():
        st_ref[...] = jnp.zeros(st_ref.shape, F32)

    for c in range(cps):
        sl = slice(c * SSM_CHUNK, (c + 1) * SSM_CHUNK)
        y_ref[sl, :] = _ssd_chunk(False, xa_ref[sl, :], dl_ref[sl, :], alog_ref[...], st_ref)


def _ssd_bwd_kernel(cps, xa_ref, dl_ref, alog_ref, yf_ref, z_ref, dsk_ref, g_ref, o_ref, st_ref):
    @pl.when(pl.program_id(1) == 0)
    def _():
        st_ref[...] = jnp.zeros(st_ref.shape, F32)

    dx = SSM_HEADS * SSM_HEAD_DIM
    for c in reversed(range(cps)):
        sl = slice(c * SSM_CHUNK, (c + 1) * SSM_CHUNK)
        xa = xa_ref[sl, :]
        yb = _ssd_chunk(True, xa, dl_ref[sl, :], alog_ref[...], st_ref)
        y = yf_ref[sl, :] + yb + dsk_ref[...] * xa[:, :dx]
        y = y * jax.nn.silu(z_ref[sl, :])
        o_ref[sl, :] = _rms(y, g_ref[...]).astype(o_ref.dtype)


def _ssd(xa, dl, alog, z, dsk, gssm, b, s, cps):
    t, c = xa.shape
    tm = cps * SSM_CHUNK
    nblk = s // tm
    dx = SSM_HEADS * SSM_HEAD_DIM
    st = pltpu.VMEM((SSM_HEADS, SSM_STATE, SSM_HEAD_DIM), F32)
    fw = lambda w: pl.BlockSpec((tm, w), lambda bi, ci: (bi * nblk + ci, 0))
    bw = lambda w: pl.BlockSpec((tm, w), lambda bi, ci: (bi * nblk + nblk - 1 - ci, 0))
    yf = pl.pallas_call(
        functools.partial(_ssd_fwd_kernel, cps),
        grid=(b, nblk),
        in_specs=[fw(c), fw(dl.shape[1]), _full(alog.shape)],
        out_specs=fw(dx),
        out_shape=jax.ShapeDtypeStruct((t, dx), F32),
        scratch_shapes=[st],
        compiler_params=_cparams(("parallel", "arbitrary")),
        name="ssd_fwd",
    )(xa, dl, alog)
    return pl.pallas_call(
        functools.partial(_ssd_bwd_kernel, cps),
        grid=(b, nblk),
        in_specs=[bw(c), bw(dl.shape[1]), _full(alog.shape), bw(dx), bw(dx), _full(dsk.shape),
                  _full(gssm.shape)],
        out_specs=bw(dx),
        out_shape=jax.ShapeDtypeStruct((t, dx), BF16),
        scratch_shapes=[st],
        compiler_params=_cparams(("parallel", "arbitrary")),
        name="ssd_bwd",
    )(xa, dl, alog, yf, z, dsk, gssm)


def _sgu_kernel(nck, u_ref, v_ref, w_ref, bias_ref, o_ref):
    for c in range(nck):
        sl = slice(c * SGU_CHUNK, (c + 1) * SGU_CHUNK)
        v = v_ref[sl, :]
        sv = [_dot(w_ref[g], v[:, g * SGU_GROUP_DIM:(g + 1) * SGU_GROUP_DIM]) for g in range(SGU_GROUPS)]
        o_ref[sl, :] = (u_ref[sl, :] * (jnp.concatenate(sv, axis=1) + bias_ref[...])).astype(o_ref.dtype)


def _sgu(u, v, w, bias, nck):
    t, c = u.shape
    tm = nck * SGU_CHUNK
    return pl.pallas_call(
        functools.partial(_sgu_kernel, nck),
        grid=(t // tm,),
        in_specs=[pl.BlockSpec((tm, c), lambda i: (i, 0)), pl.BlockSpec((tm, c), lambda i: (i, 0)),
                  _full(w.shape), _full(bias.shape)],
        out_specs=pl.BlockSpec((tm, c), lambda i: (i, 0)),
        out_shape=jax.ShapeDtypeStruct((t, c), BF16),
        compiler_params=_cparams(("parallel",)),
        name="sgu",
    )(u, v, w, bias)


def _merge_kernel(x_ref, gmix_ref, oat_ref, ob_ref, oct_ref, od_ref, wg_ref, wbr_ref, wo_ref, o_ref):
    x = x_ref[...]
    dm = x.shape[1]
    hb = _rms(x, gmix_ref[...]).astype(BF16)
    merged = jnp.zeros(x.shape, F32)
    for n, (br, transposed) in enumerate(((oat_ref, True), (ob_ref, False), (oct_ref, True), (od_ref, False))):
        gate = jax.nn.sigmoid(_dot(hb, wg_ref[:, n * dm:(n + 1) * dm]))
        wide = _dot_tn(br[...], wbr_ref[n]) if transposed else _dot(br[...], wbr_ref[n])
        merged = merged + gate * wide
    o_ref[...] = x + _dot(merged.astype(BF16), wo_ref[...])


def _merge(x, gmix, oat, ob, oct, od, wg, wbr, wo, s, tm):
    t, dm = x.shape
    nst = s // tm
    row = lambda w: pl.BlockSpec((tm, w), lambda i: (i, 0))
    lanes = pl.BlockSpec((None, BRANCH_WIDTH, tm), lambda i: (i // nst, 0, i % nst))
    return pl.pallas_call(
        _merge_kernel,
        grid=(t // tm,),
        in_specs=[row(dm), _full(gmix.shape), lanes, row(BRANCH_WIDTH), lanes, row(BRANCH_WIDTH),
                  _full(wg.shape), _full(wbr.shape), _full(wo.shape)],
        out_specs=row(dm),
        out_shape=jax.ShapeDtypeStruct((t, dm), F32),
        compiler_params=_cparams(("parallel",)),
        name="merge",
    )(x, gmix, oat, ob, oct, od, wg, wbr, wo)


def _router_kernel(x_ref, g_ref, wrt_ref, h_ref, aff_ref):
    h = _rms(x_ref[...], g_ref[...])
    h_ref[...] = h.astype(BF16)
    logits = lax.dot_general(wrt_ref[...], h, (((1,), (1,)), ((), ())), precision=HIGHEST,
                             preferred_element_type=F32)
    e = jnp.exp(logits - jnp.max(logits, axis=0, keepdims=True))
    aff_ref[...] = e / jnp.sum(e, axis=0, keepdims=True)


def _router(x, g, wrt, b, s, tm):
    t, dm = x.shape
    ne = wrt.shape[0]
    nst = s // tm
    return pl.pallas_call(
        _router_kernel,
        grid=(t // tm,),
        in_specs=[pl.BlockSpec((tm, dm), lambda i: (i, 0)), _full(g.shape), _full(wrt.shape)],
        out_specs=[pl.BlockSpec((tm, dm), lambda i: (i, 0)),
                   pl.BlockSpec((None, ne, tm), lambda i: (i // nst, 0, i % nst))],
        out_shape=[jax.ShapeDtypeStruct((t, dm), BF16), jax.ShapeDtypeStruct((b, ne, s), F32)],
        compiler_params=_cparams(("parallel",)),
        name="router",
    )(x, g, wrt)


def _select_kernel(cap, aff_ref, gate_ref, pos_ref):
    ne, s = aff_ref.shape
    bits = pltpu.bitcast(aff_ref[...], jnp.int32)
    capf = jnp.float32(cap)

    def bisect(i, thr):
        cand = thr | lax.shift_left(jnp.int32(1), 30 - i)
        cnt = jnp.sum((bits >= cand).astype(F32), axis=1, keepdims=True)
        return jnp.where(cnt >= capf, cand, thr)

    thr = lax.fori_loop(0, 31, bisect, jnp.zeros((ne, 1), jnp.int32))
    need = capf - jnp.sum((bits > thr).astype(F32), axis=1, keepdims=True)
    lanes = V7X_LANES
    r = lax.broadcasted_iota(jnp.int32, (lanes, lanes), 0)
    c = lax.broadcasted_iota(jnp.int32, (lanes, lanes), 1)
    before = (r < c).astype(BF16)

    def block(jb, carry):
        n_eq, n_keep = carry
        off = pl.multiple_of(jb * lanes, lanes)
        a = aff_ref[:, pl.ds(off, lanes)]
        bb = pltpu.bitcast(a, jnp.int32)
        eq = bb == thr
        eqf = eq.astype(F32)
        keep = jnp.logical_or(bb > thr, jnp.logical_and(eq, n_eq + _dot(eqf.astype(BF16), before) < need))
        keepf = keep.astype(F32)
        gate_ref[:, pl.ds(off, lanes)] = jnp.where(keep, a, 0.0)
        pos_ref[:, pl.ds(off, lanes)] = jnp.where(keep, n_keep + _dot(keepf.astype(BF16), before), -1.0)
        return n_eq + jnp.sum(eqf, axis=1, keepdims=True), n_keep + jnp.sum(keepf, axis=1, keepdims=True)

    zero = jnp.zeros((ne, 1), F32)
    lax.fori_loop(0, s // lanes, block, (zero, zero))


def _select(aff, cap):
    b, ne, s = aff.shape
    spec = pl.BlockSpec((None, ne, s), lambda i: (i, 0, 0))
    return pl.pallas_call(
        functools.partial(_select_kernel, cap),
        grid=(b,),
        in_specs=[spec],
        out_specs=[spec, spec],
        out_shape=[jax.ShapeDtypeStruct(aff.shape, F32), jax.ShapeDtypeStruct(aff.shape, F32)],
        compiler_params=_cparams(("parallel",)),
        name="moe_select",
    )(aff)


def _moe_kernel(rb, cnt_ref, base_ref, h_ref, posr_ref, posc_ref, gates_ref, wg_ref, wu_ref, wd_ref, o_ref):
    i = pl.program_id(0)
    e = pl.program_id(1)

    @pl.when(e == 0)
    def _():
        o_ref[...] = jnp.zeros(o_ref.shape, F32)

    tt = h_ref.shape[0]
    base = base_ref[i, e].astype(F32)
    lane = lax.broadcasted_iota(jnp.int32, gates_ref.shape, 1)
    col = lambda ref: jnp.sum(jnp.where(lane == e, ref[...], 0.0), axis=1, keepdims=True)
    gcol = col(gates_ref)
    posc = col(posc_ref) - base
    posr = posr_ref[...] - base
    h = h_ref[...]

    def block(bi, carry):
        lo = (bi * rb).astype(F32)
        slot_r = lax.broadcasted_iota(jnp.int32, (rb, 1), 0).astype(F32) + lo
        slot_c = lax.broadcasted_iota(jnp.int32, (1, rb), 1).astype(F32) + lo
        xc = _dot((posr == slot_r).astype(BF16), h).astype(BF16)
        hid = jax.nn.silu(_dot(xc, wg_ref[...])) * _dot(xc, wu_ref[...])
        y = _dot(hid.astype(BF16), wd_ref[...])
        o_ref[...] += _dot((posc == slot_c).astype(BF16), y.astype(BF16)) * gcol
        return carry

    lax.fori_loop(0, (cnt_ref[i, e] + rb - 1) // rb, block, 0)


def _moe(h, pos_lane, pos_tm, gates_tm, cnt, base, wg, wu, wd, tt, rb):
    t, dm = h.shape
    ne, _, dff = wg.shape
    s = pos_lane.shape[2]
    nst = s // tt
    return pl.pallas_call(
        functools.partial(_moe_kernel, rb),
        grid_spec=pltpu.PrefetchScalarGridSpec(
            num_scalar_prefetch=2,
            grid=(t // tt, ne),
            in_specs=[pl.BlockSpec((tt, dm), lambda i, e, *_: (i, 0)),
                      pl.BlockSpec((None, 1, tt), lambda i, e, *_: ((i // nst) * ne + e, 0, i % nst)),
                      pl.BlockSpec((tt, ne), lambda i, e, *_: (i, 0)),
                      pl.BlockSpec((tt, ne), lambda i, e, *_: (i, 0)),
                      pl.BlockSpec((None, dm, dff), lambda i, e, *_: (e, 0, 0)),
                      pl.BlockSpec((None, dm, dff), lambda i, e, *_: (e, 0, 0)),
                      pl.BlockSpec((None, dff, dm), lambda i, e, *_: (e, 0, 0))],
            out_specs=pl.BlockSpec((tt, dm), lambda i, e, *_: (i, 0))),
        out_shape=jax.ShapeDtypeStruct((t, dm), F32),
        compiler_params=_cparams(("parallel", "arbitrary")),
        name="moe_ffn",
    )(cnt, base, h, pos_lane, pos_tm, gates_tm, wg, wu, wd)


def _ple_kernel(final, x_ref, moe_ref, p_ref, g_ref, wpg_ref, wp_ref, gf_ref, o_ref):
    x = x_ref[...] + moe_ref[...]
    gate = jax.nn.sigmoid(_dot(_rms(x, g_ref[...]).astype(BF16), wpg_ref[...]))
    y = x + gate * _dot(p_ref[...].astype(BF16), wp_ref[...])
    o_ref[...] = _rms(y, gf_ref[...]) if final else y


def _ple(x, moe, p, g, wpg, wp, gf, final, tm):
    t, dm = x.shape
    return pl.pallas_call(
        functools.partial(_ple_kernel, final),
        grid=(t // tm,),
        in_specs=[pl.BlockSpec((tm, dm), lambda i: (i, 0)), pl.BlockSpec((tm, dm), lambda i: (i, 0)),
                  pl.BlockSpec((tm, p.shape[1]), lambda i: (i, 0)),
                  _full(g.shape), _full(wpg.shape), _full(wp.shape), _full(gf.shape)],
        out_specs=pl.BlockSpec((tm, dm), lambda i: (i, 0)),
        out_shape=jax.ShapeDtypeStruct((t, dm), F32),
        compiler_params=_cparams(("parallel",)),
        name="ple",
    )(x, moe, p, g, wpg, wp, gf)


def _rope_table(s):
    half = A_HEAD_DIM // 2
    pos = jnp.arange(s, dtype=jnp.int32)
    freqs = ROPE_THETA ** (-jnp.arange(0, half, 2, dtype=F32) / half)
    ang_r = freqs[:, None] * (pos // GRID_W).astype(F32)[None, :]
    ang_c = freqs[:, None] * (pos % GRID_W).astype(F32)[None, :]
    return jnp.concatenate([jnp.cos(ang_r), jnp.sin(ang_r), jnp.cos(ang_c), jnp.sin(ang_c)], axis=0)


def _t5_bucket(rel):
    nb = REL_BUCKETS // 2
    max_exact = nb // 2
    ret = jnp.where(rel > 0, nb, 0)
    r = jnp.abs(rel)
    rf = jnp.maximum(r, 1).astype(F32)
    large = max_exact + (jnp.log(rf / max_exact) / math.log(REL_MAX_DIST / max_exact)
                         * (nb - max_exact)).astype(jnp.int32)
    large = jnp.minimum(large, nb - 1)
    return ret + jnp.where(r < max_exact, r, large)


def _bias_tables(rel_bias, tq, tk):
    assert tk >= REL_MAX_DIST and tq % tk == 0
    rep = tq // tk
    offs = jnp.arange(-1, rep + 1, dtype=jnp.int32) * tk
    u = jnp.arange(tq + tk, dtype=jnp.int32)
    rel = offs[:, None] + (tk - 1) - u[None, :]
    gband = jnp.transpose(rel_bias[_t5_bucket(rel)], (2, 0, 1))[:, :, None, :].astype(F32) * LOG2E
    far = jnp.array([-REL_MAX_DIST, REL_MAX_DIST], dtype=jnp.int32)
    cfar = jnp.transpose(rel_bias[_t5_bucket(far)], (1, 0))[:, None, :].astype(F32) * LOG2E
    return gband, cfar


def _tile(n, pref):
    t = min(n, pref)
    assert n % t == 0
    return t


def kernel(x, p, rel_bias, g_mix, w_in, g_qnorm, g_knorm, conv_w, conv_b, dt_bias_f, dt_bias_b, a_log_f, a_log_b, d_skip, g_ssm, lambda_q1, lambda_k1, lambda_q2, lambda_k2, g_diff, g_sgu, w_spatial, b_spatial, w_branch, w_branch_gate, w_out, g_moe, w_router, w_exp_gate, w_exp_up, w_exp_down, g_ple, w_ple_gate, w_ple, g_final):
    b, s, dm = x.shape
    depth = w_in.shape[0]
    t = b * s
    cap = EC_CAPACITY * s // N_EXPERTS

    tm_proj = _tile(s, 512)
    tm_prep = _tile(s, 1024)
    tq_a, tk_a = _tile(s, 512), _tile(s, 1024)
    tq_c, tk_c = _tile(s, 512), _tile(s, 512)
    cps = _tile(s // SSM_CHUNK, 4)
    nck = _tile(s // SGU_CHUNK, 8)
    tt_moe = _tile(s, 1024)

    rope_tab = _rope_table(s)
    gband, cfar = _bias_tables(rel_bias, tq_c, tk_c)
    hd = A_HEAD_DIM

    sizes = (A_Q_HEADS * hd, A_KV_HEADS * hd, A_KV_HEADS * hd,
             SSM_HEADS * SSM_HEAD_DIM, SSM_HEADS * SSM_HEAD_DIM, SSM_GROUPS * SSM_STATE,
             SSM_GROUPS * SSM_STATE, SSM_HEADS, SSM_HEADS,
             DIFF_HEADS * 2 * DIFF_QK_DIM, DIFF_HEADS * 2 * DIFF_QK_DIM, DIFF_HEADS * DIFF_V_DIM,
             2 * SGU_GROUPS * SGU_GROUP_DIM)
    off = [0]
    for sz in sizes:
        off.append(off[-1] + sz)
    o_a0, o_b0, o_dt0, o_c0, o_d0, o_end = off[0], off[3], off[7], off[9], off[12], off[13]
    dt_pad = V7X_LANES - 2 * SSM_HEADS

    def lane_row(v, width=None):
        v = v.astype(F32).reshape(1, -1)
        if width is not None and v.shape[1] < width:
            v = jnp.pad(v, ((0, 0), (0, width - v.shape[1])))
        return v

    xt = x.reshape(t, dm)
    for i in range(depth):
        wi = w_in[i]
        wat = wi[:, o_a0:o_b0].T.astype(BF16)
        wb = jnp.pad(wi[:, o_b0:o_c0], ((0, 0), (0, dt_pad))).astype(BF16)
        wct = wi[:, o_c0:o_d0].T.astype(BF16)
        wd = wi[:, o_d0:o_end].astype(BF16)
        gmix = lane_row(g_mix[i])
        col = lambda v: v.astype(F32).reshape(-1, 1)
        (qt, kh, vt, z, xbc, dt, cqt, ckh, cvt, du, dv) = _inproj(
            xt, gmix, wat, wb, wct, wd, rope_tab, col(g_qnorm[i]), col(g_knorm[i]), lane_row(g_sgu[i]),
            b, s, tm_proj)

        o_at = _attn_a(qt.reshape(b, A_Q_HEADS, hd, s), kh, vt, tq_a, tk_a, ATTN_INNER).reshape(b, -1, s)

        xa, dl = _ssd_prep(xbc, dt, conv_w[i].astype(F32), lane_row(conv_b[i]),
                           lane_row(jnp.concatenate([dt_bias_f[i], dt_bias_b[i]]), V7X_LANES), s, tm_prep)
        o_b = _ssd(xa, dl, lane_row(jnp.concatenate([a_log_f[i], a_log_b[i]]), V7X_LANES), z,
                   lane_row(jnp.repeat(d_skip[i], SSM_HEAD_DIM)), lane_row(g_ssm[i]), b, s, cps)

        lam_init = 0.8 - 0.6 * math.exp(-0.3 * i)
        lam = (jnp.exp(jnp.sum(lambda_q1[i].astype(F32) * lambda_k1[i].astype(F32)))
               - jnp.exp(jnp.sum(lambda_q2[i].astype(F32) * lambda_k2[i].astype(F32))) + lam_init).reshape(1, 1)
        o_ct = _attn_c(cqt.reshape(b, DIFF_HEADS, 2, DIFF_QK_DIM, s), ckh, cvt, gband, cfar, lam, col(g_diff[i]),
                       1.0 - lam_init, tq_c, tk_c, ATTN_INNER).reshape(b, -1, s)

        sgu_bias = jnp.repeat(b_spatial[i].T.astype(F32), SGU_GROUP_DIM, axis=1)
        o_d = _sgu(du, dv, w_spatial[i].astype(BF16), sgu_bias, nck)

        xt = _merge(xt, gmix, o_at, o_b, o_ct, o_d, w_branch_gate[i].astype(BF16),
                    w_branch[i].astype(BF16), w_out[i].astype(BF16), s, tm_proj)

        h2, aff = _router(xt, lane_row(g_moe[i]), w_router[i].T.astype(F32), b, s, tm_proj)
        gates, pos = _select(aff, cap)
        token_major = lambda a: jnp.transpose(a, (0, 2, 1)).reshape(t, N_EXPERTS)
        cnt = jnp.sum((pos >= 0).reshape(b, N_EXPERTS, s // tt_moe, tt_moe), axis=3, dtype=jnp.int32)
        tile_major = lambda a: jnp.transpose(a, (0, 2, 1)).reshape(t // tt_moe, N_EXPERTS)
        moe = _moe(h2, pos.reshape(b * N_EXPERTS, 1, s), token_major(pos), token_major(gates),
                   tile_major(cnt), tile_major(jnp.cumsum(cnt, axis=2) - cnt),
                   w_exp_gate[i].astype(BF16), w_exp_up[i].astype(BF16), w_exp_down[i].astype(BF16),
                   tt_moe, MOE_ROW_BLOCK)

        xt = _ple(xt, moe, p[i].reshape(t, -1), lane_row(g_ple[i]), w_ple_gate[i].astype(BF16),
                  w_ple[i].astype(BF16), lane_row(g_final), i == depth - 1, tm_proj)

    return xt.reshape(b, s, dm)
```

```python
import functools
import math

import jax
import jax.numpy as jnp
from jax import lax
from jax.experimental import pallas as pl
from jax.experimental.pallas import tpu as pltpu

F32 = jnp.float32
BF16 = jnp.bfloat16
HIGHEST = lax.Precision.HIGHEST

EPS = 1e-6
GRID_W = 64
A_Q_HEADS, A_KV_HEADS, A_HEAD_DIM = 4, 2, 64
ROPE_THETA = 10000.0
SSM_HEADS, SSM_HEAD_DIM, SSM_GROUPS, SSM_STATE, SSM_CONV, SSM_CHUNK = 4, 64, 2, 128, 5, 128
DIFF_HEADS, DIFF_QK_DIM, DIFF_V_DIM = 4, 32, 64
REL_BUCKETS, REL_MAX_DIST = 32, 128
SGU_GROUPS, SGU_GROUP_DIM, SGU_CHUNK = 4, 64, 128
N_EXPERTS, EC_CAPACITY = 16, 2
N_BRANCHES, BRANCH_WIDTH = 4, 256

V7X_LANES = 128
V7X_SUBLANES = 8
BF16_SUBLANE_PACK = 16
V7X_VMEM_LIMIT_BYTES = 56 * 1024 * 1024

NEG_BIG = -1e30
LOG2E = math.log2(math.e)
MOE_ROW_BLOCK = 128
ATTN_INNER = 4


def _cparams(sem):
    return pltpu.CompilerParams(dimension_semantics=sem, vmem_limit_bytes=V7X_VMEM_LIMIT_BYTES)


def _rms(x, g):
    return x * lax.rsqrt(jnp.mean(x * x, axis=-1, keepdims=True) + EPS) * g


def _dot(a, b):
    return jnp.dot(a, b, preferred_element_type=F32)


def _dot_nt(a, b):
    return lax.dot_general(a, b, (((1,), (1,)), ((), ())), preferred_element_type=F32)


def _dot_tn(a, b):
    return lax.dot_general(a, b, (((0,), (0,)), ((), ())), preferred_element_type=F32)


def _full(shape):
    n = len(shape)
    return pl.BlockSpec(shape, lambda *_: (0,) * n)


def _norm_rope_t(x, g_col, tab):
    xn = x * lax.rsqrt(jnp.mean(x * x, axis=0, keepdims=True) + EPS) * g_col
    qd = A_HEAD_DIM // 4
    cr, sr, cc, sc = (tab[i * qd:(i + 1) * qd] for i in range(4))
    r1, r2, c1, c2 = (xn[i * qd:(i + 1) * qd] for i in range(4))
    return jnp.concatenate([r1 * cr - r2 * sr, r1 * sr + r2 * cr, c1 * cc - c2 * sc, c1 * sc + c2 * cc], axis=0)


def _inproj_kernel(x_ref, gmix_ref, wat_ref, wb_ref, wct_ref, wd_ref, tab_ref, gq_ref, gk_ref, gsgu_ref,
                   qa_ref, ka_ref, va_ref, z_ref, xbc_ref, dt_ref, cq_ref, ck_ref, cv_ref,
                   du_ref, dv_ref):
    hb = _rms(x_ref[...], gmix_ref[...]).astype(BF16)
    tm = hb.shape[0]
    hd = A_HEAD_DIM
    ones = jnp.ones((BF16_SUBLANE_PACK, tm), BF16)

    at = _dot_nt(wat_ref[...], hb)
    tab = tab_ref[...]
    q = [_norm_rope_t(at[h * hd:(h + 1) * hd], gq_ref[...], tab) for h in range(A_Q_HEADS)]
    qa_ref[...] = (jnp.concatenate(q, axis=0) * (hd ** -0.5 * LOG2E)).astype(BF16)
    k0 = A_Q_HEADS * hd
    v0 = k0 + A_KV_HEADS * hd
    for g in range(A_KV_HEADS):
        kt = _norm_rope_t(at[k0 + g * hd:k0 + (g + 1) * hd], gk_ref[...], tab)
        ka_ref[g] = jnp.transpose(kt).astype(BF16)
        va_ref[g, :hd] = at[v0 + g * hd:v0 + (g + 1) * hd].astype(BF16)
        va_ref[g, hd:] = ones

    b = _dot(hb, wb_ref[...])
    dx = SSM_HEADS * SSM_HEAD_DIM
    conv_ch = dx + 2 * SSM_GROUPS * SSM_STATE
    z_ref[...] = b[:, :dx]
    xbc_ref[...] = b[:, dx:dx + conv_ch]
    dt_ref[...] = b[:, dx + conv_ch:]

    ct = _dot_nt(wct_ref[...], hb)
    nqc = DIFF_HEADS * 2 * DIFF_QK_DIM
    cq_ref[...] = (ct[:nqc] * (DIFF_QK_DIM ** -0.5 * LOG2E)).astype(BF16)
    ck = jnp.transpose(ct[nqc:2 * nqc]).astype(BF16)
    for h in range(DIFF_HEADS):
        for c in range(2):
            lo = (2 * h + c) * DIFF_QK_DIM
            ck_ref[h, c] = ck[:, lo:lo + DIFF_QK_DIM]
        cv_ref[h, :DIFF_V_DIM] = ct[2 * nqc + h * DIFF_V_DIM:2 * nqc + (h + 1) * DIFF_V_DIM].astype(BF16)
        cv_ref[h, DIFF_V_DIM:] = ones

    d = jax.nn.gelu(_dot(hb, wd_ref[...]))
    w = SGU_GROUPS * SGU_GROUP_DIM
    du_ref[...] = d[:, :w]
    dv_ref[...] = _rms(d[:, w:], gsgu_ref[...]).astype(BF16)


def _inproj(x, gmix, wat, wb, wct, wd, tab, gq, gk, gsgu, b, s, tm):
    t, dm = x.shape
    nst = s // tm
    row = lambda w: pl.BlockSpec((tm, w), lambda i: (i, 0))
    lanes = lambda *lead: pl.BlockSpec((None,) + lead + (tm,), lambda i: (i // nst,) + (0,) * len(lead) + (i % nst,))
    hd, pk = A_HEAD_DIM, BF16_SUBLANE_PACK
    specs_shapes = [
        (lanes(A_Q_HEADS * hd), (b, A_Q_HEADS * hd, s), BF16),
        (pl.BlockSpec((None, A_KV_HEADS, tm, hd), lambda i: (i // nst, 0, i % nst, 0)), (b, A_KV_HEADS, s, hd), BF16),
        (lanes(A_KV_HEADS, hd + pk), (b, A_KV_HEADS, hd + pk, s), BF16),
        (row(256), (t, 256), F32), (row(768), (t, 768), F32), (row(128), (t, 128), F32),
        (lanes(DIFF_HEADS * 2 * DIFF_QK_DIM), (b, DIFF_HEADS * 2 * DIFF_QK_DIM, s), BF16),
        (pl.BlockSpec((None, DIFF_HEADS, 2, tm, DIFF_QK_DIM), lambda i: (i // nst, 0, 0, i % nst, 0)),
         (b, DIFF_HEADS, 2, s, DIFF_QK_DIM), BF16),
        (lanes(DIFF_HEADS, DIFF_V_DIM + pk), (b, DIFF_HEADS, DIFF_V_DIM + pk, s), BF16),
        (row(256), (t, 256), F32), (row(256), (t, 256), BF16)]
    return pl.pallas_call(
        _inproj_kernel,
        grid=(t // tm,),
        in_specs=[row(dm), _full(gmix.shape), _full(wat.shape), _full(wb.shape), _full(wct.shape),
                  _full(wd.shape), pl.BlockSpec((tab.shape[0], tm), lambda i: (0, i % nst)),
                  _full(gq.shape), _full(gk.shape), _full(gsgu.shape)],
        out_specs=[sp for sp, _, _ in specs_shapes],
        out_shape=[jax.ShapeDtypeStruct(shp, dt) for _, shp, dt in specs_shapes],
        compiler_params=_cparams(("parallel",)),
        name="inproj",
    )(x, gmix, wat, wb, wct, wd, tab, gq, gk, gsgu)


def _osm_tile(s, mt, shift, vt, m_ref, acc_ref, c):
    m_old = m_ref[c]
    if shift is not None:
        mt = mt + shift
    m_new = jnp.maximum(m_old, mt)
    p = jnp.exp2(s - (m_new if shift is None else m_new - shift))
    acc_ref[c] = jnp.exp2(m_old - m_new) * acc_ref[c] + _dot(vt, p.astype(BF16))
    m_ref[c] = m_new


def _osm_init(m_ref, acc_ref):
    m_ref[...] = jnp.full(m_ref.shape, NEG_BIG, F32)
    acc_ref[...] = jnp.zeros(acc_ref.shape, F32)


def _osm_result(acc_ref, c, dv):
    acc = acc_ref[c]
    return acc[:dv] / acc[dv:dv + 1]


def _attn_scores(tk, j, qt_ref, k_of, buf, bias_tile=None):
    s_ref, mx_ref = buf
    off = pl.multiple_of(j * tk, tk)
    for c in range(qt_ref.shape[0]):
        s = _dot(k_of(c, off), qt_ref[c])
        if bias_tile is not None:
            s = s + bias_tile
        s_ref[c] = s
        mx_ref[c] = jnp.max(s, axis=0, keepdims=True)


def _attn_values(tk, j, nmaps, vt_ref, buf, m_ref, acc_ref, shift=None):
    s_ref, mx_ref = buf
    vt = vt_ref[:, pl.ds(pl.multiple_of(j * tk, tk), tk)]
    for c in range(nmaps):
        _osm_tile(s_ref[c], mx_ref[c], shift, vt, m_ref, acc_ref, c)


def _pipelined(n, scores, values, s_refs, inner):
    last = jnp.maximum(n - 1, 0)
    span = 2 * inner
    scores(jnp.minimum(0, last), s_refs[0])

    def double(g):
        scores(jnp.minimum(g + 1, last), s_refs[1])
        values(g, s_refs[0])
        scores(jnp.minimum(g + 2, last), s_refs[0])
        values(g + 1, s_refs[1])

    def unrolled(t, carry):
        for w in range(inner):
            double(span * t + 2 * w)
        return carry

    lax.fori_loop(0, n // span, unrolled, 0)
    base = (n // span) * span

    def rolled(t, carry):
        double(base + 2 * t)
        return carry

    lax.fori_loop(0, (n - base) // 2, rolled, 0)

    @pl.when(n % 2 == 1)
    def _():
        values(n - 1, s_refs[0])


def _attn_a_kernel(tk, inner, qt_ref, k_ref, vt_ref, o_ref, m_ref, acc_ref, sa_ref, sb_ref, mxa_ref, mxb_ref):
    _osm_init(m_ref, acc_ref)
    nq = qt_ref.shape[0]
    k_of = lambda c, off: k_ref[pl.ds(off, tk), :]
    _pipelined(k_ref.shape[0] // tk,
               lambda j, buf: _attn_scores(tk, j, qt_ref, k_of, buf),
               lambda j, buf: _attn_values(tk, j, nq, vt_ref, buf, m_ref, acc_ref),
               ((sa_ref, mxa_ref), (sb_ref, mxb_ref)), inner)
    for c in range(nq):
        o_ref[c] = _osm_result(acc_ref, c, o_ref.shape[1]).astype(o_ref.dtype)


def _attn_a(qt, k, vt, tq, tk, inner):
    b, hq, d, s = qt.shape
    hkv = k.shape[1]
    rep = hq // hkv
    da = vt.shape[2]
    qspec = pl.BlockSpec((None, rep, d, tq), lambda bi, g, i: (bi, g, 0, i))
    return pl.pallas_call(
        functools.partial(_attn_a_kernel, tk, inner),
        grid=(b, hkv, s // tq),
        in_specs=[qspec,
                  pl.BlockSpec((None, None, s, d), lambda bi, g, i: (bi, g, 0, 0)),
                  pl.BlockSpec((None, None, da, s), lambda bi, g, i: (bi, g, 0, 0))],
        out_specs=qspec,
        out_shape=jax.ShapeDtypeStruct(qt.shape, BF16),
        scratch_shapes=[pltpu.VMEM((rep, 1, tq), F32), pltpu.VMEM((rep, da, tq), F32),
                        pltpu.VMEM((rep, tk, tq), F32), pltpu.VMEM((rep, tk, tq), F32),
                        pltpu.VMEM((rep, 1, tq), F32), pltpu.VMEM((rep, 1, tq), F32)],
        compiler_params=_cparams(("parallel", "parallel", "arbitrary")),
        name="attn_a",
    )(qt, k, vt)


def _attn_c_kernel(tk, inner, out_scale, qt_ref, k_ref, vt_ref, gband_ref, cfar_ref, lam_ref, gd_ref, o_ref,
                   m_ref, acc_ref, sa_ref, sb_ref, mxa_ref, mxb_ref):
    i = pl.program_id(2)
    tq = qt_ref.shape[2]
    nk = k_ref.shape[1] // tk
    nband = gband_ref.shape[0]
    dv = o_ref.shape[0]
    _osm_init(m_ref, acc_ref)
    k_of = lambda c, off: k_ref[c, pl.ds(off, tk), :]

    j_lo = (tq // tk) * i - 1
    n_left = jnp.clip(j_lo, 0, nk)
    j_hi = jnp.clip(j_lo + nband, 0, nk)
    n_far = n_left + nk - j_hi
    cf = cfar_ref[...]

    def far_tile(t):
        j = jnp.where(t < n_left, t, t - n_left + j_hi)
        return jnp.clip(j, 0, nk - 1), jnp.where(t < n_left, cf[:, 0:1], cf[:, 1:2])

    def far_scores(t, buf):
        _attn_scores(tk, far_tile(t)[0], qt_ref, k_of, buf)

    def far_values(t, buf):
        j, sh = far_tile(t)
        _attn_values(tk, j, 2, vt_ref, buf, m_ref, acc_ref, shift=sh)

    _pipelined(n_far, far_scores, far_values, ((sa_ref, mxa_ref), (sb_ref, mxb_ref)), inner)

    d_lo = n_left - j_lo

    def band_scores(t, buf):
        d = jnp.clip(d_lo + t, 0, nband - 1)
        g = gband_ref[d]
        tile = pltpu.roll(jnp.broadcast_to(g, (tk, g.shape[1])), tq + 1, 1, stride=1, stride_axis=0)
        _attn_scores(tk, jnp.clip(j_lo + d, 0, nk - 1), qt_ref, k_of, buf, bias_tile=tile[:, :tq])

    def band_values(t, buf):
        _attn_values(tk, j_lo + d_lo + t, 2, vt_ref, buf, m_ref, acc_ref)

    _pipelined(j_hi - n_left, band_scores, band_values, ((sa_ref, mxa_ref), (sb_ref, mxb_ref)), 1)

    o = _osm_result(acc_ref, 0, dv) - lam_ref[...] * _osm_result(acc_ref, 1, dv)
    o = o * lax.rsqrt(jnp.mean(o * o, axis=0, keepdims=True) + EPS) * gd_ref[...]
    o_ref[...] = (o * out_scale).astype(o_ref.dtype)


def _attn_c(qt, k, vt, gband, cfar, lam, gd, out_scale, tq, tk, inner):
    b, h, _, dqk, s = qt.shape
    dva = vt.shape[2]
    dv = gd.shape[0]
    nband = gband.shape[1]
    return pl.pallas_call(
        functools.partial(_attn_c_kernel, tk, inner, out_scale),
        grid=(b, h, s // tq),
        in_specs=[pl.BlockSpec((None, None, 2, dqk, tq), lambda bi, hi, i: (bi, hi, 0, 0, i)),
                  pl.BlockSpec((None, None, 2, s, dqk), lambda bi, hi, i: (bi, hi, 0, 0, 0)),
                  pl.BlockSpec((None, None, dva, s), lambda bi, hi, i: (bi, hi, 0, 0)),
                  pl.BlockSpec((None, nband, 1, tq + tk), lambda bi, hi, i: (hi, 0, 0, 0)),
                  pl.BlockSpec((None, 1, 2), lambda bi, hi, i: (hi, 0, 0)),
                  _full(lam.shape), _full(gd.shape)],
        out_specs=pl.BlockSpec((None, None, dv, tq), lambda bi, hi, i: (bi, hi, 0, i)),
        out_shape=jax.ShapeDtypeStruct((b, h, dv, s), BF16),
        scratch_shapes=[pltpu.VMEM((2, 1, tq), F32), pltpu.VMEM((2, dva, tq), F32),
                        pltpu.VMEM((2, tk, tq), F32), pltpu.VMEM((2, tk, tq), F32),
                        pltpu.VMEM((2, 1, tq), F32), pltpu.VMEM((2, 1, tq), F32)],
        compiler_params=_cparams(("parallel", "parallel", "arbitrary")),
        name="attn_c",
    )(qt, k, vt, gband, cfar, lam, gd)


def _ssd_prep_kernel(nst, xc_ref, xp_ref, xn_ref, dt_ref, cw_ref, cb_ref, dtb_ref, xa_ref, dl_ref, ext_ref):
    i = pl.program_id(0)
    tm = xc_ref.shape[0]
    halo = V7X_SUBLANES
    pos = i % nst
    keep_prev = (pos != 0).astype(F32)
    keep_next = (pos != nst - 1).astype(F32)
    ext_ref[0:halo, :] = xp_ref[...] * keep_prev
    ext_ref[halo:halo + tm, :] = xc_ref[...]
    ext_ref[halo + tm:, :] = xn_ref[...] * keep_next
    pad_l = (SSM_CONV - 1) // 2
    cw = cw_ref[...]
    acc = jnp.zeros(xc_ref.shape, F32) + cb_ref[...]
    for t in range(SSM_CONV):
        acc = acc + ext_ref[halo - pad_l + t:halo - pad_l + t + tm, :] * cw[t:t + 1, :]
    xa_ref[...] = jax.nn.silu(acc)
    dl_ref[...] = jax.nn.softplus(dt_ref[...] + dtb_ref[...])


def _ssd_prep(xbc, dt, cw, cb, dtb, s, tm):
    t, c = xbc.shape
    nst = s // tm
    hb = tm // V7X_SUBLANES
    nhb = t // V7X_SUBLANES
    return pl.pallas_call(
        functools.partial(_ssd_prep_kernel, nst),
        grid=(t // tm,),
        in_specs=[pl.BlockSpec((tm, c), lambda i: (i, 0)),
                  pl.BlockSpec((V7X_SUBLANES, c), lambda i: (jnp.maximum(i * hb - 1, 0), 0)),
                  pl.BlockSpec((V7X_SUBLANES, c), lambda i: (jnp.minimum((i + 1) * hb, nhb - 1), 0)),
                  pl.BlockSpec((tm, dt.shape[1]), lambda i: (i, 0)),
                  _full(cw.shape), _full(cb.shape), _full(dtb.shape)],
        out_specs=[pl.BlockSpec((tm, c), lambda i: (i, 0)),
                   pl.BlockSpec((tm, dt.shape[1]), lambda i: (i, 0))],
        out_shape=[jax.ShapeDtypeStruct((t, c), F32), jax.ShapeDtypeStruct(dt.shape, F32)],
        scratch_shapes=[pltpu.VMEM((tm + 2 * V7X_SUBLANES, c), F32)],
        compiler_params=_cparams(("parallel",)),
        name="ssd_prep",
    )(xbc, xbc, xbc, dt, cw, cb, dtb)


def _ssd_chunk(reverse, xa, dl, alog, st_ref):
    ln = SSM_CHUNK
    row = lax.broadcasted_iota(jnp.int32, (ln, ln), 0)
    col = lax.broadcasted_iota(jnp.int32, (ln, ln), 1)
    tri = (col >= row) if reverse else (col <= row)
    eye = row == col
    last = 0 if reverse else ln - 1
    lane0 = SSM_HEADS if reverse else 0
    a_neg = -jnp.exp(alog)
    acs = jnp.dot(tri.astype(F32), dl * a_neg, precision=HIGHEST, preferred_element_type=F32)
    dx = SSM_HEADS * SSM_HEAD_DIM
    gw = SSM_STATE
    hpg = SSM_HEADS // SSM_GROUPS
    ys = []
    for g in range(SSM_GROUPS):
        bg = xa[:, dx + g * gw:dx + (g + 1) * gw].astype(BF16)
        cg = xa[:, dx + SSM_GROUPS * gw + g * gw:dx + SSM_GROUPS * gw + (g + 1) * gw].astype(BF16)
        cb = _dot_nt(cg, bg)
        for r in range(hpg):
            h = g * hpg + r
            ln_h = lane0 + h
            ac = acs[:, ln_h:ln_h + 1]
            ar = jnp.sum(jnp.where(eye, ac, 0.0), axis=0, keepdims=True)
            dec = jnp.exp(jnp.where(tri, ac - ar, NEG_BIG))
            xdt = xa[:, h * SSM_HEAD_DIM:(h + 1) * SSM_HEAD_DIM] * dl[:, ln_h:ln_h + 1]
            st = st_ref[h]
            y = _dot((cb * dec).astype(BF16), xdt.astype(BF16))
            y = y + _dot(cg, st.astype(BF16)) * jnp.exp(ac)
            a_last = acs[last:last + 1, ln_h:ln_h + 1]
            st_ref[h] = st * jnp.exp(a_last) + _dot_tn(bg, (xdt * jnp.exp(a_last - ac)).astype(BF16))
            ys.append(y)
    return jnp.concatenate(ys, axis=1)


def _ssd_fwd_kernel(cps, xa_ref, dl_ref, alog_ref, y_ref, st_ref):
    @pl.when(pl.program_id(1) == 0)
    def _():
        st_ref[...] = jnp.zeros(st_ref.shape, F32)

    for c in range(cps):
        sl = slice(c * SSM_CHUNK, (c + 1) * SSM_CHUNK)
        y_ref[sl, :] = _ssd_chunk(False, xa_ref[sl, :], dl_ref[sl, :], alog_ref[...], st_ref)


def _ssd_bwd_kernel(cps, xa_ref, dl_ref, alog_ref, yf_ref, z_ref, dsk_ref, g_ref, o_ref, st_ref):
    @pl.when(pl.program_id(1) == 0)
    def _():
        st_ref[...] = jnp.zeros(st_ref.shape, F32)

    dx = SSM_HEADS * SSM_HEAD_DIM
    for c in reversed(range(cps)):
        sl = slice(c * SSM_CHUNK, (c + 1) * SSM_CHUNK)
        xa = xa_ref[sl, :]
        yb = _ssd_chunk(True, xa, dl_ref[sl, :], alog_ref[...], st_ref)
        y = yf_ref[sl, :] + yb + dsk_ref[...] * xa[:, :dx]
        y = y * jax.nn.silu(z_ref[sl, :])
        o_ref[sl, :] = _rms(y, g_ref[...]).astype(o_ref.dtype)


def _ssd(xa, dl, alog, z, dsk, gssm, b, s, cps):
    t, c = xa.shape
    tm = cps * SSM_CHUNK
    nblk = s // tm
    dx = SSM_HEADS * SSM_HEAD_DIM
    st = pltpu.VMEM((SSM_HEADS, SSM_STATE, SSM_HEAD_DIM), F32)
    fw = lambda w: pl.BlockSpec((tm, w), lambda bi, ci: (bi * nblk + ci, 0))
    bw = lambda w: pl.BlockSpec((tm, w), lambda bi, ci: (bi * nblk + nblk - 1 - ci, 0))
    yf = pl.pallas_call(
        functools.partial(_ssd_fwd_kernel, cps),
        grid=(b, nblk),
        in_specs=[fw(c), fw(dl.shape[1]), _full(alog.shape)],
        out_specs=fw(dx),
        out_shape=jax.ShapeDtypeStruct((t, dx), F32),
        scratch_shapes=[st],
        compiler_params=_cparams(("parallel", "arbitrary")),
        name="ssd_fwd",
    )(xa, dl, alog)
    return pl.pallas_call(
        functools.partial(_ssd_bwd_kernel, cps),
        grid=(b, nblk),
        in_specs=[bw(c), bw(dl.shape[1]), _full(alog.shape), bw(dx), bw(dx), _full(dsk.shape),
                  _full(gssm.shape)],
        out_specs=bw(dx),
        out_shape=jax.ShapeDtypeStruct((t, dx), BF16),
        scratch_shapes=[st],
        compiler_params=_cparams(("parallel", "arbitrary")),
        name="ssd_bwd",
    )(xa, dl, alog, yf, z, dsk, gssm)


def _sgu_kernel(nck, u_ref, v_ref, w_ref, bias_ref, o_ref):
    for c in range(nck):
        sl = slice(c * SGU_CHUNK, (c + 1) * SGU_CHUNK)
        v = v_ref[sl, :]
        sv = [_dot(w_ref[g], v[:, g * SGU_GROUP_DIM:(g + 1) * SGU_GROUP_DIM]) for g in range(SGU_GROUPS)]
        o_ref[sl, :] = (u_ref[sl, :] * (jnp.concatenate(sv, axis=1) + bias_ref[...])).astype(o_ref.dtype)


def _sgu(u, v, w, bias, nck):
    t, c = u.shape
    tm = nck * SGU_CHUNK
    return pl.pallas_call(
        functools.partial(_sgu_kernel, nck),
        grid=(t // tm,),
        in_specs=[pl.BlockSpec((tm, c), lambda i: (i, 0)), pl.BlockSpec((tm, c), lambda i: (i, 0)),
                  _full(w.shape), _full(bias.shape)],
        out_specs=pl.BlockSpec((tm, c), lambda i: (i, 0)),
        out_shape=jax.ShapeDtypeStruct((t, c), BF16),
        compiler_params=_cparams(("parallel",)),
        name="sgu",
    )(u, v, w, bias)


def _merge_kernel(x_ref, gmix_ref, oat_ref, ob_ref, oct_ref, od_ref, wg_ref, wbr_ref, wo_ref, o_ref):
    x = x_ref[...]
    dm = x.shape[1]
    hb = _rms(x, gmix_ref[...]).astype(BF16)
    merged = jnp.zeros(x.shape, F32)
    for n, (br, transposed) in enumerate(((oat_ref, True), (ob_ref, False), (oct_ref, True), (od_ref, False))):
        gate = jax.nn.sigmoid(_dot(hb, wg_ref[:, n * dm:(n + 1) * dm]))
        wide = _dot_tn(br[...], wbr_ref[n]) if transposed else _dot(br[...], wbr_ref[n])
        merged = merged + gate * wide
    o_ref[...] = x + _dot(merged.astype(BF16), wo_ref[...])


def _merge(x, gmix, oat, ob, oct, od, wg, wbr, wo, s, tm):
    t, dm = x.shape
    nst = s // tm
    row = lambda w: pl.BlockSpec((tm, w), lambda i: (i, 0))
    lanes = pl.BlockSpec((None, BRANCH_WIDTH, tm), lambda i: (i // nst, 0, i % nst))
    return pl.pallas_call(
        _merge_kernel,
        grid=(t // tm,),
        in_specs=[row(dm), _full(gmix.shape), lanes, row(BRANCH_WIDTH), lanes, row(BRANCH_WIDTH),
                  _full(wg.shape), _full(wbr.shape), _full(wo.shape)],
        out_specs=row(dm),
        out_shape=jax.ShapeDtypeStruct((t, dm), F32),
        compiler_params=_cparams(("parallel",)),
        name="merge",
    )(x, gmix, oat, ob, oct, od, wg, wbr, wo)


def _router_kernel(x_ref, g_ref, wrt_ref, h_ref, aff_ref):
    h = _rms(x_ref[...], g_ref[...])
    h_ref[...] = h.astype(BF16)
    logits = lax.dot_general(wrt_ref[...], h, (((1,), (1,)), ((), ())), precision=HIGHEST,
                             preferred_element_type=F32)
    e = jnp.exp(logits - jnp.max(logits, axis=0, keepdims=True))
    aff_ref[...] = e / jnp.sum(e, axis=0, keepdims=True)


def _router(x, g, wrt, b, s, tm):
    t, dm = x.shape
    ne = wrt.shape[0]
    nst = s // tm
    return pl.pallas_call(
        _router_kernel,
        grid=(t // tm,),
        in_specs=[pl.BlockSpec((tm, dm), lambda i: (i, 0)), _full(g.shape), _full(wrt.shape)],
        out_specs=[pl.BlockSpec((tm, dm), lambda i: (i, 0)),
                   pl.BlockSpec((None, ne, tm), lambda i: (i // nst, 0, i % nst))],
        out_shape=[jax.ShapeDtypeStruct((t, dm), BF16), jax.ShapeDtypeStruct((b, ne, s), F32)],
        compiler_params=_cparams(("parallel",)),
        name="router",
    )(x, g, wrt)


def _select_kernel(cap, aff_ref, gate_ref, pos_ref):
    ne, s = aff_ref.shape
    bits = pltpu.bitcast(aff_ref[...], jnp.int32)
    capf = jnp.float32(cap)

    def bisect(i, thr):
        cand = thr | lax.shift_left(jnp.int32(1), 30 - i)
        cnt = jnp.sum((bits >= cand).astype(F32), axis=1, keepdims=True)
        return jnp.where(cnt >= capf, cand, thr)

    thr = lax.fori_loop(0, 31, bisect, jnp.zeros((ne, 1), jnp.int32))
    need = capf - jnp.sum((bits > thr).astype(F32), axis=1, keepdims=True)
    lanes = V7X_LANES
    r = lax.broadcasted_iota(jnp.int32, (lanes, lanes), 0)
    c = lax.broadcasted_iota(jnp.int32, (lanes, lanes), 1)
    before = (r < c).astype(BF16)

    def block(jb, carry):
        n_eq, n_keep = carry
        off = pl.multiple_of(jb * lanes, lanes)
        a = aff_ref[:, pl.ds(off, lanes)]
        bb = pltpu.bitcast(a, jnp.int32)
        eq = bb == thr
        eqf = eq.astype(F32)
        keep = jnp.logical_or(bb > thr, jnp.logical_and(eq, n_eq + _dot(eqf.astype(BF16), before) < need))
        keepf = keep.astype(F32)
        gate_ref[:, pl.ds(off, lanes)] = jnp.where(keep, a, 0.0)
        pos_ref[:, pl.ds(off, lanes)] = jnp.where(keep, n_keep + _dot(keepf.astype(BF16), before), -1.0)
        return n_eq + jnp.sum(eqf, axis=1, keepdims=True), n_keep + jnp.sum(keepf, axis=1, keepdims=True)

    zero = jnp.zeros((ne, 1), F32)
    lax.fori_loop(0, s // lanes, block, (zero, zero))


def _select(aff, cap):
    b, ne, s = aff.shape
    spec = pl.BlockSpec((None, ne, s), lambda i: (i, 0, 0))
    return pl.pallas_call(
        functools.partial(_select_kernel, cap),
        grid=(b,),
        in_specs=[spec],
        out_specs=[spec, spec],
        out_shape=[jax.ShapeDtypeStruct(aff.shape, F32), jax.ShapeDtypeStruct(aff.shape, F32)],
        compiler_params=_cparams(("parallel",)),
        name="moe_select",
    )(aff)


def _moe_kernel(rb, cnt_ref, base_ref, h_ref, pos_ref, gate_ref, wg_ref, wu_ref, wd_ref, o_ref):
    i = pl.program_id(0)
    e = pl.program_id(1)

    @pl.when(e == 0)
    def _():
        o_ref[...] = jnp.zeros(o_ref.shape, F32)

    pos = pos_ref[...] - base_ref[i, e].astype(F32)
    gate = gate_ref[...]
    h = h_ref[...]

    def block(bi, carry):
        slot = lax.broadcasted_iota(jnp.int32, (rb, 1), 0).astype(F32) + (bi * rb).astype(F32)
        hit = pos == slot
        onehot = hit.astype(BF16)
        xc = _dot(onehot, h).astype(BF16)
        hid = jax.nn.silu(_dot(xc, wg_ref[...])) * _dot(xc, wu_ref[...])
        y = _dot(hid.astype(BF16), wd_ref[...])
        g = jnp.sum(jnp.where(hit, gate, 0.0), axis=1, keepdims=True)
        o_ref[...] += _dot_tn(onehot, (y * g).astype(BF16))
        return carry

    lax.fori_loop(0, (cnt_ref[i, e] + rb - 1) // rb, block, 0)


def _moe(h, pos, gates, cnt, base, wg, wu, wd, tt, rb):
    t, dm = h.shape
    ne, _, dff = wg.shape
    nst = pos.shape[2] // tt
    lane_row = pl.BlockSpec((None, 1, tt), lambda i, e, *_: ((i // nst) * ne + e, 0, i % nst))
    return pl.pallas_call(
        functools.partial(_moe_kernel, rb),
        grid_spec=pltpu.PrefetchScalarGridSpec(
            num_scalar_prefetch=2,
            grid=(t // tt, ne),
            in_specs=[pl.BlockSpec((tt, dm), lambda i, e, *_: (i, 0)), lane_row, lane_row,
                      pl.BlockSpec((None, dm, dff), lambda i, e, *_: (e, 0, 0)),
                      pl.BlockSpec((None, dm, dff), lambda i, e, *_: (e, 0, 0)),
                      pl.BlockSpec((None, dff, dm), lambda i, e, *_: (e, 0, 0))],
            out_specs=pl.BlockSpec((tt, dm), lambda i, e, *_: (i, 0))),
        out_shape=jax.ShapeDtypeStruct((t, dm), F32),
        compiler_params=_cparams(("parallel", "arbitrary")),
        name="moe_ffn",
    )(cnt, base, h, pos, gates, wg, wu, wd)


def _ple_kernel(final, x_ref, moe_ref, p_ref, g_ref, wpg_ref, wp_ref, gf_ref, o_ref):
    x = x_ref[...] + moe_ref[...]
    gate = jax.nn.sigmoid(_dot(_rms(x, g_ref[...]).astype(BF16), wpg_ref[...]))
    y = x + gate * _dot(p_ref[...].astype(BF16), wp_ref[...])
    o_ref[...] = _rms(y, gf_ref[...]) if final else y


def _ple(x, moe, p, g, wpg, wp, gf, final, tm):
    t, dm = x.shape
    return pl.pallas_call(
        functools.partial(_ple_kernel, final),
        grid=(t // tm,),
        in_specs=[pl.BlockSpec((tm, dm), lambda i: (i, 0)), pl.BlockSpec((tm, dm), lambda i: (i, 0)),
                  pl.BlockSpec((tm, p.shape[1]), lambda i: (i, 0)),
                  _full(g.shape), _full(wpg.shape), _full(wp.shape), _full(gf.shape)],
        out_specs=pl.BlockSpec((tm, dm), lambda i: (i, 0)),
        out_shape=jax.ShapeDtypeStruct((t, dm), F32),
        compiler_params=_cparams(("parallel",)),
        name="ple",
    )(x, moe, p, g, wpg, wp, gf)


def _rope_table(s):
    half = A_HEAD_DIM // 2
    pos = jnp.arange(s, dtype=jnp.int32)
    freqs = ROPE_THETA ** (-jnp.arange(0, half, 2, dtype=F32) / half)
    ang_r = freqs[:, None] * (pos // GRID_W).astype(F32)[None, :]
    ang_c = freqs[:, None] * (pos % GRID_W).astype(F32)[None, :]
    return jnp.concatenate([jnp.cos(ang_r), jnp.sin(ang_r), jnp.cos(ang_c), jnp.sin(ang_c)], axis=0)


def _t5_bucket(rel):
    nb = REL_BUCKETS // 2
    max_exact = nb // 2
    ret = jnp.where(rel > 0, nb, 0)
    r = jnp.abs(rel)
    rf = jnp.maximum(r, 1).astype(F32)
    large = max_exact + (jnp.log(rf / max_exact) / math.log(REL_MAX_DIST / max_exact)
                         * (nb - max_exact)).astype(jnp.int32)
    large = jnp.minimum(large, nb - 1)
    return ret + jnp.where(r < max_exact, r, large)


def _bias_tables(rel_bias, tq, tk):
    assert tk >= REL_MAX_DIST and tq % tk == 0
    rep = tq // tk
    offs = jnp.arange(-1, rep + 1, dtype=jnp.int32) * tk
    u = jnp.arange(tq + tk, dtype=jnp.int32)
    rel = offs[:, None] + (tk - 1) - u[None, :]
    gband = jnp.transpose(rel_bias[_t5_bucket(rel)], (2, 0, 1))[:, :, None, :].astype(F32) * LOG2E
    far = jnp.array([-REL_MAX_DIST, REL_MAX_DIST], dtype=jnp.int32)
    cfar = jnp.transpose(rel_bias[_t5_bucket(far)], (1, 0))[:, None, :].astype(F32) * LOG2E
    return gband, cfar


def _tile(n, pref):
    t = min(n, pref)
    assert n % t == 0
    return t


def kernel(x, p, rel_bias, g_mix, w_in, g_qnorm, g_knorm, conv_w, conv_b, dt_bias_f, dt_bias_b, a_log_f, a_log_b, d_skip, g_ssm, lambda_q1, lambda_k1, lambda_q2, lambda_k2, g_diff, g_sgu, w_spatial, b_spatial, w_branch, w_branch_gate, w_out, g_moe, w_router, w_exp_gate, w_exp_up, w_exp_down, g_ple, w_ple_gate, w_ple, g_final):
    b, s, dm = x.shape
    depth = w_in.shape[0]
    t = b * s
    cap = EC_CAPACITY * s // N_EXPERTS

    tm_proj = _tile(s, 512)
    tm_prep = _tile(s, 1024)
    tq_a, tk_a = _tile(s, 512), _tile(s, 1024)
    tq_c, tk_c = _tile(s, 512), _tile(s, 512)
    cps = _tile(s // SSM_CHUNK, 4)
    nck = _tile(s // SGU_CHUNK, 8)
    tt_moe = _tile(s, 1024)

    rope_tab = _rope_table(s)
    gband, cfar = _bias_tables(rel_bias, tq_c, tk_c)
    hd = A_HEAD_DIM

    sizes = (A_Q_HEADS * hd, A_KV_HEADS * hd, A_KV_HEADS * hd,
             SSM_HEADS * SSM_HEAD_DIM, SSM_HEADS * SSM_HEAD_DIM, SSM_GROUPS * SSM_STATE,
             SSM_GROUPS * SSM_STATE, SSM_HEADS, SSM_HEADS,
             DIFF_HEADS * 2 * DIFF_QK_DIM, DIFF_HEADS * 2 * DIFF_QK_DIM, DIFF_HEADS * DIFF_V_DIM,
             2 * SGU_GROUPS * SGU_GROUP_DIM)
    off = [0]
    for sz in sizes:
        off.append(off[-1] + sz)
    o_a0, o_b0, o_dt0, o_c0, o_d0, o_end = off[0], off[3], off[7], off[9], off[12], off[13]
    dt_pad = V7X_LANES - 2 * SSM_HEADS

    def lane_row(v, width=None):
        v = v.astype(F32).reshape(1, -1)
        if width is not None and v.shape[1] < width:
            v = jnp.pad(v, ((0, 0), (0, width - v.shape[1])))
        return v

    xt = x.reshape(t, dm)
    for i in range(depth):
        wi = w_in[i]
        wat = wi[:, o_a0:o_b0].T.astype(BF16)
        wb = jnp.pad(wi[:, o_b0:o_c0], ((0, 0), (0, dt_pad))).astype(BF16)
        wct = wi[:, o_c0:o_d0].T.astype(BF16)
        wd = wi[:, o_d0:o_end].astype(BF16)
        gmix = lane_row(g_mix[i])
        col = lambda v: v.astype(F32).reshape(-1, 1)
        (qt, kh, vt, z, xbc, dt, cqt, ckh, cvt, du, dv) = _inproj(
            xt, gmix, wat, wb, wct, wd, rope_tab, col(g_qnorm[i]), col(g_knorm[i]), lane_row(g_sgu[i]),
            b, s, tm_proj)

        o_at = _attn_a(qt.reshape(b, A_Q_HEADS, hd, s), kh, vt, tq_a, tk_a, ATTN_INNER).reshape(b, -1, s)

        xa, dl = _ssd_prep(xbc, dt, conv_w[i].astype(F32), lane_row(conv_b[i]),
                           lane_row(jnp.concatenate([dt_bias_f[i], dt_bias_b[i]]), V7X_LANES), s, tm_prep)
        o_b = _ssd(xa, dl, lane_row(jnp.concatenate([a_log_f[i], a_log_b[i]]), V7X_LANES), z,
                   lane_row(jnp.repeat(d_skip[i], SSM_HEAD_DIM)), lane_row(g_ssm[i]), b, s, cps)

        lam_init = 0.8 - 0.6 * math.exp(-0.3 * i)
        lam = (jnp.exp(jnp.sum(lambda_q1[i].astype(F32) * lambda_k1[i].astype(F32)))
               - jnp.exp(jnp.sum(lambda_q2[i].astype(F32) * lambda_k2[i].astype(F32))) + lam_init).reshape(1, 1)
        o_ct = _attn_c(cqt.reshape(b, DIFF_HEADS, 2, DIFF_QK_DIM, s), ckh, cvt, gband, cfar, lam, col(g_diff[i]),
                       1.0 - lam_init, tq_c, tk_c, ATTN_INNER).reshape(b, -1, s)

        sgu_bias = jnp.repeat(b_spatial[i].T.astype(F32), SGU_GROUP_DIM, axis=1)
        o_d = _sgu(du, dv, w_spatial[i].astype(BF16), sgu_bias, nck)

        xt = _merge(xt, gmix, o_at, o_b, o_ct, o_d, w_branch_gate[i].astype(BF16),
                    w_branch[i].astype(BF16), w_out[i].astype(BF16), s, tm_proj)

        h2, aff = _router(xt, lane_row(g_moe[i]), w_router[i].T.astype(F32), b, s, tm_proj)
        gates, pos = _select(aff, cap)
        cnt = jnp.sum((pos >= 0).reshape(b, N_EXPERTS, s // tt_moe, tt_moe), axis=3, dtype=jnp.int32)
        tile_major = lambda a: jnp.transpose(a, (0, 2, 1)).reshape(t // tt_moe, N_EXPERTS)
        moe = _moe(h2, pos.reshape(b * N_EXPERTS, 1, s), gates.reshape(b * N_EXPERTS, 1, s), tile_major(cnt),
                   tile_major(jnp.cumsum(cnt, axis=2) - cnt), w_exp_gate[i].astype(BF16), w_exp_up[i].astype(BF16), w_exp_down[i].astype(BF16),
                   tt_moe, MOE_ROW_BLOCK)

        xt = _ple(xt, moe, p[i].reshape(t, -1), lane_row(g_ple[i]), w_ple_gate[i].astype(BF16),
                  w_ple[i].astype(BF16), lane_row(g_final), i == depth - 1, tm_proj)

    return xt.reshape(b, s, dm)
```

```python
import functools
import math

import jax
import jax.numpy as jnp
from jax import lax
from jax.experimental import pallas as pl
from jax.experimental.pallas import tpu as pltpu

F32 = jnp.float32
BF16 = jnp.bfloat16
HIGHEST = lax.Precision.HIGHEST

EPS = 1e-6
GRID_W = 64
A_Q_HEADS, A_KV_HEADS, A_HEAD_DIM = 4, 2, 64
ROPE_THETA = 10000.0
SSM_HEADS, SSM_HEAD_DIM, SSM_GROUPS, SSM_STATE, SSM_CONV, SSM_CHUNK = 4, 64, 2, 128, 5, 128
DIFF_HEADS, DIFF_QK_DIM, DIFF_V_DIM = 4, 32, 64
REL_BUCKETS, REL_MAX_DIST = 32, 128
SGU_GROUPS, SGU_GROUP_DIM, SGU_CHUNK = 4, 64, 128
N_EXPERTS, EC_CAPACITY = 16, 2
N_BRANCHES, BRANCH_WIDTH = 4, 256

V7X_LANES = 128
V7X_SUBLANES = 8
BF16_SUBLANE_PACK = 16
V7X_VMEM_LIMIT_BYTES = 56 * 1024 * 1024

NEG_BIG = -1e30
LOG2E = math.log2(math.e)
MOE_ROW_BLOCK = 160
ATTN_INNER = 4
ATTN_C_INNER = 7


def _cparams(sem):
    return pltpu.CompilerParams(dimension_semantics=sem, vmem_limit_bytes=V7X_VMEM_LIMIT_BYTES)


def _rms(x, g):
    return x * lax.rsqrt(jnp.mean(x * x, axis=-1, keepdims=True) + EPS) * g


def _dot(a, b):
    return jnp.dot(a, b, preferred_element_type=F32)


def _dot_nt(a, b):
    return lax.dot_general(a, b, (((1,), (1,)), ((), ())), preferred_element_type=F32)


def _dot_tn(a, b):
    return lax.dot_general(a, b, (((0,), (0,)), ((), ())), preferred_element_type=F32)


def _full(shape):
    n = len(shape)
    return pl.BlockSpec(shape, lambda *_: (0,) * n)


def _norm_rope_t(x, g_col, tab):
    xn = x * lax.rsqrt(jnp.mean(x * x, axis=0, keepdims=True) + EPS) * g_col
    qd = A_HEAD_DIM // 4
    cr, sr, cc, sc = (tab[i * qd:(i + 1) * qd] for i in range(4))
    r1, r2, c1, c2 = (xn[i * qd:(i + 1) * qd] for i in range(4))
    return jnp.concatenate([r1 * cr - r2 * sr, r1 * sr + r2 * cr, c1 * cc - c2 * sc, c1 * sc + c2 * cc], axis=0)


def _inproj_kernel(x_ref, gmix_ref, wat_ref, wb_ref, wct_ref, wd_ref, tab_ref, gq_ref, gk_ref, gsgu_ref,
                   qa_ref, ka_ref, va_ref, z_ref, xbc_ref, dt_ref, cq_ref, ck_ref, cv_ref,
                   du_ref, dv_ref):
    hb = _rms(x_ref[...], gmix_ref[...]).astype(BF16)
    tm = hb.shape[0]
    hd = A_HEAD_DIM
    ones = jnp.ones((BF16_SUBLANE_PACK, tm), BF16)

    at = _dot_nt(wat_ref[...], hb)
    tab = tab_ref[...]
    q = [_norm_rope_t(at[h * hd:(h + 1) * hd], gq_ref[...], tab) for h in range(A_Q_HEADS)]
    qa_ref[...] = (jnp.concatenate(q, axis=0) * (hd ** -0.5 * LOG2E)).astype(BF16)
    k0 = A_Q_HEADS * hd
    v0 = k0 + A_KV_HEADS * hd
    for g in range(A_KV_HEADS):
        kt = _norm_rope_t(at[k0 + g * hd:k0 + (g + 1) * hd], gk_ref[...], tab)
        ka_ref[g] = jnp.transpose(kt).astype(BF16)
        va_ref[g, :hd] = at[v0 + g * hd:v0 + (g + 1) * hd].astype(BF16)
        va_ref[g, hd:] = ones

    b = _dot(hb, wb_ref[...])
    dx = SSM_HEADS * SSM_HEAD_DIM
    conv_ch = dx + 2 * SSM_GROUPS * SSM_STATE
    z_ref[...] = b[:, :dx]
    xbc_ref[...] = b[:, dx:dx + conv_ch]
    dt_ref[...] = b[:, dx + conv_ch:]

    ct = _dot_nt(wct_ref[...], hb)
    nqc = DIFF_HEADS * 2 * DIFF_QK_DIM
    cq_ref[...] = (ct[:nqc] * (DIFF_QK_DIM ** -0.5 * LOG2E)).astype(BF16)
    ck = jnp.transpose(ct[nqc:2 * nqc]).astype(BF16)
    for h in range(DIFF_HEADS):
        for c in range(2):
            lo = (2 * h + c) * DIFF_QK_DIM
            ck_ref[h, c] = ck[:, lo:lo + DIFF_QK_DIM]
        cv_ref[h, :DIFF_V_DIM] = ct[2 * nqc + h * DIFF_V_DIM:2 * nqc + (h + 1) * DIFF_V_DIM].astype(BF16)
        cv_ref[h, DIFF_V_DIM:] = ones

    d = jax.nn.gelu(_dot(hb, wd_ref[...]))
    w = SGU_GROUPS * SGU_GROUP_DIM
    du_ref[...] = d[:, :w]
    dv_ref[...] = _rms(d[:, w:], gsgu_ref[...]).astype(BF16)


def _inproj(x, gmix, wat, wb, wct, wd, tab, gq, gk, gsgu, b, s, tm):
    t, dm = x.shape
    nst = s // tm
    row = lambda w: pl.BlockSpec((tm, w), lambda i: (i, 0))
    lanes = lambda *lead: pl.BlockSpec((None,) + lead + (tm,), lambda i: (i // nst,) + (0,) * len(lead) + (i % nst,))
    hd, pk = A_HEAD_DIM, BF16_SUBLANE_PACK
    specs_shapes = [
        (lanes(A_Q_HEADS * hd), (b, A_Q_HEADS * hd, s), BF16),
        (pl.BlockSpec((None, A_KV_HEADS, tm, hd), lambda i: (i // nst, 0, i % nst, 0)), (b, A_KV_HEADS, s, hd), BF16),
        (lanes(A_KV_HEADS, hd + pk), (b, A_KV_HEADS, hd + pk, s), BF16),
        (row(256), (t, 256), F32), (row(768), (t, 768), F32), (row(128), (t, 128), F32),
        (lanes(DIFF_HEADS * 2 * DIFF_QK_DIM), (b, DIFF_HEADS * 2 * DIFF_QK_DIM, s), BF16),
        (pl.BlockSpec((None, DIFF_HEADS, 2, tm, DIFF_QK_DIM), lambda i: (i // nst, 0, 0, i % nst, 0)),
         (b, DIFF_HEADS, 2, s, DIFF_QK_DIM), BF16),
        (lanes(DIFF_HEADS, DIFF_V_DIM + pk), (b, DIFF_HEADS, DIFF_V_DIM + pk, s), BF16),
        (row(256), (t, 256), F32), (row(256), (t, 256), BF16)]
    return pl.pallas_call(
        _inproj_kernel,
        grid=(t // tm,),
        in_specs=[row(dm), _full(gmix.shape), _full(wat.shape), _full(wb.shape), _full(wct.shape),
                  _full(wd.shape), pl.BlockSpec((tab.shape[0], tm), lambda i: (0, i % nst)),
                  _full(gq.shape), _full(gk.shape), _full(gsgu.shape)],
        out_specs=[sp for sp, _, _ in specs_shapes],
        out_shape=[jax.ShapeDtypeStruct(shp, dt) for _, shp, dt in specs_shapes],
        compiler_params=_cparams(("parallel",)),
        name="inproj",
    )(x, gmix, wat, wb, wct, wd, tab, gq, gk, gsgu)


def _osm_tile(s, mt, shift, vt, m_ref, acc_ref, c):
    m_old = m_ref[c]
    if shift is not None:
        mt = mt + shift
    m_new = jnp.maximum(m_old, mt)
    p = jnp.exp2(s - (m_new if shift is None else m_new - shift))
    acc_ref[c] = jnp.exp2(m_old - m_new) * acc_ref[c] + _dot(vt, p.astype(BF16))
    m_ref[c] = m_new


def _osm_init(m_ref, acc_ref):
    m_ref[...] = jnp.full(m_ref.shape, NEG_BIG, F32)
    acc_ref[...] = jnp.zeros(acc_ref.shape, F32)


def _osm_result(acc_ref, c, dv):
    acc = acc_ref[c]
    return acc[:dv] / acc[dv:dv + 1]


def _attn_scores(tk, j, qt_ref, k_of, buf, bias_tile=None):
    s_ref, mx_ref = buf
    off = pl.multiple_of(j * tk, tk)
    for c in range(qt_ref.shape[0]):
        s = _dot(k_of(c, off), qt_ref[c])
        if bias_tile is not None:
            s = s + bias_tile
        s_ref[c] = s
        mx_ref[c] = jnp.max(s, axis=0, keepdims=True)


def _attn_values(tk, j, nmaps, vt_ref, buf, m_ref, acc_ref, shift=None):
    s_ref, mx_ref = buf
    vt = vt_ref[:, pl.ds(pl.multiple_of(j * tk, tk), tk)]
    for c in range(nmaps):
        _osm_tile(s_ref[c], mx_ref[c], shift, vt, m_ref, acc_ref, c)


def _pipelined(n, scores, values, s_refs, inner):
    last = jnp.maximum(n - 1, 0)
    span = 2 * inner
    scores(jnp.minimum(0, last), s_refs[0])

    def double(g):
        scores(jnp.minimum(g + 1, last), s_refs[1])
        values(g, s_refs[0])
        scores(jnp.minimum(g + 2, last), s_refs[0])
        values(g + 1, s_refs[1])

    def unrolled(t, carry):
        for w in range(inner):
            double(span * t + 2 * w)
        return carry

    lax.fori_loop(0, n // span, unrolled, 0)
    base = (n // span) * span

    def rolled(t, carry):
        double(base + 2 * t)
        return carry

    lax.fori_loop(0, (n - base) // 2, rolled, 0)

    @pl.when(n % 2 == 1)
    def _():
        values(n - 1, s_refs[0])


def _attn_a_kernel(tk, inner, qt_ref, k_ref, vt_ref, o_ref, m_ref, acc_ref, sa_ref, sb_ref, mxa_ref, mxb_ref):
    _osm_init(m_ref, acc_ref)
    nq = qt_ref.shape[0]
    k_of = lambda c, off: k_ref[pl.ds(off, tk), :]
    _pipelined(k_ref.shape[0] // tk,
               lambda j, buf: _attn_scores(tk, j, qt_ref, k_of, buf),
               lambda j, buf: _attn_values(tk, j, nq, vt_ref, buf, m_ref, acc_ref),
               ((sa_ref, mxa_ref), (sb_ref, mxb_ref)), inner)
    for c in range(nq):
        o_ref[c] = _osm_result(acc_ref, c, o_ref.shape[1]).astype(o_ref.dtype)


def _attn_a(qt, k, vt, tq, tk, inner):
    b, hq, d, s = qt.shape
    hkv = k.shape[1]
    rep = hq // hkv
    da = vt.shape[2]
    qspec = pl.BlockSpec((None, rep, d, tq), lambda bi, g, i: (bi, g, 0, i))
    return pl.pallas_call(
        functools.partial(_attn_a_kernel, tk, inner),
        grid=(b, hkv, s // tq),
        in_specs=[qspec,
                  pl.BlockSpec((None, None, s, d), lambda bi, g, i: (bi, g, 0, 0)),
                  pl.BlockSpec((None, None, da, s), lambda bi, g, i: (bi, g, 0, 0))],
        out_specs=qspec,
        out_shape=jax.ShapeDtypeStruct(qt.shape, BF16),
        scratch_shapes=[pltpu.VMEM((rep, 1, tq), F32), pltpu.VMEM((rep, da, tq), F32),
                        pltpu.VMEM((rep, tk, tq), F32), pltpu.VMEM((rep, tk, tq), F32),
                        pltpu.VMEM((rep, 1, tq), F32), pltpu.VMEM((rep, 1, tq), F32)],
        compiler_params=_cparams(("parallel", "parallel", "arbitrary")),
        name="attn_a",
    )(qt, k, vt)


def _attn_c_kernel(tk, inner, out_scale, qt_ref, k_ref, vt_ref, gband_ref, cfar_ref, lam_ref, gd_ref, o_ref,
                   m_ref, acc_ref, sa_ref, sb_ref, mxa_ref, mxb_ref):
    i = pl.program_id(2)
    tq = qt_ref.shape[2]
    nk = k_ref.shape[1] // tk
    nband = gband_ref.shape[0]
    dv = o_ref.shape[0]
    _osm_init(m_ref, acc_ref)
    k_of = lambda c, off: k_ref[c, pl.ds(off, tk), :]

    j_lo = (tq // tk) * i - 1
    n_left = jnp.clip(j_lo, 0, nk)
    j_hi = jnp.clip(j_lo + nband, 0, nk)
    n_far = n_left + nk - j_hi
    cf = cfar_ref[...]

    def far_tile(t):
        j = jnp.where(t < n_left, t, t - n_left + j_hi)
        return jnp.clip(j, 0, nk - 1), jnp.where(t < n_left, cf[:, 0:1], cf[:, 1:2])

    def far_scores(t, buf):
        _attn_scores(tk, far_tile(t)[0], qt_ref, k_of, buf)

    def far_values(t, buf):
        j, sh = far_tile(t)
        _attn_values(tk, j, 2, vt_ref, buf, m_ref, acc_ref, shift=sh)

    _pipelined(n_far, far_scores, far_values, ((sa_ref, mxa_ref), (sb_ref, mxb_ref)), inner)

    d_lo = n_left - j_lo

    def band_scores(t, buf):
        d = jnp.clip(d_lo + t, 0, nband - 1)
        g = gband_ref[d]
        tile = pltpu.roll(jnp.broadcast_to(g, (tk, g.shape[1])), tq + 1, 1, stride=1, stride_axis=0)
        _attn_scores(tk, jnp.clip(j_lo + d, 0, nk - 1), qt_ref, k_of, buf, bias_tile=tile[:, :tq])

    def band_values(t, buf):
        _attn_values(tk, j_lo + d_lo + t, 2, vt_ref, buf, m_ref, acc_ref)

    _pipelined(j_hi - n_left, band_scores, band_values, ((sa_ref, mxa_ref), (sb_ref, mxb_ref)), 1)

    o = _osm_result(acc_ref, 0, dv) - lam_ref[...] * _osm_result(acc_ref, 1, dv)
    o = o * lax.rsqrt(jnp.mean(o * o, axis=0, keepdims=True) + EPS) * gd_ref[...]
    o_ref[...] = (o * out_scale).astype(o_ref.dtype)


def _attn_c(qt, k, vt, gband, cfar, lam, gd, out_scale, tq, tk, inner):
    b, h, _, dqk, s = qt.shape
    dva = vt.shape[2]
    dv = gd.shape[0]
    nband = gband.shape[1]
    return pl.pallas_call(
        functools.partial(_attn_c_kernel, tk, inner, out_scale),
        grid=(b, h, s // tq),
        in_specs=[pl.BlockSpec((None, None, 2, dqk, tq), lambda bi, hi, i: (bi, hi, 0, 0, i)),
                  pl.BlockSpec((None, None, 2, s, dqk), lambda bi, hi, i: (bi, hi, 0, 0, 0)),
                  pl.BlockSpec((None, None, dva, s), lambda bi, hi, i: (bi, hi, 0, 0)),
                  pl.BlockSpec((None, nband, 1, tq + tk), lambda bi, hi, i: (hi, 0, 0, 0)),
                  pl.BlockSpec((None, 1, 2), lambda bi, hi, i: (hi, 0, 0)),
                  _full(lam.shape), _full(gd.shape)],
        out_specs=pl.BlockSpec((None, None, dv, tq), lambda bi, hi, i: (bi, hi, 0, i)),
        out_shape=jax.ShapeDtypeStruct((b, h, dv, s), BF16),
        scratch_shapes=[pltpu.VMEM((2, 1, tq), F32), pltpu.VMEM((2, dva, tq), F32),
                        pltpu.VMEM((2, tk, tq), F32), pltpu.VMEM((2, tk, tq), F32),
                        pltpu.VMEM((2, 1, tq), F32), pltpu.VMEM((2, 1, tq), F32)],
        compiler_params=_cparams(("parallel", "parallel", "arbitrary")),
        name="attn_c",
    )(qt, k, vt, gband, cfar, lam, gd)


def _ssd_prep_kernel(nst, xc_ref, xp_ref, xn_ref, dt_ref, cw_ref, cb_ref, dtb_ref, xa_ref, dl_ref, ext_ref):
    i = pl.program_id(0)
    tm = xc_ref.shape[0]
    halo = V7X_SUBLANES
    pos = i % nst
    keep_prev = (pos != 0).astype(F32)
    keep_next = (pos != nst - 1).astype(F32)
    ext_ref[0:halo, :] = xp_ref[...] * keep_prev
    ext_ref[halo:halo + tm, :] = xc_ref[...]
    ext_ref[halo + tm:, :] = xn_ref[...] * keep_next
    pad_l = (SSM_CONV - 1) // 2
    cw = cw_ref[...]
    acc = jnp.zeros(xc_ref.shape, F32) + cb_ref[...]
    for t in range(SSM_CONV):
        acc = acc + ext_ref[halo - pad_l + t:halo - pad_l + t + tm, :] * cw[t:t + 1, :]
    xa_ref[...] = jax.nn.silu(acc)
    dl_ref[...] = jax.nn.softplus(dt_ref[...] + dtb_ref[...])


def _ssd_prep(xbc, dt, cw, cb, dtb, s, tm):
    t, c = xbc.shape
    nst = s // tm
    hb = tm // V7X_SUBLANES
    nhb = t // V7X_SUBLANES
    return pl.pallas_call(
        functools.partial(_ssd_prep_kernel, nst),
        grid=(t // tm,),
        in_specs=[pl.BlockSpec((tm, c), lambda i: (i, 0)),
                  pl.BlockSpec((V7X_SUBLANES, c), lambda i: (jnp.maximum(i * hb - 1, 0), 0)),
                  pl.BlockSpec((V7X_SUBLANES, c), lambda i: (jnp.minimum((i + 1) * hb, nhb - 1), 0)),
                  pl.BlockSpec((tm, dt.shape[1]), lambda i: (i, 0)),
                  _full(cw.shape), _full(cb.shape), _full(dtb.shape)],
        out_specs=[pl.BlockSpec((tm, c), lambda i: (i, 0)),
                   pl.BlockSpec((tm, dt.shape[1]), lambda i: (i, 0))],
        out_shape=[jax.ShapeDtypeStruct((t, c), F32), jax.ShapeDtypeStruct(dt.shape, F32)],
        scratch_shapes=[pltpu.VMEM((tm + 2 * V7X_SUBLANES, c), F32)],
        compiler_params=_cparams(("parallel",)),
        name="ssd_prep",
    )(xbc, xbc, xbc, dt, cw, cb, dtb)


def _ssd_chunk(reverse, xa, dl, alog, st_ref):
    ln = SSM_CHUNK
    row = lax.broadcasted_iota(jnp.int32, (ln, ln), 0)
    col = lax.broadcasted_iota(jnp.int32, (ln, ln), 1)
    tri = (col >= row) if reverse else (col <= row)
    eye = row == col
    last = 0 if reverse else ln - 1
    lane0 = SSM_HEADS if reverse else 0
    a_neg = -jnp.exp(alog)
    acs = jnp.dot(tri.astype(F32), dl * a_neg, precision=HIGHEST, preferred_element_type=F32)
    dx = SSM_HEADS * SSM_HEAD_DIM
    gw = SSM_STATE
    hpg = SSM_HEADS // SSM_GROUPS
    ys = []
    for g in range(SSM_GROUPS):
        bg = xa[:, dx + g * gw:dx + (g + 1) * gw].astype(BF16)
        cg = xa[:, dx + SSM_GROUPS * gw + g * gw:dx + SSM_GROUPS * gw + (g + 1) * gw].astype(BF16)
        cb = _dot_nt(cg, bg)
        for r in range(hpg):
            h = g * hpg + r
            ln_h = lane0 + h
            ac = acs[:, ln_h:ln_h + 1]
            ar = jnp.sum(jnp.where(eye, ac, 0.0), axis=0, keepdims=True)
            dec = jnp.exp(jnp.where(tri, ac - ar, NEG_BIG))
            xdt = xa[:, h * SSM_HEAD_DIM:(h + 1) * SSM_HEAD_DIM] * dl[:, ln_h:ln_h + 1]
            st = st_ref[h]
            y = _dot((cb * dec).astype(BF16), xdt.astype(BF16))
            y = y + _dot(cg, st.astype(BF16)) * jnp.exp(ac)
            a_last = acs[last:last + 1, ln_h:ln_h + 1]
            st_ref[h] = st * jnp.exp(a_last) + _dot_tn(bg, (xdt * jnp.exp(a_last - ac)).astype(BF16))
            ys.append(y)
    return jnp.concatenate(ys, axis=1)


def _ssd_fwd_kernel(cps, xa_ref, dl_ref, alog_ref, y_ref, st_ref):
    @pl.when(pl.program_id(1) == 0)
    def _():
        st_ref[...] = jnp.zeros(st_ref.shape, F32)

    for c in range(cps):
        sl = slice(c * SSM_CHUNK, (c + 1) * SSM_CHUNK)
        y_ref[sl, :] = _ssd_chunk(False, xa_ref[sl, :], dl_ref[sl, :], alog_ref[...], st_ref)


def _ssd_bwd_kernel(cps, xa_ref, dl_ref, alog_ref, yf_ref, z_ref, dsk_ref, g_ref, o_ref, st_ref):
    @pl.when(pl.program_id(1) == 0)
    def _():
        st_ref[...] = jnp.zeros(st_ref.shape, F32)

    dx = SSM_HEADS * SSM_HEAD_DIM
    for c in reversed(range(cps)):
        sl = slice(c * SSM_CHUNK, (c + 1) * SSM_CHUNK)
        xa = xa_ref[sl, :]
        yb = _ssd_chunk(True, xa, dl_ref[sl, :], alog_ref[...], st_ref)
        y = yf_ref[sl, :] + yb + dsk_ref[...] * xa[:, :dx]
        y = y * jax.nn.silu(z_ref[sl, :])
        o_ref[sl, :] = _rms(y, g_ref[...]).astype(o_ref.dtype)


def _ssd(xa, dl, alog, z, dsk, gssm, b, s, cps):
    t, c = xa.shape
    tm = cps * SSM_CHUNK
    nblk = s // tm
    dx = SSM_HEADS * SSM_HEAD_DIM
    st = pltpu.VMEM((SSM_HEADS, SSM_STATE, SSM_HEAD_DIM), F32)
    fw = lambda w: pl.BlockSpec((tm, w), lambda bi, ci: (bi * nblk + ci, 0))
    bw = lambda w: pl.BlockSpec((tm, w), lambda bi, ci: (bi * nblk + nblk - 1 - ci, 0))
    yf = pl.pallas_call(
        functools.partial(_ssd_fwd_kernel, cps),
        grid=(b, nblk),
        in_specs=[fw(c), fw(dl.shape[1]), _full(alog.shape)],
        out_specs=fw(dx),
        out_shape=jax.ShapeDtypeStruct((t, dx), F32),
        scratch_shapes=[st],
        compiler_params=_cparams(("parallel", "arbitrary")),
        name="ssd_fwd",
    )(xa, dl, alog)
    return pl.pallas_call(
        functools.partial(_ssd_bwd_kernel, cps),
        grid=(b, nblk),
        in_specs=[bw(c), bw(dl.shape[1]), _full(alog.shape), bw(dx), bw(dx), _full(dsk.shape),
                  _full(gssm.shape)],
        out_specs=bw(dx),
        out_shape=jax.ShapeDtypeStruct((t, dx), BF16),
        scratch_shapes=[st],
        compiler_params=_cparams(("parallel", "arbitrary")),
        name="ssd_bwd",
    )(xa, dl, alog, yf, z, dsk, gssm)


def _sgu_kernel(nck, u_ref, v_ref, w_ref, bias_ref, o_ref):
    for c in range(nck):
        sl = slice(c * SGU_CHUNK, (c + 1) * SGU_CHUNK)
        v = v_ref[sl, :]
        sv = [_dot(w_ref[g], v[:, g * SGU_GROUP_DIM:(g + 1) * SGU_GROUP_DIM]) for g in range(SGU_GROUPS)]
        o_ref[sl, :] = (u_ref[sl, :] * (jnp.concatenate(sv, axis=1) + bias_ref[...])).astype(o_ref.dtype)


def _sgu(u, v, w, bias, nck):
    t, c = u.shape
    tm = nck * SGU_CHUNK
    return pl.pallas_call(
        functools.partial(_sgu_kernel, nck),
        grid=(t // tm,),
        in_specs=[pl.BlockSpec((tm, c), lambda i: (i, 0)), pl.BlockSpec((tm, c), lambda i: (i, 0)),
                  _full(w.shape), _full(bias.shape)],
        out_specs=pl.BlockSpec((tm, c), lambda i: (i, 0)),
        out_shape=jax.ShapeDtypeStruct((t, c), BF16),
        compiler_params=_cparams(("parallel",)),
        name="sgu",
    )(u, v, w, bias)


def _merge_kernel(x_ref, gmix_ref, oat_ref, ob_ref, oct_ref, od_ref, wg_ref, wbr_ref, wo_ref, o_ref):
    x = x_ref[...]
    dm = x.shape[1]
    hb = _rms(x, gmix_ref[...]).astype(BF16)
    merged = jnp.zeros(x.shape, F32)
    for n, (br, transposed) in enumerate(((oat_ref, True), (ob_ref, False), (oct_ref, True), (od_ref, False))):
        gate = jax.nn.sigmoid(_dot(hb, wg_ref[:, n * dm:(n + 1) * dm]))
        wide = _dot_tn(br[...], wbr_ref[n]) if transposed else _dot(br[...], wbr_ref[n])
        merged = merged + gate * wide
    o_ref[...] = x + _dot(merged.astype(BF16), wo_ref[...])


def _merge(x, gmix, oat, ob, oct, od, wg, wbr, wo, s, tm):
    t, dm = x.shape
    nst = s // tm
    row = lambda w: pl.BlockSpec((tm, w), lambda i: (i, 0))
    lanes = pl.BlockSpec((None, BRANCH_WIDTH, tm), lambda i: (i // nst, 0, i % nst))
    return pl.pallas_call(
        _merge_kernel,
        grid=(t // tm,),
        in_specs=[row(dm), _full(gmix.shape), lanes, row(BRANCH_WIDTH), lanes, row(BRANCH_WIDTH),
                  _full(wg.shape), _full(wbr.shape), _full(wo.shape)],
        out_specs=row(dm),
        out_shape=jax.ShapeDtypeStruct((t, dm), F32),
        compiler_params=_cparams(("parallel",)),
        name="merge",
    )(x, gmix, oat, ob, oct, od, wg, wbr, wo)


def _router_kernel(x_ref, g_ref, wrt_ref, h_ref, aff_ref):
    h = _rms(x_ref[...], g_ref[...])
    h_ref[...] = h.astype(BF16)
    logits = lax.dot_general(wrt_ref[...], h, (((1,), (1,)), ((), ())), precision=HIGHEST,
                             preferred_element_type=F32)
    e = jnp.exp(logits - jnp.max(logits, axis=0, keepdims=True))
    aff_ref[...] = e / jnp.sum(e, axis=0, keepdims=True)


def _router(x, g, wrt, b, s, tm):
    t, dm = x.shape
    ne = wrt.shape[0]
    nst = s // tm
    return pl.pallas_call(
        _router_kernel,
        grid=(t // tm,),
        in_specs=[pl.BlockSpec((tm, dm), lambda i: (i, 0)), _full(g.shape), _full(wrt.shape)],
        out_specs=[pl.BlockSpec((tm, dm), lambda i: (i, 0)),
                   pl.BlockSpec((None, ne, tm), lambda i: (i // nst, 0, i % nst))],
        out_shape=[jax.ShapeDtypeStruct((t, dm), BF16), jax.ShapeDtypeStruct((b, ne, s), F32)],
        compiler_params=_cparams(("parallel",)),
        name="router",
    )(x, g, wrt)


def _select_kernel(cap, aff_ref, gate_ref, pos_ref):
    ne, s = aff_ref.shape
    bits = pltpu.bitcast(aff_ref[...], jnp.int32)
    capf = jnp.float32(cap)

    def bisect(i, thr):
        cand = thr | lax.shift_left(jnp.int32(1), 30 - i)
        cnt = jnp.sum((bits >= cand).astype(F32), axis=1, keepdims=True)
        return jnp.where(cnt >= capf, cand, thr)

    thr = lax.fori_loop(0, 31, bisect, jnp.zeros((ne, 1), jnp.int32))
    need = capf - jnp.sum((bits > thr).astype(F32), axis=1, keepdims=True)
    lanes = V7X_LANES
    r = lax.broadcasted_iota(jnp.int32, (lanes, lanes), 0)
    c = lax.broadcasted_iota(jnp.int32, (lanes, lanes), 1)
    before = (r < c).astype(BF16)

    def block(jb, carry):
        n_eq, n_keep = carry
        off = pl.multiple_of(jb * lanes, lanes)
        a = aff_ref[:, pl.ds(off, lanes)]
        bb = pltpu.bitcast(a, jnp.int32)
        eq = bb == thr
        eqf = eq.astype(F32)
        keep = jnp.logical_or(bb > thr, jnp.logical_and(eq, n_eq + _dot(eqf.astype(BF16), before) < need))
        keepf = keep.astype(F32)
        gate_ref[:, pl.ds(off, lanes)] = jnp.where(keep, a, 0.0)
        pos_ref[:, pl.ds(off, lanes)] = jnp.where(keep, n_keep + _dot(keepf.astype(BF16), before), -1.0)
        return n_eq + jnp.sum(eqf, axis=1, keepdims=True), n_keep + jnp.sum(keepf, axis=1, keepdims=True)

    zero = jnp.zeros((ne, 1), F32)
    lax.fori_loop(0, s // lanes, block, (zero, zero))


def _select(aff, cap):
    b, ne, s = aff.shape
    spec = pl.BlockSpec((None, ne, s), lambda i: (i, 0, 0))
    return pl.pallas_call(
        functools.partial(_select_kernel, cap),
        grid=(b,),
        in_specs=[spec],
        out_specs=[spec, spec],
        out_shape=[jax.ShapeDtypeStruct(aff.shape, F32), jax.ShapeDtypeStruct(aff.shape, F32)],
        compiler_params=_cparams(("parallel",)),
        name="moe_select",
    )(aff)


def _moe_kernel(rb, cnt_ref, base_ref, h_ref, pos_ref, gate_ref, wg_ref, wu_ref, wd_ref, o_ref):
    i = pl.program_id(0)
    e = pl.program_id(1)

    @pl.when(e == 0)
    def _():
        o_ref[...] = jnp.zeros(o_ref.shape, F32)

    pos = pos_ref[...] - base_ref[i, e].astype(F32)
    gate = gate_ref[...]
    h = h_ref[...]

    def block(bi, carry):
        slot = lax.broadcasted_iota(jnp.int32, (rb, 1), 0).astype(F32) + (bi * rb).astype(F32)
        hit = pos == slot
        onehot = hit.astype(BF16)
        xc = _dot(onehot, h).astype(BF16)
        hid = jax.nn.silu(_dot(xc, wg_ref[...])) * _dot(xc, wu_ref[...])
        y = _dot(hid.astype(BF16), wd_ref[...])
        g = jnp.sum(jnp.where(hit, gate, 0.0), axis=1, keepdims=True)
        o_ref[...] += _dot_tn(onehot, (y * g).astype(BF16))
        return carry

    lax.fori_loop(0, (cnt_ref[i, e] + rb - 1) // rb, block, 0)


def _moe(h, pos, gates, cnt, base, wg, wu, wd, tt, rb):
    t, dm = h.shape
    ne, _, dff = wg.shape
    nst = pos.shape[2] // tt
    lane_row = pl.BlockSpec((None, 1, tt), lambda i, e, *_: ((i // nst) * ne + e, 0, i % nst))
    return pl.pallas_call(
        functools.partial(_moe_kernel, rb),
        grid_spec=pltpu.PrefetchScalarGridSpec(
            num_scalar_prefetch=2,
            grid=(t // tt, ne),
            in_specs=[pl.BlockSpec((tt, dm), lambda i, e, *_: (i, 0)), lane_row, lane_row,
                      pl.BlockSpec((None, dm, dff), lambda i, e, *_: (e, 0, 0)),
                      pl.BlockSpec((None, dm, dff), lambda i, e, *_: (e, 0, 0)),
                      pl.BlockSpec((None, dff, dm), lambda i, e, *_: (e, 0, 0))],
            out_specs=pl.BlockSpec((tt, dm), lambda i, e, *_: (i, 0))),
        out_shape=jax.ShapeDtypeStruct((t, dm), F32),
        compiler_params=_cparams(("parallel", "arbitrary")),
        name="moe_ffn",
    )(cnt, base, h, pos, gates, wg, wu, wd)


def _ple_kernel(final, x_ref, moe_ref, p_ref, g_ref, wpg_ref, wp_ref, gf_ref, o_ref):
    x = x_ref[...] + moe_ref[...]
    gate = jax.nn.sigmoid(_dot(_rms(x, g_ref[...]).astype(BF16), wpg_ref[...]))
    y = x + gate * _dot(p_ref[...].astype(BF16), wp_ref[...])
    o_ref[...] = _rms(y, gf_ref[...]) if final else y


def _ple(x, moe, p, g, wpg, wp, gf, final, tm):
    t, dm = x.shape
    return pl.pallas_call(
        functools.partial(_ple_kernel, final),
        grid=(t // tm,),
        in_specs=[pl.BlockSpec((tm, dm), lambda i: (i, 0)), pl.BlockSpec((tm, dm), lambda i: (i, 0)),
                  pl.BlockSpec((tm, p.shape[1]), lambda i: (i, 0)),
                  _full(g.shape), _full(wpg.shape), _full(wp.shape), _full(gf.shape)],
        out_specs=pl.BlockSpec((tm, dm), lambda i: (i, 0)),
        out_shape=jax.ShapeDtypeStruct((t, dm), F32),
        compiler_params=_cparams(("parallel",)),
        name="ple",
    )(x, moe, p, g, wpg, wp, gf)


def _rope_table(s):
    half = A_HEAD_DIM // 2
    pos = jnp.arange(s, dtype=jnp.int32)
    freqs = ROPE_THETA ** (-jnp.arange(0, half, 2, dtype=F32) / half)
    ang_r = freqs[:, None] * (pos // GRID_W).astype(F32)[None, :]
    ang_c = freqs[:, None] * (pos % GRID_W).astype(F32)[None, :]
    return jnp.concatenate([jnp.cos(ang_r), jnp.sin(ang_r), jnp.cos(ang_c), jnp.sin(ang_c)], axis=0)


def _t5_bucket(rel):
    nb = REL_BUCKETS // 2
    max_exact = nb // 2
    ret = jnp.where(rel > 0, nb, 0)
    r = jnp.abs(rel)
    rf = jnp.maximum(r, 1).astype(F32)
    large = max_exact + (jnp.log(rf / max_exact) / math.log(REL_MAX_DIST / max_exact)
                         * (nb - max_exact)).astype(jnp.int32)
    large = jnp.minimum(large, nb - 1)
    return ret + jnp.where(r < max_exact, r, large)


def _bias_tables(rel_bias, tq, tk):
    assert tk >= REL_MAX_DIST and tq % tk == 0
    rep = tq // tk
    offs = jnp.arange(-1, rep + 1, dtype=jnp.int32) * tk
    u = jnp.arange(tq + tk, dtype=jnp.int32)
    rel = offs[:, None] + (tk - 1) - u[None, :]
    gband = jnp.transpose(rel_bias[_t5_bucket(rel)], (2, 0, 1))[:, :, None, :].astype(F32) * LOG2E
    far = jnp.array([-REL_MAX_DIST, REL_MAX_DIST], dtype=jnp.int32)
    cfar = jnp.transpose(rel_bias[_t5_bucket(far)], (1, 0))[:, None, :].astype(F32) * LOG2E
    return gband, cfar


def _tile(n, pref):
    t = min(n, pref)
    assert n % t == 0
    return t


def kernel(x, p, rel_bias, g_mix, w_in, g_qnorm, g_knorm, conv_w, conv_b, dt_bias_f, dt_bias_b, a_log_f, a_log_b, d_skip, g_ssm, lambda_q1, lambda_k1, lambda_q2, lambda_k2, g_diff, g_sgu, w_spatial, b_spatial, w_branch, w_branch_gate, w_out, g_moe, w_router, w_exp_gate, w_exp_up, w_exp_down, g_ple, w_ple_gate, w_ple, g_final):
    b, s, dm = x.shape
    depth = w_in.shape[0]
    t = b * s
    cap = EC_CAPACITY * s // N_EXPERTS

    tm_proj = _tile(s, 512)
    tm_prep = _tile(s, 1024)
    tq_a, tk_a = _tile(s, 512), _tile(s, 1024)
    tq_c, tk_c = _tile(s, 512), _tile(s, 512)
    cps = _tile(s // SSM_CHUNK, 4)
    nck = _tile(s // SGU_CHUNK, 8)
    tt_moe = _tile(s, 1024)

    rope_tab = _rope_table(s)
    gband, cfar = _bias_tables(rel_bias, tq_c, tk_c)
    hd = A_HEAD_DIM

    sizes = (A_Q_HEADS * hd, A_KV_HEADS * hd, A_KV_HEADS * hd,
             SSM_HEADS * SSM_HEAD_DIM, SSM_HEADS * SSM_HEAD_DIM, SSM_GROUPS * SSM_STATE,
             SSM_GROUPS * SSM_STATE, SSM_HEADS, SSM_HEADS,
             DIFF_HEADS * 2 * DIFF_QK_DIM, DIFF_HEADS * 2 * DIFF_QK_DIM, DIFF_HEADS * DIFF_V_DIM,
             2 * SGU_GROUPS * SGU_GROUP_DIM)
    off = [0]
    for sz in sizes:
        off.append(off[-1] + sz)
    o_a0, o_b0, o_dt0, o_c0, o_d0, o_end = off[0], off[3], off[7], off[9], off[12], off[13]
    dt_pad = V7X_LANES - 2 * SSM_HEADS

    def lane_row(v, width=None):
        v = v.astype(F32).reshape(1, -1)
        if width is not None and v.shape[1] < width:
            v = jnp.pad(v, ((0, 0), (0, width - v.shape[1])))
        return v

    xt = x.reshape(t, dm)
    for i in range(depth):
        wi = w_in[i]
        wat = wi[:, o_a0:o_b0].T.astype(BF16)
        wb = jnp.pad(wi[:, o_b0:o_c0], ((0, 0), (0, dt_pad))).astype(BF16)
        wct = wi[:, o_c0:o_d0].T.astype(BF16)
        wd = wi[:, o_d0:o_end].astype(BF16)
        gmix = lane_row(g_mix[i])
        col = lambda v: v.astype(F32).reshape(-1, 1)
        (qt, kh, vt, z, xbc, dt, cqt, ckh, cvt, du, dv) = _inproj(
            xt, gmix, wat, wb, wct, wd, rope_tab, col(g_qnorm[i]), col(g_knorm[i]), lane_row(g_sgu[i]),
            b, s, tm_proj)

        o_at = _attn_a(qt.reshape(b, A_Q_HEADS, hd, s), kh, vt, tq_a, tk_a, ATTN_INNER).reshape(b, -1, s)

        xa, dl = _ssd_prep(xbc, dt, conv_w[i].astype(F32), lane_row(conv_b[i]),
                           lane_row(jnp.concatenate([dt_bias_f[i], dt_bias_b[i]]), V7X_LANES), s, tm_prep)
        o_b = _ssd(xa, dl, lane_row(jnp.concatenate([a_log_f[i], a_log_b[i]]), V7X_LANES), z,
                   lane_row(jnp.repeat(d_skip[i], SSM_HEAD_DIM)), lane_row(g_ssm[i]), b, s, cps)

        lam_init = 0.8 - 0.6 * math.exp(-0.3 * i)
        lam = (jnp.exp(jnp.sum(lambda_q1[i].astype(F32) * lambda_k1[i].astype(F32)))
               - jnp.exp(jnp.sum(lambda_q2[i].astype(F32) * lambda_k2[i].astype(F32))) + lam_init).reshape(1, 1)
        o_ct = _attn_c(cqt.reshape(b, DIFF_HEADS, 2, DIFF_QK_DIM, s), ckh, cvt, gband, cfar, lam, col(g_diff[i]),
                       1.0 - lam_init, tq_c, tk_c, ATTN_C_INNER).reshape(b, -1, s)

        sgu_bias = jnp.repeat(b_spatial[i].T.astype(F32), SGU_GROUP_DIM, axis=1)
        o_d = _sgu(du, dv, w_spatial[i].astype(BF16), sgu_bias, nck)

        xt = _merge(xt, gmix, o_at, o_b, o_ct, o_d, w_branch_gate[i].astype(BF16),
                    w_branch[i].astype(BF16), w_out[i].astype(BF16), s, tm_proj)

        h2, aff = _router(xt, lane_row(g_moe[i]), w_router[i].T.astype(F32), b, s, tm_proj)
        gates, pos = _select(aff, cap)
        cnt = jnp.sum((pos >= 0).reshape(b, N_EXPERTS, s // tt_moe, tt_moe), axis=3, dtype=jnp.int32)
        tile_major = lambda a: jnp.transpose(a, (0, 2, 1)).reshape(t // tt_moe, N_EXPERTS)
        moe = _moe(h2, pos.reshape(b * N_EXPERTS, 1, s), gates.reshape(b * N_EXPERTS, 1, s), tile_major(cnt),
                   tile_major(jnp.cumsum(cnt, axis=2) - cnt), w_exp_gate[i].astype(BF16), w_exp_up[i].astype(BF16), w_exp_down[i].astype(BF16),
                   tt_moe, MOE_ROW_BLOCK)

        xt = _ple(xt, moe, p[i].reshape(t, -1), lane_row(g_ple[i]), w_ple_gate[i].astype(BF16),
                  w_ple[i].astype(BF16), lane_row(g_final), i == depth - 1, tm_proj)

    return xt.reshape(b, s, dm)
```

```python
import functools
import math

import jax
import jax.numpy as jnp
from jax import lax
from jax.experimental import pallas as pl
from jax.experimental.pallas import tpu as pltpu

F32 = jnp.float32
BF16 = jnp.bfloat16
HIGHEST = lax.Precision.HIGHEST

EPS = 1e-6
GRID_W = 64
A_Q_HEADS, A_KV_HEADS, A_HEAD_DIM = 4, 2, 64
ROPE_THETA = 10000.0
SSM_HEADS, SSM_HEAD_DIM, SSM_GROUPS, SSM_STATE, SSM_CONV, SSM_CHUNK = 4, 64, 2, 128, 5, 128
DIFF_HEADS, DIFF_QK_DIM, DIFF_V_DIM = 4, 32, 64
REL_BUCKETS, REL_MAX_DIST = 32, 128
SGU_GROUPS, SGU_GROUP_DIM, SGU_CHUNK = 4, 64, 128
N_EXPERTS, EC_CAPACITY = 16, 2
N_BRANCHES, BRANCH_WIDTH = 4, 256

V7X_LANES = 128
V7X_SUBLANES = 8
BF16_SUBLANE_PACK = 16
V7X_VMEM_LIMIT_BYTES = 56 * 1024 * 1024

NEG_BIG = -1e30
LOG2E = math.log2(math.e)
MOE_ROW_BLOCK = 160
ATTN_INNER = 4
ATTN_C_INNER = 7


def _cparams(sem):
    return pltpu.CompilerParams(dimension_semantics=sem, vmem_limit_bytes=V7X_VMEM_LIMIT_BYTES)


def _rms(x, g):
    return x * lax.rsqrt(jnp.mean(x * x, axis=-1, keepdims=True) + EPS) * g


def _dot(a, b):
    return jnp.dot(a, b, preferred_element_type=F32)


def _dot_nt(a, b):
    return lax.dot_general(a, b, (((1,), (1,)), ((), ())), preferred_element_type=F32)


def _dot_tn(a, b):
    return lax.dot_general(a, b, (((0,), (0,)), ((), ())), preferred_element_type=F32)


def _full(shape):
    n = len(shape)
    return pl.BlockSpec(shape, lambda *_: (0,) * n)


def _norm_rope_t(x, g_col, tab):
    xn = x * lax.rsqrt(jnp.mean(x * x, axis=0, keepdims=True) + EPS) * g_col
    qd = A_HEAD_DIM // 4
    cr, sr, cc, sc = (tab[i * qd:(i + 1) * qd] for i in range(4))
    r1, r2, c1, c2 = (xn[i * qd:(i + 1) * qd] for i in range(4))
    return jnp.concatenate([r1 * cr - r2 * sr, r1 * sr + r2 * cr, c1 * cc - c2 * sc, c1 * sc + c2 * cc], axis=0)


def _inproj_kernel(x_ref, gmix_ref, wat_ref, wb_ref, wct_ref, wd_ref, tab_ref, gq_ref, gk_ref, gsgu_ref,
                   qa_ref, ka_ref, va_ref, z_ref, xbc_ref, dt_ref, cq_ref, ck_ref, cv_ref,
                   du_ref, dv_ref):
    hb = _rms(x_ref[...], gmix_ref[...]).astype(BF16)
    tm = hb.shape[0]
    hd = A_HEAD_DIM
    ones = jnp.ones((BF16_SUBLANE_PACK, tm), BF16)

    at = _dot_nt(wat_ref[...], hb)
    tab = tab_ref[...]
    q = [_norm_rope_t(at[h * hd:(h + 1) * hd], gq_ref[...], tab) for h in range(A_Q_HEADS)]
    qa_ref[...] = (jnp.concatenate(q, axis=0) * (hd ** -0.5 * LOG2E)).astype(BF16)
    k0 = A_Q_HEADS * hd
    v0 = k0 + A_KV_HEADS * hd
    for g in range(A_KV_HEADS):
        kt = _norm_rope_t(at[k0 + g * hd:k0 + (g + 1) * hd], gk_ref[...], tab)
        ka_ref[g] = jnp.transpose(kt).astype(BF16)
        va_ref[g, :hd] = at[v0 + g * hd:v0 + (g + 1) * hd].astype(BF16)
        va_ref[g, hd:] = ones

    b = _dot(hb, wb_ref[...])
    dx = SSM_HEADS * SSM_HEAD_DIM
    conv_ch = dx + 2 * SSM_GROUPS * SSM_STATE
    z_ref[...] = b[:, :dx]
    xbc_ref[...] = b[:, dx:dx + conv_ch]
    dt_ref[...] = b[:, dx + conv_ch:]

    ct = _dot_nt(wct_ref[...], hb)
    nqc = DIFF_HEADS * 2 * DIFF_QK_DIM
    cq_ref[...] = (ct[:nqc] * (DIFF_QK_DIM ** -0.5 * LOG2E)).astype(BF16)
    ck = jnp.transpose(ct[nqc:2 * nqc]).astype(BF16)
    for h in range(DIFF_HEADS):
        for c in range(2):
            lo = (2 * h + c) * DIFF_QK_DIM
            ck_ref[h, c] = ck[:, lo:lo + DIFF_QK_DIM]
        cv_ref[h, :DIFF_V_DIM] = ct[2 * nqc + h * DIFF_V_DIM:2 * nqc + (h + 1) * DIFF_V_DIM].astype(BF16)
        cv_ref[h, DIFF_V_DIM:] = ones

    d = jax.nn.gelu(_dot(hb, wd_ref[...]))
    w = SGU_GROUPS * SGU_GROUP_DIM
    du_ref[...] = d[:, :w]
    dv_ref[...] = _rms(d[:, w:], gsgu_ref[...]).astype(BF16)


def _inproj(x, gmix, wat, wb, wct, wd, tab, gq, gk, gsgu, b, s, tm):
    t, dm = x.shape
    nst = s // tm
    row = lambda w: pl.BlockSpec((tm, w), lambda i: (i, 0))
    lanes = lambda *lead: pl.BlockSpec((None,) + lead + (tm,), lambda i: (i // nst,) + (0,) * len(lead) + (i % nst,))
    hd, pk = A_HEAD_DIM, BF16_SUBLANE_PACK
    specs_shapes = [
        (lanes(A_Q_HEADS * hd), (b, A_Q_HEADS * hd, s), BF16),
        (pl.BlockSpec((None, A_KV_HEADS, tm, hd), lambda i: (i // nst, 0, i % nst, 0)), (b, A_KV_HEADS, s, hd), BF16),
        (lanes(A_KV_HEADS, hd + pk), (b, A_KV_HEADS, hd + pk, s), BF16),
        (row(256), (t, 256), F32), (row(768), (t, 768), F32), (row(128), (t, 128), F32),
        (lanes(DIFF_HEADS * 2 * DIFF_QK_DIM), (b, DIFF_HEADS * 2 * DIFF_QK_DIM, s), BF16),
        (pl.BlockSpec((None, DIFF_HEADS, 2, tm, DIFF_QK_DIM), lambda i: (i // nst, 0, 0, i % nst, 0)),
         (b, DIFF_HEADS, 2, s, DIFF_QK_DIM), BF16),
        (lanes(DIFF_HEADS, DIFF_V_DIM + pk), (b, DIFF_HEADS, DIFF_V_DIM + pk, s), BF16),
        (row(256), (t, 256), F32), (row(256), (t, 256), BF16)]
    return pl.pallas_call(
        _inproj_kernel,
        grid=(t // tm,),
        in_specs=[row(dm), _full(gmix.shape), _full(wat.shape), _full(wb.shape), _full(wct.shape),
                  _full(wd.shape), pl.BlockSpec((tab.shape[0], tm), lambda i: (0, i % nst)),
                  _full(gq.shape), _full(gk.shape), _full(gsgu.shape)],
        out_specs=[sp for sp, _, _ in specs_shapes],
        out_shape=[jax.ShapeDtypeStruct(shp, dt) for _, shp, dt in specs_shapes],
        compiler_params=_cparams(("parallel",)),
        name="inproj",
    )(x, gmix, wat, wb, wct, wd, tab, gq, gk, gsgu)


def _osm_tile(s, mt, shift, vt, m_ref, acc_ref, c):
    m_old = m_ref[c]
    if shift is not None:
        mt = mt + shift
    m_new = jnp.maximum(m_old, mt)
    p = jnp.exp2(s - (m_new if shift is None else m_new - shift))
    acc_ref[c] = jnp.exp2(m_old - m_new) * acc_ref[c] + _dot(vt, p.astype(BF16))
    m_ref[c] = m_new


def _osm_init(m_ref, acc_ref):
    m_ref[...] = jnp.full(m_ref.shape, NEG_BIG, F32)
    acc_ref[...] = jnp.zeros(acc_ref.shape, F32)


def _osm_result(acc_ref, c, dv):
    acc = acc_ref[c]
    return acc[:dv] / acc[dv:dv + 1]


def _attn_scores(tk, j, qt_ref, k_of, buf, bias_tile=None):
    s_ref, mx_ref = buf
    off = pl.multiple_of(j * tk, tk)
    for c in range(qt_ref.shape[0]):
        s = _dot(k_of(c, off), qt_ref[c])
        if bias_tile is not None:
            s = s + bias_tile
        s_ref[c] = s
        mx_ref[c] = jnp.max(s, axis=0, keepdims=True)


def _attn_values(tk, j, nmaps, vt_ref, buf, m_ref, acc_ref, shift=None):
    s_ref, mx_ref = buf
    vt = vt_ref[:, pl.ds(pl.multiple_of(j * tk, tk), tk)]
    for c in range(nmaps):
        _osm_tile(s_ref[c], mx_ref[c], shift, vt, m_ref, acc_ref, c)


def _pipelined(n, scores, values, s_refs, inner):
    last = jnp.maximum(n - 1, 0)
    span = 2 * inner
    scores(jnp.minimum(0, last), s_refs[0])

    def double(g):
        scores(jnp.minimum(g + 1, last), s_refs[1])
        values(g, s_refs[0])
        scores(jnp.minimum(g + 2, last), s_refs[0])
        values(g + 1, s_refs[1])

    def unrolled(t, carry):
        for w in range(inner):
            double(span * t + 2 * w)
        return carry

    lax.fori_loop(0, n // span, unrolled, 0)
    base = (n // span) * span

    def rolled(t, carry):
        double(base + 2 * t)
        return carry

    lax.fori_loop(0, (n - base) // 2, rolled, 0)

    @pl.when(n % 2 == 1)
    def _():
        values(n - 1, s_refs[0])


def _attn_a_kernel(tk, inner, qt_ref, k_ref, vt_ref, o_ref, m_ref, acc_ref, sa_ref, sb_ref, mxa_ref, mxb_ref):
    _osm_init(m_ref, acc_ref)
    nq = qt_ref.shape[0]
    k_of = lambda c, off: k_ref[pl.ds(off, tk), :]
    _pipelined(k_ref.shape[0] // tk,
               lambda j, buf: _attn_scores(tk, j, qt_ref, k_of, buf),
               lambda j, buf: _attn_values(tk, j, nq, vt_ref, buf, m_ref, acc_ref),
               ((sa_ref, mxa_ref), (sb_ref, mxb_ref)), inner)
    for c in range(nq):
        o_ref[c] = _osm_result(acc_ref, c, o_ref.shape[1]).astype(o_ref.dtype)


def _attn_a(qt, k, vt, tq, tk, inner):
    b, hq, d, s = qt.shape
    hkv = k.shape[1]
    rep = hq // hkv
    da = vt.shape[2]
    qspec = pl.BlockSpec((None, rep, d, tq), lambda bi, g, i: (bi, g, 0, i))
    return pl.pallas_call(
        functools.partial(_attn_a_kernel, tk, inner),
        grid=(b, hkv, s // tq),
        in_specs=[qspec,
                  pl.BlockSpec((None, None, s, d), lambda bi, g, i: (bi, g, 0, 0)),
                  pl.BlockSpec((None, None, da, s), lambda bi, g, i: (bi, g, 0, 0))],
        out_specs=qspec,
        out_shape=jax.ShapeDtypeStruct(qt.shape, BF16),
        scratch_shapes=[pltpu.VMEM((rep, 1, tq), F32), pltpu.VMEM((rep, da, tq), F32),
                        pltpu.VMEM((rep, tk, tq), F32), pltpu.VMEM((rep, tk, tq), F32),
                        pltpu.VMEM((rep, 1, tq), F32), pltpu.VMEM((rep, 1, tq), F32)],
        compiler_params=_cparams(("parallel", "parallel", "arbitrary")),
        name="attn_a",
    )(qt, k, vt)


def _attn_c_kernel(tk, inner, out_scale, qt_ref, k_ref, vt_ref, gband_ref, cfar_ref, lam_ref, gd_ref, o_ref,
                   m_ref, acc_ref, sa_ref, sb_ref, mxa_ref, mxb_ref):
    i = pl.program_id(2)
    tq = qt_ref.shape[2]
    nk = k_ref.shape[1] // tk
    nband = gband_ref.shape[0]
    dv = o_ref.shape[0]
    _osm_init(m_ref, acc_ref)
    k_of = lambda c, off: k_ref[c, pl.ds(off, tk), :]

    j_lo = (tq // tk) * i - 1
    n_left = jnp.clip(j_lo, 0, nk)
    j_hi = jnp.clip(j_lo + nband, 0, nk)
    n_far = n_left + nk - j_hi
    cf = cfar_ref[...]

    def far_tile(t):
        j = jnp.where(t < n_left, t, t - n_left + j_hi)
        return jnp.clip(j, 0, nk - 1), jnp.where(t < n_left, cf[:, 0:1], cf[:, 1:2])

    def far_scores(t, buf):
        _attn_scores(tk, far_tile(t)[0], qt_ref, k_of, buf)

    def far_values(t, buf):
        j, sh = far_tile(t)
        _attn_values(tk, j, 2, vt_ref, buf, m_ref, acc_ref, shift=sh)

    _pipelined(n_far, far_scores, far_values, ((sa_ref, mxa_ref), (sb_ref, mxb_ref)), inner)

    d_lo = n_left - j_lo

    def band_scores(t, buf):
        d = jnp.clip(d_lo + t, 0, nband - 1)
        g = gband_ref[d]
        tile = pltpu.roll(jnp.broadcast_to(g, (tk, g.shape[1])), tq + 1, 1, stride=1, stride_axis=0)
        _attn_scores(tk, jnp.clip(j_lo + d, 0, nk - 1), qt_ref, k_of, buf, bias_tile=tile[:, :tq])

    def band_values(t, buf):
        _attn_values(tk, j_lo + d_lo + t, 2, vt_ref, buf, m_ref, acc_ref)

    _pipelined(j_hi - n_left, band_scores, band_values, ((sa_ref, mxa_ref), (sb_ref, mxb_ref)), 1)

    o = _osm_result(acc_ref, 0, dv) - lam_ref[...] * _osm_result(acc_ref, 1, dv)
    o = o * lax.rsqrt(jnp.mean(o * o, axis=0, keepdims=True) + EPS) * gd_ref[...]
    o_ref[...] = (o * out_scale).astype(o_ref.dtype)


def _attn_c(qt, k, vt, gband, cfar, lam, gd, out_scale, tq, tk, inner):
    b, h, _, dqk, s = qt.shape
    dva = vt.shape[2]
    dv = gd.shape[0]
    nband = gband.shape[1]
    return pl.pallas_call(
        functools.partial(_attn_c_kernel, tk, inner, out_scale),
        grid=(b, h, s // tq),
        in_specs=[pl.BlockSpec((None, None, 2, dqk, tq), lambda bi, hi, i: (bi, hi, 0, 0, i)),
                  pl.BlockSpec((None, None, 2, s, dqk), lambda bi, hi, i: (bi, hi, 0, 0, 0)),
                  pl.BlockSpec((None, None, dva, s), lambda bi, hi, i: (bi, hi, 0, 0)),
                  pl.BlockSpec((None, nband, 1, tq + tk), lambda bi, hi, i: (hi, 0, 0, 0)),
                  pl.BlockSpec((None, 1, 2), lambda bi, hi, i: (hi, 0, 0)),
                  _full(lam.shape), _full(gd.shape)],
        out_specs=pl.BlockSpec((None, None, dv, tq), lambda bi, hi, i: (bi, hi, 0, i)),
        out_shape=jax.ShapeDtypeStruct((b, h, dv, s), BF16),
        scratch_shapes=[pltpu.VMEM((2, 1, tq), F32), pltpu.VMEM((2, dva, tq), F32),
                        pltpu.VMEM((2, tk, tq), F32), pltpu.VMEM((2, tk, tq), F32),
                        pltpu.VMEM((2, 1, tq), F32), pltpu.VMEM((2, 1, tq), F32)],
        compiler_params=_cparams(("parallel", "parallel", "arbitrary")),
        name="attn_c",
    )(qt, k, vt, gband, cfar, lam, gd)


def _ssd_prep_kernel(nst, xc_ref, xp_ref, xn_ref, dt_ref, cw_ref, cb_ref, dtb_ref, xa_ref, dl_ref, ext_ref):
    i = pl.program_id(0)
    tm = xc_ref.shape[0]
    halo = V7X_SUBLANES
    pos = i % nst
    keep_prev = (pos != 0).astype(F32)
    keep_next = (pos != nst - 1).astype(F32)
    ext_ref[0:halo, :] = xp_ref[...] * keep_prev
    ext_ref[halo:halo + tm, :] = xc_ref[...]
    ext_ref[halo + tm:, :] = xn_ref[...] * keep_next
    pad_l = (SSM_CONV - 1) // 2
    cw = cw_ref[...]
    acc = jnp.zeros(xc_ref.shape, F32) + cb_ref[...]
    for t in range(SSM_CONV):
        acc = acc + ext_ref[halo - pad_l + t:halo - pad_l + t + tm, :] * cw[t:t + 1, :]
    xa_ref[...] = jax.nn.silu(acc)
    dl_ref[...] = jax.nn.softplus(dt_ref[...] + dtb_ref[...])


def _ssd_prep(xbc, dt, cw, cb, dtb, s, tm):
    t, c = xbc.shape
    nst = s // tm
    hb = tm // V7X_SUBLANES
    nhb = t // V7X_SUBLANES
    return pl.pallas_call(
        functools.partial(_ssd_prep_kernel, nst),
        grid=(t // tm,),
        in_specs=[pl.BlockSpec((tm, c), lambda i: (i, 0)),
                  pl.BlockSpec((V7X_SUBLANES, c), lambda i: (jnp.maximum(i * hb - 1, 0), 0)),
                  pl.BlockSpec((V7X_SUBLANES, c), lambda i: (jnp.minimum((i + 1) * hb, nhb - 1), 0)),
                  pl.BlockSpec((tm, dt.shape[1]), lambda i: (i, 0)),
                  _full(cw.shape), _full(cb.shape), _full(dtb.shape)],
        out_specs=[pl.BlockSpec((tm, c), lambda i: (i, 0)),
                   pl.BlockSpec((tm, dt.shape[1]), lambda i: (i, 0))],
        out_shape=[jax.ShapeDtypeStruct((t, c), F32), jax.ShapeDtypeStruct(dt.shape, F32)],
        scratch_shapes=[pltpu.VMEM((tm + 2 * V7X_SUBLANES, c), F32)],
        compiler_params=_cparams(("parallel",)),
        name="ssd_prep",
    )(xbc, xbc, xbc, dt, cw, cb, dtb)


def _ssd_chunk(reverse, xa, dl, alog, st_ref):
    ln = SSM_CHUNK
    row = lax.broadcasted_iota(jnp.int32, (ln, ln), 0)
    col = lax.broadcasted_iota(jnp.int32, (ln, ln), 1)
    tri = (col >= row) if reverse else (col <= row)
    eye = row == col
    last = 0 if reverse else ln - 1
    lane0 = SSM_HEADS if reverse else 0
    a_neg = -jnp.exp(alog)
    acs = jnp.dot(tri.astype(F32), dl * a_neg, precision=HIGHEST, preferred_element_type=F32)
    dx = SSM_HEADS * SSM_HEAD_DIM
    gw = SSM_STATE
    hpg = SSM_HEADS // SSM_GROUPS
    ys = []
    for g in range(SSM_GROUPS):
        bg = xa[:, dx + g * gw:dx + (g + 1) * gw].astype(BF16)
        cg = xa[:, dx + SSM_GROUPS * gw + g * gw:dx + SSM_GROUPS * gw + (g + 1) * gw].astype(BF16)
        cb = _dot_nt(cg, bg)
        for r in range(hpg):
            h = g * hpg + r
            ln_h = lane0 + h
            ac = acs[:, ln_h:ln_h + 1]
            ar = jnp.sum(jnp.where(eye, ac, 0.0), axis=0, keepdims=True)
            dec = jnp.exp(jnp.where(tri, ac - ar, NEG_BIG))
            xdt = xa[:, h * SSM_HEAD_DIM:(h + 1) * SSM_HEAD_DIM] * dl[:, ln_h:ln_h + 1]
            st = st_ref[h]
            y = _dot((cb * dec).astype(BF16), xdt.astype(BF16))
            y = y + _dot(cg, st.astype(BF16)) * jnp.exp(ac)
            a_last = acs[last:last + 1, ln_h:ln_h + 1]
            st_ref[h] = st * jnp.exp(a_last) + _dot_tn(bg, (xdt * jnp.exp(a_last - ac)).astype(BF16))
            ys.append(y)
    return jnp.concatenate(ys, axis=1)


def _ssd_fwd_kernel(cps, xa_ref, dl_ref, alog_ref, y_ref, st_ref):
    @pl.when(pl.program_id(1) == 0)
    def _():
        st_ref[...] = jnp.zeros(st_ref.shape, F32)

    for c in range(cps):
        sl = slice(c * SSM_CHUNK, (c + 1) * SSM_CHUNK)
        y_ref[sl, :] = _ssd_chunk(False, xa_ref[sl, :], dl_ref[sl, :], alog_ref[...], st_ref)


def _ssd_bwd_kernel(cps, xa_ref, dl_ref, alog_ref, yf_ref, z_ref, dsk_ref, g_ref, o_ref, st_ref):
    @pl.when(pl.program_id(1) == 0)
    def _():
        st_ref[...] = jnp.zeros(st_ref.shape, F32)

    dx = SSM_HEADS * SSM_HEAD_DIM
    for c in reversed(range(cps)):
        sl = slice(c * SSM_CHUNK, (c + 1) * SSM_CHUNK)
        xa = xa_ref[sl, :]
        yb = _ssd_chunk(True, xa, dl_ref[sl, :], alog_ref[...], st_ref)
        y = yf_ref[sl, :] + yb + dsk_ref[...] * xa[:, :dx]
        y = y * jax.nn.silu(z_ref[sl, :])
        o_ref[sl, :] = _rms(y, g_ref[...]).astype(o_ref.dtype)


def _ssd(xa, dl, alog, z, dsk, gssm, b, s, cps):
    t, c = xa.shape
    tm = cps * SSM_CHUNK
    nblk = s // tm
    dx = SSM_HEADS * SSM_HEAD_DIM
    st = pltpu.VMEM((SSM_HEADS, SSM_STATE, SSM_HEAD_DIM), F32)
    fw = lambda w: pl.BlockSpec((tm, w), lambda bi, ci: (bi * nblk + ci, 0))
    bw = lambda w: pl.BlockSpec((tm, w), lambda bi, ci: (bi * nblk + nblk - 1 - ci, 0))
    yf = pl.pallas_call(
        functools.partial(_ssd_fwd_kernel, cps),
        grid=(b, nblk),
        in_specs=[fw(c), fw(dl.shape[1]), _full(alog.shape)],
        out_specs=fw(dx),
        out_shape=jax.ShapeDtypeStruct((t, dx), F32),
        scratch_shapes=[st],
        compiler_params=_cparams(("parallel", "arbitrary")),
        name="ssd_fwd",
    )(xa, dl, alog)
    return pl.pallas_call(
        functools.partial(_ssd_bwd_kernel, cps),
        grid=(b, nblk),
        in_specs=[bw(c), bw(dl.shape[1]), _full(alog.shape), bw(dx), bw(dx), _full(dsk.shape),
                  _full(gssm.shape)],
        out_specs=bw(dx),
        out_shape=jax.ShapeDtypeStruct((t, dx), BF16),
        scratch_shapes=[st],
        compiler_params=_cparams(("parallel", "arbitrary")),
        name="ssd_bwd",
    )(xa, dl, alog, yf, z, dsk, gssm)


def _sgu_kernel(nck, u_ref, v_ref, w_ref, bias_ref, o_ref):
    for c in range(nck):
        sl = slice(c * SGU_CHUNK, (c + 1) * SGU_CHUNK)
        v = v_ref[sl, :]
        sv = [_dot(w_ref[g], v[:, g * SGU_GROUP_DIM:(g + 1) * SGU_GROUP_DIM]) for g in range(SGU_GROUPS)]
        o_ref[sl, :] = (u_ref[sl, :] * (jnp.concatenate(sv, axis=1) + bias_ref[...])).astype(o_ref.dtype)


def _sgu(u, v, w, bias, nck):
    t, c = u.shape
    tm = nck * SGU_CHUNK
    return pl.pallas_call(
        functools.partial(_sgu_kernel, nck),
        grid=(t // tm,),
        in_specs=[pl.BlockSpec((tm, c), lambda i: (i, 0)), pl.BlockSpec((tm, c), lambda i: (i, 0)),
                  _full(w.shape), _full(bias.shape)],
        out_specs=pl.BlockSpec((tm, c), lambda i: (i, 0)),
        out_shape=jax.ShapeDtypeStruct((t, c), BF16),
        compiler_params=_cparams(("parallel",)),
        name="sgu",
    )(u, v, w, bias)


def _merge_kernel(x_ref, gmix_ref, oat_ref, ob_ref, oct_ref, od_ref, wg_ref, wbr_ref, wo_ref, gmoe_ref, wrt_ref,
                  o_ref, h_ref, aff_ref):
    x = x_ref[...]
    dm = x.shape[1]
    hb = _rms(x, gmix_ref[...]).astype(BF16)
    merged = jnp.zeros(x.shape, F32)
    for n, (br, transposed) in enumerate(((oat_ref, True), (ob_ref, False), (oct_ref, True), (od_ref, False))):
        gate = jax.nn.sigmoid(_dot(hb, wg_ref[:, n * dm:(n + 1) * dm]))
        wide = _dot_tn(br[...], wbr_ref[n]) if transposed else _dot(br[...], wbr_ref[n])
        merged = merged + gate * wide
    x = x + _dot(merged.astype(BF16), wo_ref[...])
    o_ref[...] = x

    h = _rms(x, gmoe_ref[...])
    h_ref[...] = h.astype(BF16)
    logits = lax.dot_general(wrt_ref[...], h, (((1,), (1,)), ((), ())), precision=HIGHEST,
                             preferred_element_type=F32)
    e = jnp.exp(logits - jnp.max(logits, axis=0, keepdims=True))
    aff_ref[...] = e / jnp.sum(e, axis=0, keepdims=True)


def _merge(x, gmix, oat, ob, oct, od, wg, wbr, wo, gmoe, wrt, b, s, tm):
    t, dm = x.shape
    ne = wrt.shape[0]
    nst = s // tm
    row = lambda w: pl.BlockSpec((tm, w), lambda i: (i, 0))
    lanes = lambda w: pl.BlockSpec((None, w, tm), lambda i: (i // nst, 0, i % nst))
    return pl.pallas_call(
        _merge_kernel,
        grid=(t // tm,),
        in_specs=[row(dm), _full(gmix.shape), lanes(BRANCH_WIDTH), row(BRANCH_WIDTH), lanes(BRANCH_WIDTH),
                  row(BRANCH_WIDTH), _full(wg.shape), _full(wbr.shape), _full(wo.shape), _full(gmoe.shape),
                  _full(wrt.shape)],
        out_specs=[row(dm), row(dm), lanes(ne)],
        out_shape=[jax.ShapeDtypeStruct((t, dm), F32), jax.ShapeDtypeStruct((t, dm), BF16),
                   jax.ShapeDtypeStruct((b, ne, s), F32)],
        compiler_params=_cparams(("parallel",)),
        name="merge",
    )(x, gmix, oat, ob, oct, od, wg, wbr, wo, gmoe, wrt)


def _select_kernel(cap, aff_ref, gate_ref, pos_ref):
    ne, s = aff_ref.shape
    bits = pltpu.bitcast(aff_ref[...], jnp.int32)
    capf = jnp.float32(cap)

    def bisect(i, thr):
        cand = thr | lax.shift_left(jnp.int32(1), 30 - i)
        cnt = jnp.sum((bits >= cand).astype(F32), axis=1, keepdims=True)
        return jnp.where(cnt >= capf, cand, thr)

    thr = lax.fori_loop(0, 31, bisect, jnp.zeros((ne, 1), jnp.int32))
    need = capf - jnp.sum((bits > thr).astype(F32), axis=1, keepdims=True)
    lanes = V7X_LANES
    r = lax.broadcasted_iota(jnp.int32, (lanes, lanes), 0)
    c = lax.broadcasted_iota(jnp.int32, (lanes, lanes), 1)
    before = (r < c).astype(BF16)

    def block(jb, carry):
        n_eq, n_keep = carry
        off = pl.multiple_of(jb * lanes, lanes)
        a = aff_ref[:, pl.ds(off, lanes)]
        bb = pltpu.bitcast(a, jnp.int32)
        eq = bb == thr
        eqf = eq.astype(F32)
        keep = jnp.logical_or(bb > thr, jnp.logical_and(eq, n_eq + _dot(eqf.astype(BF16), before) < need))
        keepf = keep.astype(F32)
        gate_ref[:, pl.ds(off, lanes)] = jnp.where(keep, a, 0.0)
        pos_ref[:, pl.ds(off, lanes)] = jnp.where(keep, n_keep + _dot(keepf.astype(BF16), before), -1.0)
        return n_eq + jnp.sum(eqf, axis=1, keepdims=True), n_keep + jnp.sum(keepf, axis=1, keepdims=True)

    zero = jnp.zeros((ne, 1), F32)
    lax.fori_loop(0, s // lanes, block, (zero, zero))


def _select(aff, cap):
    b, ne, s = aff.shape
    spec = pl.BlockSpec((None, ne, s), lambda i: (i, 0, 0))
    return pl.pallas_call(
        functools.partial(_select_kernel, cap),
        grid=(b,),
        in_specs=[spec],
        out_specs=[spec, spec],
        out_shape=[jax.ShapeDtypeStruct(aff.shape, F32), jax.ShapeDtypeStruct(aff.shape, F32)],
        compiler_params=_cparams(("parallel",)),
        name="moe_select",
    )(aff)


def _moe_kernel(rb, tt, cnt_ref, base_ref, h_ref, pos_ref, gate_ref, wg_ref, wu_ref, wd_ref, o_ref):
    i = pl.program_id(0)
    e = pl.program_id(1)
    nsub = h_ref.shape[0] // tt

    @pl.when(e == 0)
    def _():
        o_ref[...] = jnp.zeros(o_ref.shape, F32)

    for sub in range(nsub):
        rows = slice(sub * tt, (sub + 1) * tt)
        tile = i * nsub + sub
        pos = pos_ref[:, rows] - base_ref[tile, e].astype(F32)
        gate = gate_ref[:, rows]

        def block(bi, carry):
            slot = lax.broadcasted_iota(jnp.int32, (rb, 1), 0).astype(F32) + (bi * rb).astype(F32)
            hit = pos == slot
            onehot = hit.astype(BF16)
            xc = _dot(onehot, h_ref[rows, :]).astype(BF16)
            hid = jax.nn.silu(_dot(xc, wg_ref[...])) * _dot(xc, wu_ref[...])
            y = _dot(hid.astype(BF16), wd_ref[...])
            g = jnp.sum(jnp.where(hit, gate, 0.0), axis=1, keepdims=True)
            o_ref[rows, :] += _dot_tn(onehot, (y * g).astype(BF16))
            return carry

        lax.fori_loop(0, (cnt_ref[tile, e] + rb - 1) // rb, block, 0)


def _moe(h, pos, gates, cnt, base, wg, wu, wd, tt, nsub, rb):
    t, dm = h.shape
    ne, _, dff = wg.shape
    ts = tt * nsub
    nst = pos.shape[2] // ts
    lane_row = pl.BlockSpec((None, 1, ts), lambda i, e, *_: ((i // nst) * ne + e, 0, i % nst))
    return pl.pallas_call(
        functools.partial(_moe_kernel, rb, tt),
        grid_spec=pltpu.PrefetchScalarGridSpec(
            num_scalar_prefetch=2,
            grid=(t // ts, ne),
            in_specs=[pl.BlockSpec((ts, dm), lambda i, e, *_: (i, 0)), lane_row, lane_row,
                      pl.BlockSpec((None, dm, dff), lambda i, e, *_: (e, 0, 0)),
                      pl.BlockSpec((None, dm, dff), lambda i, e, *_: (e, 0, 0)),
                      pl.BlockSpec((None, dff, dm), lambda i, e, *_: (e, 0, 0))],
            out_specs=pl.BlockSpec((ts, dm), lambda i, e, *_: (i, 0))),
        out_shape=jax.ShapeDtypeStruct((t, dm), F32),
        compiler_params=_cparams(("parallel", "arbitrary")),
        name="moe_ffn",
    )(cnt, base, h, pos, gates, wg, wu, wd)


def _ple_kernel(final, x_ref, moe_ref, p_ref, g_ref, wpg_ref, wp_ref, gf_ref, o_ref):
    x = x_ref[...] + moe_ref[...]
    gate = jax.nn.sigmoid(_dot(_rms(x, g_ref[...]).astype(BF16), wpg_ref[...]))
    y = x + gate * _dot(p_ref[...].astype(BF16), wp_ref[...])
    o_ref[...] = _rms(y, gf_ref[...]) if final else y


def _ple(x, moe, p, g, wpg, wp, gf, final, tm):
    t, dm = x.shape
    return pl.pallas_call(
        functools.partial(_ple_kernel, final),
        grid=(t // tm,),
        in_specs=[pl.BlockSpec((tm, dm), lambda i: (i, 0)), pl.BlockSpec((tm, dm), lambda i: (i, 0)),
                  pl.BlockSpec((tm, p.shape[1]), lambda i: (i, 0)),
                  _full(g.shape), _full(wpg.shape), _full(wp.shape), _full(gf.shape)],
        out_specs=pl.BlockSpec((tm, dm), lambda i: (i, 0)),
        out_shape=jax.ShapeDtypeStruct((t, dm), F32),
        compiler_params=_cparams(("parallel",)),
        name="ple",
    )(x, moe, p, g, wpg, wp, gf)


def _rope_table(s):
    half = A_HEAD_DIM // 2
    pos = jnp.arange(s, dtype=jnp.int32)
    freqs = ROPE_THETA ** (-jnp.arange(0, half, 2, dtype=F32) / half)
    ang_r = freqs[:, None] * (pos // GRID_W).astype(F32)[None, :]
    ang_c = freqs[:, None] * (pos % GRID_W).astype(F32)[None, :]
    return jnp.concatenate([jnp.cos(ang_r), jnp.sin(ang_r), jnp.cos(ang_c), jnp.sin(ang_c)], axis=0)


def _t5_bucket(rel):
    nb = REL_BUCKETS // 2
    max_exact = nb // 2
    ret = jnp.where(rel > 0, nb, 0)
    r = jnp.abs(rel)
    rf = jnp.maximum(r, 1).astype(F32)
    large = max_exact + (jnp.log(rf / max_exact) / math.log(REL_MAX_DIST / max_exact)
                         * (nb - max_exact)).astype(jnp.int32)
    large = jnp.minimum(large, nb - 1)
    return ret + jnp.where(r < max_exact, r, large)


def _bias_tables(rel_bias, tq, tk):
    assert tk >= REL_MAX_DIST and tq % tk == 0
    rep = tq // tk
    offs = jnp.arange(-1, rep + 1, dtype=jnp.int32) * tk
    u = jnp.arange(tq + tk, dtype=jnp.int32)
    rel = offs[:, None] + (tk - 1) - u[None, :]
    gband = jnp.transpose(rel_bias[_t5_bucket(rel)], (2, 0, 1))[:, :, None, :].astype(F32) * LOG2E
    far = jnp.array([-REL_MAX_DIST, REL_MAX_DIST], dtype=jnp.int32)
    cfar = jnp.transpose(rel_bias[_t5_bucket(far)], (1, 0))[:, None, :].astype(F32) * LOG2E
    return gband, cfar


def _tile(n, pref):
    t = min(n, pref)
    assert n % t == 0
    return t


def kernel(x, p, rel_bias, g_mix, w_in, g_qnorm, g_knorm, conv_w, conv_b, dt_bias_f, dt_bias_b, a_log_f, a_log_b, d_skip, g_ssm, lambda_q1, lambda_k1, lambda_q2, lambda_k2, g_diff, g_sgu, w_spatial, b_spatial, w_branch, w_branch_gate, w_out, g_moe, w_router, w_exp_gate, w_exp_up, w_exp_down, g_ple, w_ple_gate, w_ple, g_final):
    b, s, dm = x.shape
    depth = w_in.shape[0]
    t = b * s
    cap = EC_CAPACITY * s // N_EXPERTS

    tm_proj = _tile(s, 512)
    tm_prep = _tile(s, 1024)
    tq_a, tk_a = _tile(s, 512), _tile(s, 1024)
    tq_c, tk_c = _tile(s, 512), _tile(s, 512)
    cps = _tile(s // SSM_CHUNK, 4)
    nck = _tile(s // SGU_CHUNK, 8)
    tt_moe = _tile(s, 1024)

    rope_tab = _rope_table(s)
    gband, cfar = _bias_tables(rel_bias, tq_c, tk_c)
    hd = A_HEAD_DIM

    sizes = (A_Q_HEADS * hd, A_KV_HEADS * hd, A_KV_HEADS * hd,
             SSM_HEADS * SSM_HEAD_DIM, SSM_HEADS * SSM_HEAD_DIM, SSM_GROUPS * SSM_STATE,
             SSM_GROUPS * SSM_STATE, SSM_HEADS, SSM_HEADS,
             DIFF_HEADS * 2 * DIFF_QK_DIM, DIFF_HEADS * 2 * DIFF_QK_DIM, DIFF_HEADS * DIFF_V_DIM,
             2 * SGU_GROUPS * SGU_GROUP_DIM)
    off = [0]
    for sz in sizes:
        off.append(off[-1] + sz)
    o_a0, o_b0, o_dt0, o_c0, o_d0, o_end = off[0], off[3], off[7], off[9], off[12], off[13]
    dt_pad = V7X_LANES - 2 * SSM_HEADS

    def lane_row(v, width=None):
        v = v.astype(F32).reshape(1, -1)
        if width is not None and v.shape[1] < width:
            v = jnp.pad(v, ((0, 0), (0, width - v.shape[1])))
        return v

    xt = x.reshape(t, dm)
    for i in range(depth):
        wi = w_in[i]
        wat = wi[:, o_a0:o_b0].T.astype(BF16)
        wb = jnp.pad(wi[:, o_b0:o_c0], ((0, 0), (0, dt_pad))).astype(BF16)
        wct = wi[:, o_c0:o_d0].T.astype(BF16)
        wd = wi[:, o_d0:o_end].astype(BF16)
        gmix = lane_row(g_mix[i])
        col = lambda v: v.astype(F32).reshape(-1, 1)
        (qt, kh, vt, z, xbc, dt, cqt, ckh, cvt, du, dv) = _inproj(
            xt, gmix, wat, wb, wct, wd, rope_tab, col(g_qnorm[i]), col(g_knorm[i]), lane_row(g_sgu[i]),
            b, s, tm_proj)

        o_at = _attn_a(qt.reshape(b, A_Q_HEADS, hd, s), kh, vt, tq_a, tk_a, ATTN_INNER).reshape(b, -1, s)

        xa, dl = _ssd_prep(xbc, dt, conv_w[i].astype(F32), lane_row(conv_b[i]),
                           lane_row(jnp.concatenate([dt_bias_f[i], dt_bias_b[i]]), V7X_LANES), s, tm_prep)
        o_b = _ssd(xa, dl, lane_row(jnp.concatenate([a_log_f[i], a_log_b[i]]), V7X_LANES), z,
                   lane_row(jnp.repeat(d_skip[i], SSM_HEAD_DIM)), lane_row(g_ssm[i]), b, s, cps)

        lam_init = 0.8 - 0.6 * math.exp(-0.3 * i)
        lam = (jnp.exp(jnp.sum(lambda_q1[i].astype(F32) * lambda_k1[i].astype(F32)))
               - jnp.exp(jnp.sum(lambda_q2[i].astype(F32) * lambda_k2[i].astype(F32))) + lam_init).reshape(1, 1)
        o_ct = _attn_c(cqt.reshape(b, DIFF_HEADS, 2, DIFF_QK_DIM, s), ckh, cvt, gband, cfar, lam, col(g_diff[i]),
                       1.0 - lam_init, tq_c, tk_c, ATTN_C_INNER).reshape(b, -1, s)

        sgu_bias = jnp.repeat(b_spatial[i].T.astype(F32), SGU_GROUP_DIM, axis=1)
        o_d = _sgu(du, dv, w_spatial[i].astype(BF16), sgu_bias, nck)

        xt, h2, aff = _merge(xt, gmix, o_at, o_b, o_ct, o_d, w_branch_gate[i].astype(BF16),
                             w_branch[i].astype(BF16), w_out[i].astype(BF16), lane_row(g_moe[i]),
                             w_router[i].T.astype(F32), b, s, tm_proj)

        gates, pos = _select(aff, cap)
        cnt = jnp.sum((pos >= 0).reshape(b, N_EXPERTS, s // tt_moe, tt_moe), axis=3, dtype=jnp.int32)
        tile_major = lambda a: jnp.transpose(a, (0, 2, 1)).reshape(t // tt_moe, N_EXPERTS)
        moe = _moe(h2, pos.reshape(b * N_EXPERTS, 1, s), gates.reshape(b * N_EXPERTS, 1, s), tile_major(cnt),
                   tile_major(jnp.cumsum(cnt, axis=2) - cnt), w_exp_gate[i].astype(BF16), w_exp_up[i].astype(BF16), w_exp_down[i].astype(BF16),
                   tt_moe, _tile(s // tt_moe, 2), MOE_ROW_BLOCK)

        xt = _ple(xt, moe, p[i].reshape(t, -1), lane_row(g_ple[i]), w_ple_gate[i].astype(BF16),
                  w_ple[i].astype(BF16), lane_row(g_final), i == depth - 1, tm_proj)

    return xt.reshape(b, s, dm)
```

```python
import functools
import math

import jax
import jax.numpy as jnp
from jax import lax
from jax.experimental import pallas as pl
from jax.experimental.pallas import tpu as pltpu

F32 = jnp.float32
BF16 = jnp.bfloat16
HIGHEST = lax.Precision.HIGHEST

EPS = 1e-6
GRID_W = 64
A_Q_HEADS, A_KV_HEADS, A_HEAD_DIM = 4, 2, 64
ROPE_THETA = 10000.0
SSM_HEADS, SSM_HEAD_DIM, SSM_GROUPS, SSM_STATE, SSM_CONV, SSM_CHUNK = 4, 64, 2, 128, 5, 128
DIFF_HEADS, DIFF_QK_DIM, DIFF_V_DIM = 4, 32, 64
REL_BUCKETS, REL_MAX_DIST = 32, 128
SGU_GROUPS, SGU_GROUP_DIM, SGU_CHUNK = 4, 64, 128
N_EXPERTS, EC_CAPACITY = 16, 2
N_BRANCHES, BRANCH_WIDTH = 4, 256

V7X_LANES = 128
V7X_SUBLANES = 8
BF16_SUBLANE_PACK = 16
V7X_VMEM_LIMIT_BYTES = 56 * 1024 * 1024

NEG_BIG = -1e30
LOG2E = math.log2(math.e)
MOE_ROW_BLOCK = 160
ATTN_INNER = 4
ATTN_C_INNER = 7


def _cparams(sem):
    return pltpu.CompilerParams(dimension_semantics=sem, vmem_limit_bytes=V7X_VMEM_LIMIT_BYTES)


def _rms(x, g):
    return x * lax.rsqrt(jnp.mean(x * x, axis=-1, keepdims=True) + EPS) * g


def _dot(a, b):
    return jnp.dot(a, b, preferred_element_type=F32)


def _dot_nt(a, b):
    return lax.dot_general(a, b, (((1,), (1,)), ((), ())), preferred_element_type=F32)


def _dot_tn(a, b):
    return lax.dot_general(a, b, (((0,), (0,)), ((), ())), preferred_element_type=F32)


def _full(shape):
    n = len(shape)
    return pl.BlockSpec(shape, lambda *_: (0,) * n)


def _norm_rope_t(x, g_col, tab):
    xn = x * lax.rsqrt(jnp.mean(x * x, axis=0, keepdims=True) + EPS) * g_col
    qd = A_HEAD_DIM // 4
    cr, sr, cc, sc = (tab[i * qd:(i + 1) * qd] for i in range(4))
    r1, r2, c1, c2 = (xn[i * qd:(i + 1) * qd] for i in range(4))
    return jnp.concatenate([r1 * cr - r2 * sr, r1 * sr + r2 * cr, c1 * cc - c2 * sc, c1 * sc + c2 * cc], axis=0)


def _inproj_kernel(x_ref, gmix_ref, wat_ref, wb_ref, wct_ref, wd_ref, tab_ref, gq_ref, gk_ref, gsgu_ref,
                   qa_ref, ka_ref, va_ref, z_ref, xbc_ref, dt_ref, cq_ref, ck_ref, cv_ref,
                   du_ref, dv_ref):
    hb = _rms(x_ref[...], gmix_ref[...]).astype(BF16)
    tm = hb.shape[0]
    hd = A_HEAD_DIM
    ones = jnp.ones((BF16_SUBLANE_PACK, tm), BF16)

    at = _dot_nt(wat_ref[...], hb)
    tab = tab_ref[...]
    q = [_norm_rope_t(at[h * hd:(h + 1) * hd], gq_ref[...], tab) for h in range(A_Q_HEADS)]
    qa_ref[...] = (jnp.concatenate(q, axis=0) * (hd ** -0.5 * LOG2E)).astype(BF16)
    k0 = A_Q_HEADS * hd
    v0 = k0 + A_KV_HEADS * hd
    for g in range(A_KV_HEADS):
        kt = _norm_rope_t(at[k0 + g * hd:k0 + (g + 1) * hd], gk_ref[...], tab)
        ka_ref[g] = jnp.transpose(kt).astype(BF16)
        va_ref[g, :hd] = at[v0 + g * hd:v0 + (g + 1) * hd].astype(BF16)
        va_ref[g, hd:] = ones

    b = _dot(hb, wb_ref[...])
    dx = SSM_HEADS * SSM_HEAD_DIM
    conv_ch = dx + 2 * SSM_GROUPS * SSM_STATE
    z_ref[...] = b[:, :dx]
    xbc_ref[...] = b[:, dx:dx + conv_ch]
    dt_ref[...] = b[:, dx + conv_ch:]

    ct = _dot_nt(wct_ref[...], hb)
    nqc = DIFF_HEADS * 2 * DIFF_QK_DIM
    cq_ref[...] = (ct[:nqc] * (DIFF_QK_DIM ** -0.5 * LOG2E)).astype(BF16)
    ck = jnp.transpose(ct[nqc:2 * nqc]).astype(BF16)
    for h in range(DIFF_HEADS):
        for c in range(2):
            lo = (2 * h + c) * DIFF_QK_DIM
            ck_ref[h, c] = ck[:, lo:lo + DIFF_QK_DIM]
        cv_ref[h, :DIFF_V_DIM] = ct[2 * nqc + h * DIFF_V_DIM:2 * nqc + (h + 1) * DIFF_V_DIM].astype(BF16)
        cv_ref[h, DIFF_V_DIM:] = ones

    d = jax.nn.gelu(_dot(hb, wd_ref[...]))
    w = SGU_GROUPS * SGU_GROUP_DIM
    du_ref[...] = d[:, :w]
    dv_ref[...] = _rms(d[:, w:], gsgu_ref[...]).astype(BF16)


def _inproj(x, gmix, wat, wb, wct, wd, tab, gq, gk, gsgu, b, s, tm):
    t, dm = x.shape
    nst = s // tm
    row = lambda w: pl.BlockSpec((tm, w), lambda i: (i, 0))
    lanes = lambda *lead: pl.BlockSpec((None,) + lead + (tm,), lambda i: (i // nst,) + (0,) * len(lead) + (i % nst,))
    hd, pk = A_HEAD_DIM, BF16_SUBLANE_PACK
    specs_shapes = [
        (lanes(A_Q_HEADS * hd), (b, A_Q_HEADS * hd, s), BF16),
        (pl.BlockSpec((None, A_KV_HEADS, tm, hd), lambda i: (i // nst, 0, i % nst, 0)), (b, A_KV_HEADS, s, hd), BF16),
        (lanes(A_KV_HEADS, hd + pk), (b, A_KV_HEADS, hd + pk, s), BF16),
        (row(256), (t, 256), F32), (row(768), (t, 768), F32), (row(128), (t, 128), F32),
        (lanes(DIFF_HEADS * 2 * DIFF_QK_DIM), (b, DIFF_HEADS * 2 * DIFF_QK_DIM, s), BF16),
        (pl.BlockSpec((None, DIFF_HEADS, 2, tm, DIFF_QK_DIM), lambda i: (i // nst, 0, 0, i % nst, 0)),
         (b, DIFF_HEADS, 2, s, DIFF_QK_DIM), BF16),
        (lanes(DIFF_HEADS, DIFF_V_DIM + pk), (b, DIFF_HEADS, DIFF_V_DIM + pk, s), BF16),
        (row(256), (t, 256), F32), (row(256), (t, 256), BF16)]
    return pl.pallas_call(
        _inproj_kernel,
        grid=(t // tm,),
        in_specs=[row(dm), _full(gmix.shape), _full(wat.shape), _full(wb.shape), _full(wct.shape),
                  _full(wd.shape), pl.BlockSpec((tab.shape[0], tm), lambda i: (0, i % nst)),
                  _full(gq.shape), _full(gk.shape), _full(gsgu.shape)],
        out_specs=[sp for sp, _, _ in specs_shapes],
        out_shape=[jax.ShapeDtypeStruct(shp, dt) for _, shp, dt in specs_shapes],
        compiler_params=_cparams(("parallel",)),
        name="inproj",
    )(x, gmix, wat, wb, wct, wd, tab, gq, gk, gsgu)


def _osm_tile(s, mt, shift, vt, m_ref, acc_ref, c):
    m_old = m_ref[c]
    if shift is not None:
        mt = mt + shift
    m_new = jnp.maximum(m_old, mt)
    p = jnp.exp2(s - (m_new if shift is None else m_new - shift))
    acc_ref[c] = jnp.exp2(m_old - m_new) * acc_ref[c] + _dot(vt, p.astype(BF16))
    m_ref[c] = m_new


def _osm_init(m_ref, acc_ref):
    m_ref[...] = jnp.full(m_ref.shape, NEG_BIG, F32)
    acc_ref[...] = jnp.zeros(acc_ref.shape, F32)


def _osm_result(acc_ref, c, dv):
    acc = acc_ref[c]
    return acc[:dv] / acc[dv:dv + 1]


def _attn_scores(tk, j, qt_ref, k_of, buf, bias_tile=None):
    s_ref, mx_ref = buf
    off = pl.multiple_of(j * tk, tk)
    for c in range(qt_ref.shape[0]):
        s = _dot(k_of(c, off), qt_ref[c])
        if bias_tile is not None:
            s = s + bias_tile
        s_ref[c] = s
        mx_ref[c] = jnp.max(s, axis=0, keepdims=True)


def _attn_values(tk, j, nmaps, vt_ref, buf, m_ref, acc_ref, shift=None):
    s_ref, mx_ref = buf
    vt = vt_ref[:, pl.ds(pl.multiple_of(j * tk, tk), tk)]
    for c in range(nmaps):
        _osm_tile(s_ref[c], mx_ref[c], shift, vt, m_ref, acc_ref, c)


def _pipelined(n, scores, values, s_refs, inner):
    last = jnp.maximum(n - 1, 0)
    span = 2 * inner
    scores(jnp.minimum(0, last), s_refs[0])

    def double(g):
        scores(jnp.minimum(g + 1, last), s_refs[1])
        values(g, s_refs[0])
        scores(jnp.minimum(g + 2, last), s_refs[0])
        values(g + 1, s_refs[1])

    def unrolled(t, carry):
        for w in range(inner):
            double(span * t + 2 * w)
        return carry

    lax.fori_loop(0, n // span, unrolled, 0)
    base = (n // span) * span

    def rolled(t, carry):
        double(base + 2 * t)
        return carry

    lax.fori_loop(0, (n - base) // 2, rolled, 0)

    @pl.when(n % 2 == 1)
    def _():
        values(n - 1, s_refs[0])


def _attn_a_kernel(tk, inner, qt_ref, k_ref, vt_ref, o_ref, m_ref, acc_ref, sa_ref, sb_ref, mxa_ref, mxb_ref):
    _osm_init(m_ref, acc_ref)
    nq = qt_ref.shape[0]
    k_of = lambda c, off: k_ref[pl.ds(off, tk), :]
    _pipelined(k_ref.shape[0] // tk,
               lambda j, buf: _attn_scores(tk, j, qt_ref, k_of, buf),
               lambda j, buf: _attn_values(tk, j, nq, vt_ref, buf, m_ref, acc_ref),
               ((sa_ref, mxa_ref), (sb_ref, mxb_ref)), inner)
    for c in range(nq):
        o_ref[c] = _osm_result(acc_ref, c, o_ref.shape[1]).astype(o_ref.dtype)


def _attn_a(qt, k, vt, tq, tk, inner):
    b, hq, d, s = qt.shape
    hkv = k.shape[1]
    rep = hq // hkv
    da = vt.shape[2]
    qspec = pl.BlockSpec((None, rep, d, tq), lambda bi, g, i: (bi, g, 0, i))
    return pl.pallas_call(
        functools.partial(_attn_a_kernel, tk, inner),
        grid=(b, hkv, s // tq),
        in_specs=[qspec,
                  pl.BlockSpec((None, None, s, d), lambda bi, g, i: (bi, g, 0, 0)),
                  pl.BlockSpec((None, None, da, s), lambda bi, g, i: (bi, g, 0, 0))],
        out_specs=qspec,
        out_shape=jax.ShapeDtypeStruct(qt.shape, BF16),
        scratch_shapes=[pltpu.VMEM((rep, 1, tq), F32), pltpu.VMEM((rep, da, tq), F32),
                        pltpu.VMEM((rep, tk, tq), F32), pltpu.VMEM((rep, tk, tq), F32),
                        pltpu.VMEM((rep, 1, tq), F32), pltpu.VMEM((rep, 1, tq), F32)],
        compiler_params=_cparams(("parallel", "parallel", "arbitrary")),
        name="attn_a",
    )(qt, k, vt)


def _attn_c_kernel(tk, inner, out_scale, qt_ref, k_ref, vt_ref, gband_ref, cfar_ref, lam_ref, gd_ref, o_ref,
                   m_ref, acc_ref, sa_ref, sb_ref, mxa_ref, mxb_ref, band_ref):
    i = pl.program_id(2)
    tq = qt_ref.shape[2]
    nk = k_ref.shape[1] // tk
    nband = gband_ref.shape[0]
    dv = o_ref.shape[0]
    _osm_init(m_ref, acc_ref)
    k_of = lambda c, off: k_ref[c, pl.ds(off, tk), :]

    j_lo = (tq // tk) * i - 1
    n_left = jnp.clip(j_lo, 0, nk)
    j_hi = jnp.clip(j_lo + nband, 0, nk)
    n_far = n_left + nk - j_hi
    cf = cfar_ref[...]

    def far_tile(t):
        j = jnp.where(t < n_left, t, t - n_left + j_hi)
        return jnp.clip(j, 0, nk - 1), jnp.where(t < n_left, cf[:, 0:1], cf[:, 1:2])

    def far_scores(t, buf):
        _attn_scores(tk, far_tile(t)[0], qt_ref, k_of, buf)

    def far_values(t, buf):
        j, sh = far_tile(t)
        _attn_values(tk, j, 2, vt_ref, buf, m_ref, acc_ref, shift=sh)

    _pipelined(n_far, far_scores, far_values, ((sa_ref, mxa_ref), (sb_ref, mxb_ref)), inner)

    @pl.when(i == 0)
    def _():
        for d in range(nband):
            g = gband_ref[d]
            tile = pltpu.roll(jnp.broadcast_to(g, (tk, g.shape[1])), tq + 1, 1, stride=1, stride_axis=0)
            band_ref[d] = tile[:, :tq]

    d_lo = n_left - j_lo

    def band_scores(t, buf):
        d = jnp.clip(d_lo + t, 0, nband - 1)
        _attn_scores(tk, jnp.clip(j_lo + d, 0, nk - 1), qt_ref, k_of, buf, bias_tile=band_ref[d])

    def band_values(t, buf):
        _attn_values(tk, j_lo + d_lo + t, 2, vt_ref, buf, m_ref, acc_ref)

    _pipelined(j_hi - n_left, band_scores, band_values, ((sa_ref, mxa_ref), (sb_ref, mxb_ref)), 1)

    o = _osm_result(acc_ref, 0, dv) - lam_ref[...] * _osm_result(acc_ref, 1, dv)
    o = o * lax.rsqrt(jnp.mean(o * o, axis=0, keepdims=True) + EPS) * gd_ref[...]
    o_ref[...] = (o * out_scale).astype(o_ref.dtype)


def _attn_c(qt, k, vt, gband, cfar, lam, gd, out_scale, tq, tk, inner):
    b, h, _, dqk, s = qt.shape
    dva = vt.shape[2]
    dv = gd.shape[0]
    nband = gband.shape[1]
    return pl.pallas_call(
        functools.partial(_attn_c_kernel, tk, inner, out_scale),
        grid=(b, h, s // tq),
        in_specs=[pl.BlockSpec((None, None, 2, dqk, tq), lambda bi, hi, i: (bi, hi, 0, 0, i)),
                  pl.BlockSpec((None, None, 2, s, dqk), lambda bi, hi, i: (bi, hi, 0, 0, 0)),
                  pl.BlockSpec((None, None, dva, s), lambda bi, hi, i: (bi, hi, 0, 0)),
                  pl.BlockSpec((None, nband, 1, tq + tk), lambda bi, hi, i: (hi, 0, 0, 0)),
                  pl.BlockSpec((None, 1, 2), lambda bi, hi, i: (hi, 0, 0)),
                  _full(lam.shape), _full(gd.shape)],
        out_specs=pl.BlockSpec((None, None, dv, tq), lambda bi, hi, i: (bi, hi, 0, i)),
        out_shape=jax.ShapeDtypeStruct((b, h, dv, s), BF16),
        scratch_shapes=[pltpu.VMEM((2, 1, tq), F32), pltpu.VMEM((2, dva, tq), F32),
                        pltpu.VMEM((2, tk, tq), F32), pltpu.VMEM((2, tk, tq), F32),
                        pltpu.VMEM((2, 1, tq), F32), pltpu.VMEM((2, 1, tq), F32),
                        pltpu.VMEM((nband, tk, tq), F32)],
        compiler_params=_cparams(("parallel", "parallel", "arbitrary")),
        name="attn_c",
    )(qt, k, vt, gband, cfar, lam, gd)


def _ssd_prep_kernel(nst, xc_ref, xp_ref, xn_ref, dt_ref, cw_ref, cb_ref, dtb_ref, xa_ref, dl_ref, ext_ref):
    i = pl.program_id(0)
    tm = xc_ref.shape[0]
    halo = V7X_SUBLANES
    pos = i % nst
    keep_prev = (pos != 0).astype(F32)
    keep_next = (pos != nst - 1).astype(F32)
    ext_ref[0:halo, :] = xp_ref[...] * keep_prev
    ext_ref[halo:halo + tm, :] = xc_ref[...]
    ext_ref[halo + tm:, :] = xn_ref[...] * keep_next
    pad_l = (SSM_CONV - 1) // 2
    cw = cw_ref[...]
    acc = jnp.zeros(xc_ref.shape, F32) + cb_ref[...]
    for t in range(SSM_CONV):
        acc = acc + ext_ref[halo - pad_l + t:halo - pad_l + t + tm, :] * cw[t:t + 1, :]
    xa_ref[...] = jax.nn.silu(acc)
    dl_ref[...] = jax.nn.softplus(dt_ref[...] + dtb_ref[...])


def _ssd_prep(xbc, dt, cw, cb, dtb, s, tm):
    t, c = xbc.shape
    nst = s // tm
    hb = tm // V7X_SUBLANES
    nhb = t // V7X_SUBLANES
    return pl.pallas_call(
        functools.partial(_ssd_prep_kernel, nst),
        grid=(t // tm,),
        in_specs=[pl.BlockSpec((tm, c), lambda i: (i, 0)),
                  pl.BlockSpec((V7X_SUBLANES, c), lambda i: (jnp.maximum(i * hb - 1, 0), 0)),
                  pl.BlockSpec((V7X_SUBLANES, c), lambda i: (jnp.minimum((i + 1) * hb, nhb - 1), 0)),
                  pl.BlockSpec((tm, dt.shape[1]), lambda i: (i, 0)),
                  _full(cw.shape), _full(cb.shape), _full(dtb.shape)],
        out_specs=[pl.BlockSpec((tm, c), lambda i: (i, 0)),
                   pl.BlockSpec((tm, dt.shape[1]), lambda i: (i, 0))],
        out_shape=[jax.ShapeDtypeStruct((t, c), F32), jax.ShapeDtypeStruct(dt.shape, F32)],
        scratch_shapes=[pltpu.VMEM((tm + 2 * V7X_SUBLANES, c), F32)],
        compiler_params=_cparams(("parallel",)),
        name="ssd_prep",
    )(xbc, xbc, xbc, dt, cw, cb, dtb)


def _ssd_chunk(reverse, xa, dl, alog, st_ref):
    ln = SSM_CHUNK
    row = lax.broadcasted_iota(jnp.int32, (ln, ln), 0)
    col = lax.broadcasted_iota(jnp.int32, (ln, ln), 1)
    tri = (col >= row) if reverse else (col <= row)
    eye = row == col
    last = 0 if reverse else ln - 1
    lane0 = SSM_HEADS if reverse else 0
    a_neg = -jnp.exp(alog)
    acs = jnp.dot(tri.astype(F32), dl * a_neg, precision=HIGHEST, preferred_element_type=F32)
    dx = SSM_HEADS * SSM_HEAD_DIM
    gw = SSM_STATE
    hpg = SSM_HEADS // SSM_GROUPS
    ys = []
    for g in range(SSM_GROUPS):
        bg = xa[:, dx + g * gw:dx + (g + 1) * gw].astype(BF16)
        cg = xa[:, dx + SSM_GROUPS * gw + g * gw:dx + SSM_GROUPS * gw + (g + 1) * gw].astype(BF16)
        cb = _dot_nt(cg, bg)
        for r in range(hpg):
            h = g * hpg + r
            ln_h = lane0 + h
            ac = acs[:, ln_h:ln_h + 1]
            ar = jnp.sum(jnp.where(eye, ac, 0.0), axis=0, keepdims=True)
            dec = jnp.exp(jnp.where(tri, ac - ar, NEG_BIG))
            xdt = xa[:, h * SSM_HEAD_DIM:(h + 1) * SSM_HEAD_DIM] * dl[:, ln_h:ln_h + 1]
            st = st_ref[h]
            y = _dot((cb * dec).astype(BF16), xdt.astype(BF16))
            y = y + _dot(cg, st.astype(BF16)) * jnp.exp(ac)
            a_last = acs[last:last + 1, ln_h:ln_h + 1]
            st_ref[h] = st * jnp.exp(a_last) + _dot_tn(bg, (xdt * jnp.exp(a_last - ac)).astype(BF16))
            ys.append(y)
    return jnp.concatenate(ys, axis=1)


def _ssd_fwd_kernel(cps, xa_ref, dl_ref, alog_ref, y_ref, st_ref):
    @pl.when(pl.program_id(1) == 0)
    def _():
        st_ref[...] = jnp.zeros(st_ref.shape, F32)

    for c in range(cps):
        sl = slice(c * SSM_CHUNK, (c + 1) * SSM_CHUNK)
        y_ref[sl, :] = _ssd_chunk(False, xa_ref[sl, :], dl_ref[sl, :], alog_ref[...], st_ref)


def _ssd_bwd_kernel(cps, xa_ref, dl_ref, alog_ref, yf_ref, z_ref, dsk_ref, g_ref, o_ref, st_ref):
    @pl.when(pl.program_id(1) == 0)
    def _():
        st_ref[...] = jnp.zeros(st_ref.shape, F32)

    dx = SSM_HEADS * SSM_HEAD_DIM
    for c in reversed(range(cps)):
        sl = slice(c * SSM_CHUNK, (c + 1) * SSM_CHUNK)
        xa = xa_ref[sl, :]
        yb = _ssd_chunk(True, xa, dl_ref[sl, :], alog_ref[...], st_ref)
        y = yf_ref[sl, :] + yb + dsk_ref[...] * xa[:, :dx]
        y = y * jax.nn.silu(z_ref[sl, :])
        o_ref[sl, :] = _rms(y, g_ref[...]).astype(o_ref.dtype)


def _ssd(xa, dl, alog, z, dsk, gssm, b, s, cps):
    t, c = xa.shape
    tm = cps * SSM_CHUNK
    nblk = s // tm
    dx = SSM_HEADS * SSM_HEAD_DIM
    st = pltpu.VMEM((SSM_HEADS, SSM_STATE, SSM_HEAD_DIM), F32)
    fw = lambda w: pl.BlockSpec((tm, w), lambda bi, ci: (bi * nblk + ci, 0))
    bw = lambda w: pl.BlockSpec((tm, w), lambda bi, ci: (bi * nblk + nblk - 1 - ci, 0))
    yf = pl.pallas_call(
        functools.partial(_ssd_fwd_kernel, cps),
        grid=(b, nblk),
        in_specs=[fw(c), fw(dl.shape[1]), _full(alog.shape)],
        out_specs=fw(dx),
        out_shape=jax.ShapeDtypeStruct((t, dx), F32),
        scratch_shapes=[st],
        compiler_params=_cparams(("parallel", "arbitrary")),
        name="ssd_fwd",
    )(xa, dl, alog)
    return pl.pallas_call(
        functools.partial(_ssd_bwd_kernel, cps),
        grid=(b, nblk),
        in_specs=[bw(c), bw(dl.shape[1]), _full(alog.shape), bw(dx), bw(dx), _full(dsk.shape),
                  _full(gssm.shape)],
        out_specs=bw(dx),
        out_shape=jax.ShapeDtypeStruct((t, dx), BF16),
        scratch_shapes=[st],
        compiler_params=_cparams(("parallel", "arbitrary")),
        name="ssd_bwd",
    )(xa, dl, alog, yf, z, dsk, gssm)


def _sgu_kernel(nck, u_ref, v_ref, w_ref, bias_ref, o_ref):
    for c in range(nck):
        sl = slice(c * SGU_CHUNK, (c + 1) * SGU_CHUNK)
        v = v_ref[sl, :]
        sv = [_dot(w_ref[g], v[:, g * SGU_GROUP_DIM:(g + 1) * SGU_GROUP_DIM]) for g in range(SGU_GROUPS)]
        o_ref[sl, :] = (u_ref[sl, :] * (jnp.concatenate(sv, axis=1) + bias_ref[...])).astype(o_ref.dtype)


def _sgu(u, v, w, bias, nck):
    t, c = u.shape
    tm = nck * SGU_CHUNK
    return pl.pallas_call(
        functools.partial(_sgu_kernel, nck),
        grid=(t // tm,),
        in_specs=[pl.BlockSpec((tm, c), lambda i: (i, 0)), pl.BlockSpec((tm, c), lambda i: (i, 0)),
                  _full(w.shape), _full(bias.shape)],
        out_specs=pl.BlockSpec((tm, c), lambda i: (i, 0)),
        out_shape=jax.ShapeDtypeStruct((t, c), BF16),
        compiler_params=_cparams(("parallel",)),
        name="sgu",
    )(u, v, w, bias)


def _merge_kernel(x_ref, gmix_ref, oat_ref, ob_ref, oct_ref, od_ref, wg_ref, wbr_ref, wo_ref, gmoe_ref, wrt_ref,
                  o_ref, h_ref, aff_ref):
    x = x_ref[...]
    dm = x.shape[1]
    hb = _rms(x, gmix_ref[...]).astype(BF16)
    merged = jnp.zeros(x.shape, F32)
    for n, (br, transposed) in enumerate(((oat_ref, True), (ob_ref, False), (oct_ref, True), (od_ref, False))):
        gate = jax.nn.sigmoid(_dot(hb, wg_ref[:, n * dm:(n + 1) * dm]))
        wide = _dot_tn(br[...], wbr_ref[n]) if transposed else _dot(br[...], wbr_ref[n])
        merged = merged + gate * wide
    x = x + _dot(merged.astype(BF16), wo_ref[...])
    o_ref[...] = x

    h = _rms(x, gmoe_ref[...])
    h_ref[...] = h.astype(BF16)
    logits = lax.dot_general(wrt_ref[...], h, (((1,), (1,)), ((), ())), precision=HIGHEST,
                             preferred_element_type=F32)
    e = jnp.exp(logits - jnp.max(logits, axis=0, keepdims=True))
    aff_ref[...] = e / jnp.sum(e, axis=0, keepdims=True)


def _merge(x, gmix, oat, ob, oct, od, wg, wbr, wo, gmoe, wrt, b, s, tm):
    t, dm = x.shape
    ne = wrt.shape[0]
    nst = s // tm
    row = lambda w: pl.BlockSpec((tm, w), lambda i: (i, 0))
    lanes = lambda w: pl.BlockSpec((None, w, tm), lambda i: (i // nst, 0, i % nst))
    return pl.pallas_call(
        _merge_kernel,
        grid=(t // tm,),
        in_specs=[row(dm), _full(gmix.shape), lanes(BRANCH_WIDTH), row(BRANCH_WIDTH), lanes(BRANCH_WIDTH),
                  row(BRANCH_WIDTH), _full(wg.shape), _full(wbr.shape), _full(wo.shape), _full(gmoe.shape),
                  _full(wrt.shape)],
        out_specs=[row(dm), row(dm), lanes(ne)],
        out_shape=[jax.ShapeDtypeStruct((t, dm), F32), jax.ShapeDtypeStruct((t, dm), BF16),
                   jax.ShapeDtypeStruct((b, ne, s), F32)],
        compiler_params=_cparams(("parallel",)),
        name="merge",
    )(x, gmix, oat, ob, oct, od, wg, wbr, wo, gmoe, wrt)


def _select_kernel(cap, aff_ref, gate_ref, pos_ref):
    ne, s = aff_ref.shape
    bits = pltpu.bitcast(aff_ref[...], jnp.int32)
    capf = jnp.float32(cap)

    def bisect(i, thr):
        cand = thr | lax.shift_left(jnp.int32(1), 30 - i)
        cnt = jnp.sum((bits >= cand).astype(F32), axis=1, keepdims=True)
        return jnp.where(cnt >= capf, cand, thr)

    thr = lax.fori_loop(0, 31, bisect, jnp.zeros((ne, 1), jnp.int32))
    need = capf - jnp.sum((bits > thr).astype(F32), axis=1, keepdims=True)
    lanes = V7X_LANES
    r = lax.broadcasted_iota(jnp.int32, (lanes, lanes), 0)
    c = lax.broadcasted_iota(jnp.int32, (lanes, lanes), 1)
    before = (r < c).astype(BF16)

    def block(jb, carry):
        n_eq, n_keep = carry
        off = pl.multiple_of(jb * lanes, lanes)
        a = aff_ref[:, pl.ds(off, lanes)]
        bb = pltpu.bitcast(a, jnp.int32)
        eq = bb == thr
        eqf = eq.astype(F32)
        keep = jnp.logical_or(bb > thr, jnp.logical_and(eq, n_eq + _dot(eqf.astype(BF16), before) < need))
        keepf = keep.astype(F32)
        gate_ref[:, pl.ds(off, lanes)] = jnp.where(keep, a, 0.0)
        pos_ref[:, pl.ds(off, lanes)] = jnp.where(keep, n_keep + _dot(keepf.astype(BF16), before), -1.0)
        return n_eq + jnp.sum(eqf, axis=1, keepdims=True), n_keep + jnp.sum(keepf, axis=1, keepdims=True)

    zero = jnp.zeros((ne, 1), F32)
    lax.fori_loop(0, s // lanes, block, (zero, zero))


def _select(aff, cap):
    b, ne, s = aff.shape
    spec = pl.BlockSpec((None, ne, s), lambda i: (i, 0, 0))
    return pl.pallas_call(
        functools.partial(_select_kernel, cap),
        grid=(b,),
        in_specs=[spec],
        out_specs=[spec, spec],
        out_shape=[jax.ShapeDtypeStruct(aff.shape, F32), jax.ShapeDtypeStruct(aff.shape, F32)],
        compiler_params=_cparams(("parallel",)),
        name="moe_select",
    )(aff)


def _moe_kernel(rb, tt, cnt_ref, base_ref, h_ref, pos_ref, gate_ref, wg_ref, wu_ref, wd_ref, o_ref):
    i = pl.program_id(0)
    e = pl.program_id(1)
    nsub = h_ref.shape[0] // tt

    @pl.when(e == 0)
    def _():
        o_ref[...] = jnp.zeros(o_ref.shape, F32)

    for sub in range(nsub):
        rows = slice(sub * tt, (sub + 1) * tt)
        tile = i * nsub + sub
        pos = pos_ref[:, rows] - base_ref[tile, e].astype(F32)
        gate = gate_ref[:, rows]

        def block(bi, carry):
            slot = lax.broadcasted_iota(jnp.int32, (rb, 1), 0).astype(F32) + (bi * rb).astype(F32)
            hit = pos == slot
            onehot = hit.astype(BF16)
            xc = _dot(onehot, h_ref[rows, :]).astype(BF16)
            hid = jax.nn.silu(_dot(xc, wg_ref[...])) * _dot(xc, wu_ref[...])
            y = _dot(hid.astype(BF16), wd_ref[...])
            g = jnp.sum(jnp.where(hit, gate, 0.0), axis=1, keepdims=True)
            o_ref[rows, :] += _dot_tn(onehot, (y * g).astype(BF16))
            return carry

        lax.fori_loop(0, (cnt_ref[tile, e] + rb - 1) // rb, block, 0)


def _moe(h, pos, gates, cnt, base, wg, wu, wd, tt, nsub, rb):
    t, dm = h.shape
    ne, _, dff = wg.shape
    ts = tt * nsub
    nst = pos.shape[2] // ts
    lane_row = pl.BlockSpec((None, 1, ts), lambda i, e, *_: ((i // nst) * ne + e, 0, i % nst))
    return pl.pallas_call(
        functools.partial(_moe_kernel, rb, tt),
        grid_spec=pltpu.PrefetchScalarGridSpec(
            num_scalar_prefetch=2,
            grid=(t // ts, ne),
            in_specs=[pl.BlockSpec((ts, dm), lambda i, e, *_: (i, 0)), lane_row, lane_row,
                      pl.BlockSpec((None, dm, dff), lambda i, e, *_: (e, 0, 0)),
                      pl.BlockSpec((None, dm, dff), lambda i, e, *_: (e, 0, 0)),
                      pl.BlockSpec((None, dff, dm), lambda i, e, *_: (e, 0, 0))],
            out_specs=pl.BlockSpec((ts, dm), lambda i, e, *_: (i, 0))),
        out_shape=jax.ShapeDtypeStruct((t, dm), F32),
        compiler_params=_cparams(("parallel", "arbitrary")),
        name="moe_ffn",
    )(cnt, base, h, pos, gates, wg, wu, wd)


def _ple_kernel(final, x_ref, moe_ref, p_ref, g_ref, wpg_ref, wp_ref, gf_ref, o_ref):
    x = x_ref[...] + moe_ref[...]
    gate = jax.nn.sigmoid(_dot(_rms(x, g_ref[...]).astype(BF16), wpg_ref[...]))
    y = x + gate * _dot(p_ref[...].astype(BF16), wp_ref[...])
    o_ref[...] = _rms(y, gf_ref[...]) if final else y


def _ple(x, moe, p, g, wpg, wp, gf, final, tm):
    t, dm = x.shape
    return pl.pallas_call(
        functools.partial(_ple_kernel, final),
        grid=(t // tm,),
        in_specs=[pl.BlockSpec((tm, dm), lambda i: (i, 0)), pl.BlockSpec((tm, dm), lambda i: (i, 0)),
                  pl.BlockSpec((tm, p.shape[1]), lambda i: (i, 0)),
                  _full(g.shape), _full(wpg.shape), _full(wp.shape), _full(gf.shape)],
        out_specs=pl.BlockSpec((tm, dm), lambda i: (i, 0)),
        out_shape=jax.ShapeDtypeStruct((t, dm), F32),
        compiler_params=_cparams(("parallel",)),
        name="ple",
    )(x, moe, p, g, wpg, wp, gf)


def _rope_table(s):
    half = A_HEAD_DIM // 2
    pos = jnp.arange(s, dtype=jnp.int32)
    freqs = ROPE_THETA ** (-jnp.arange(0, half, 2, dtype=F32) / half)
    ang_r = freqs[:, None] * (pos // GRID_W).astype(F32)[None, :]
    ang_c = freqs[:, None] * (pos % GRID_W).astype(F32)[None, :]
    return jnp.concatenate([jnp.cos(ang_r), jnp.sin(ang_r), jnp.cos(ang_c), jnp.sin(ang_c)], axis=0)


def _t5_bucket(rel):
    nb = REL_BUCKETS // 2
    max_exact = nb // 2
    ret = jnp.where(rel > 0, nb, 0)
    r = jnp.abs(rel)
    rf = jnp.maximum(r, 1).astype(F32)
    large = max_exact + (jnp.log(rf / max_exact) / math.log(REL_MAX_DIST / max_exact)
                         * (nb - max_exact)).astype(jnp.int32)
    large = jnp.minimum(large, nb - 1)
    return ret + jnp.where(r < max_exact, r, large)


def _bias_tables(rel_bias, tq, tk):
    assert tk >= REL_MAX_DIST and tq % tk == 0
    rep = tq // tk
    offs = jnp.arange(-1, rep + 1, dtype=jnp.int32) * tk
    u = jnp.arange(tq + tk, dtype=jnp.int32)
    rel = offs[:, None] + (tk - 1) - u[None, :]
    gband = jnp.transpose(rel_bias[_t5_bucket(rel)], (2, 0, 1))[:, :, None, :].astype(F32) * LOG2E
    far = jnp.array([-REL_MAX_DIST, REL_MAX_DIST], dtype=jnp.int32)
    cfar = jnp.transpose(rel_bias[_t5_bucket(far)], (1, 0))[:, None, :].astype(F32) * LOG2E
    return gband, cfar


def _tile(n, pref):
    t = min(n, pref)
    assert n % t == 0
    return t


def kernel(x, p, rel_bias, g_mix, w_in, g_qnorm, g_knorm, conv_w, conv_b, dt_bias_f, dt_bias_b, a_log_f, a_log_b, d_skip, g_ssm, lambda_q1, lambda_k1, lambda_q2, lambda_k2, g_diff, g_sgu, w_spatial, b_spatial, w_branch, w_branch_gate, w_out, g_moe, w_router, w_exp_gate, w_exp_up, w_exp_down, g_ple, w_ple_gate, w_ple, g_final):
    b, s, dm = x.shape
    depth = w_in.shape[0]
    t = b * s
    cap = EC_CAPACITY * s // N_EXPERTS

    tm_proj = _tile(s, 512)
    tm_prep = _tile(s, 1024)
    tq_a, tk_a = _tile(s, 512), _tile(s, 1024)
    tq_c, tk_c = _tile(s, 512), _tile(s, 512)
    cps = _tile(s // SSM_CHUNK, 4)
    nck = _tile(s // SGU_CHUNK, 8)
    tt_moe = _tile(s, 1024)

    rope_tab = _rope_table(s)
    gband, cfar = _bias_tables(rel_bias, tq_c, tk_c)
    hd = A_HEAD_DIM

    sizes = (A_Q_HEADS * hd, A_KV_HEADS * hd, A_KV_HEADS * hd,
             SSM_HEADS * SSM_HEAD_DIM, SSM_HEADS * SSM_HEAD_DIM, SSM_GROUPS * SSM_STATE,
             SSM_GROUPS * SSM_STATE, SSM_HEADS, SSM_HEADS,
             DIFF_HEADS * 2 * DIFF_QK_DIM, DIFF_HEADS * 2 * DIFF_QK_DIM, DIFF_HEADS * DIFF_V_DIM,
             2 * SGU_GROUPS * SGU_GROUP_DIM)
    off = [0]
    for sz in sizes:
        off.append(off[-1] + sz)
    o_a0, o_b0, o_dt0, o_c0, o_d0, o_end = off[0], off[3], off[7], off[9], off[12], off[13]
    dt_pad = V7X_LANES - 2 * SSM_HEADS

    def lane_row(v, width=None):
        v = v.astype(F32).reshape(1, -1)
        if width is not None and v.shape[1] < width:
            v = jnp.pad(v, ((0, 0), (0, width - v.shape[1])))
        return v

    xt = x.reshape(t, dm)
    for i in range(depth):
        wi = w_in[i]
        wat = wi[:, o_a0:o_b0].T.astype(BF16)
        wb = jnp.pad(wi[:, o_b0:o_c0], ((0, 0), (0, dt_pad))).astype(BF16)
        wct = wi[:, o_c0:o_d0].T.astype(BF16)
        wd = wi[:, o_d0:o_end].astype(BF16)
        gmix = lane_row(g_mix[i])
        col = lambda v: v.astype(F32).reshape(-1, 1)
        (qt, kh, vt, z, xbc, dt, cqt, ckh, cvt, du, dv) = _inproj(
            xt, gmix, wat, wb, wct, wd, rope_tab, col(g_qnorm[i]), col(g_knorm[i]), lane_row(g_sgu[i]),
            b, s, tm_proj)

        o_at = _attn_a(qt.reshape(b, A_Q_HEADS, hd, s), kh, vt, tq_a, tk_a, ATTN_INNER).reshape(b, -1, s)

        xa, dl = _ssd_prep(xbc, dt, conv_w[i].astype(F32), lane_row(conv_b[i]),
                           lane_row(jnp.concatenate([dt_bias_f[i], dt_bias_b[i]]), V7X_LANES), s, tm_prep)
        o_b = _ssd(xa, dl, lane_row(jnp.concatenate([a_log_f[i], a_log_b[i]]), V7X_LANES), z,
                   lane_row(jnp.repeat(d_skip[i], SSM_HEAD_DIM)), lane_row(g_ssm[i]), b, s, cps)

        lam_init = 0.8 - 0.6 * math.exp(-0.3 * i)
        lam = (jnp.exp(jnp.sum(lambda_q1[i].astype(F32) * lambda_k1[i].astype(F32)))
               - jnp.exp(jnp.sum(lambda_q2[i].astype(F32) * lambda_k2[i].astype(F32))) + lam_init).reshape(1, 1)
        o_ct = _attn_c(cqt.reshape(b, DIFF_HEADS, 2, DIFF_QK_DIM, s), ckh, cvt, gband, cfar, lam, col(g_diff[i]),
                       1.0 - lam_init, tq_c, tk_c, ATTN_C_INNER).reshape(b, -1, s)

        sgu_bias = jnp.repeat(b_spatial[i].T.astype(F32), SGU_GROUP_DIM, axis=1)
        o_d = _sgu(du, dv, w_spatial[i].astype(BF16), sgu_bias, nck)

        xt, h2, aff = _merge(xt, gmix, o_at, o_b, o_ct, o_d, w_branch_gate[i].astype(BF16),
                             w_branch[i].astype(BF16), w_out[i].astype(BF16), lane_row(g_moe[i]),
                             w_router[i].T.astype(F32), b, s, tm_proj)

        gates, pos = _select(aff, cap)
        cnt = jnp.sum((pos >= 0).reshape(b, N_EXPERTS, s // tt_moe, tt_moe), axis=3, dtype=jnp.int32)
        tile_major = lambda a: jnp.transpose(a, (0, 2, 1)).reshape(t // tt_moe, N_EXPERTS)
        moe = _moe(h2, pos.reshape(b * N_EXPERTS, 1, s), gates.reshape(b * N_EXPERTS, 1, s), tile_major(cnt),
                   tile_major(jnp.cumsum(cnt, axis=2) - cnt), w_exp_gate[i].astype(BF16), w_exp_up[i].astype(BF16), w_exp_down[i].astype(BF16),
                   tt_moe, _tile(s // tt_moe, 2), MOE_ROW_BLOCK)

        xt = _ple(xt, moe, p[i].reshape(t, -1), lane_row(g_ple[i]), w_ple_gate[i].astype(BF16),
                  w_ple[i].astype(BF16), lane_row(g_final), i == depth - 1, tm_proj)

    return xt.reshape(b, s, dm)
```

```python
import functools
import math

import jax
import jax.numpy as jnp
from jax import lax
from jax.experimental import pallas as pl
from jax.experimental.pallas import tpu as pltpu

F32 = jnp.float32
BF16 = jnp.bfloat16
HIGHEST = lax.Precision.HIGHEST

EPS = 1e-6
GRID_W = 64
A_Q_HEADS, A_KV_HEADS, A_HEAD_DIM = 4, 2, 64
ROPE_THETA = 10000.0
SSM_HEADS, SSM_HEAD_DIM, SSM_GROUPS, SSM_STATE, SSM_CONV, SSM_CHUNK = 4, 64, 2, 128, 5, 128
DIFF_HEADS, DIFF_QK_DIM, DIFF_V_DIM = 4, 32, 64
REL_BUCKETS, REL_MAX_DIST = 32, 128
SGU_GROUPS, SGU_GROUP_DIM, SGU_CHUNK = 4, 64, 128
N_EXPERTS, EC_CAPACITY = 16, 2
N_BRANCHES, BRANCH_WIDTH = 4, 256

V7X_LANES = 128
V7X_SUBLANES = 8
BF16_SUBLANE_PACK = 16
V7X_VMEM_LIMIT_BYTES = 56 * 1024 * 1024

NEG_BIG = -1e30
LOG2E = math.log2(math.e)
MOE_ROW_BLOCK = 160
ATTN_A_RING, ATTN_A_INNER = 3, 2
ATTN_C_RING, ATTN_C_INNER = 2, 7


def _cparams(sem):
    return pltpu.CompilerParams(dimension_semantics=sem, vmem_limit_bytes=V7X_VMEM_LIMIT_BYTES)


def _rms(x, g):
    return x * lax.rsqrt(jnp.mean(x * x, axis=-1, keepdims=True) + EPS) * g


def _dot(a, b):
    return jnp.dot(a, b, preferred_element_type=F32)


def _dot_nt(a, b):
    return lax.dot_general(a, b, (((1,), (1,)), ((), ())), preferred_element_type=F32)


def _dot_tn(a, b):
    return lax.dot_general(a, b, (((0,), (0,)), ((), ())), preferred_element_type=F32)


def _full(shape):
    n = len(shape)
    return pl.BlockSpec(shape, lambda *_: (0,) * n)


def _norm_rope_t(x, g_col, tab):
    xn = x * lax.rsqrt(jnp.mean(x * x, axis=0, keepdims=True) + EPS) * g_col
    qd = A_HEAD_DIM // 4
    cr, sr, cc, sc = (tab[i * qd:(i + 1) * qd] for i in range(4))
    r1, r2, c1, c2 = (xn[i * qd:(i + 1) * qd] for i in range(4))
    return jnp.concatenate([r1 * cr - r2 * sr, r1 * sr + r2 * cr, c1 * cc - c2 * sc, c1 * sc + c2 * cc], axis=0)


def _inproj_kernel(x_ref, gmix_ref, wat_ref, wb_ref, wct_ref, wd_ref, tab_ref, gq_ref, gk_ref, gsgu_ref,
                   qa_ref, ka_ref, va_ref, z_ref, xbc_ref, dt_ref, cq_ref, ck_ref, cv_ref,
                   du_ref, dv_ref):
    hb = _rms(x_ref[...], gmix_ref[...]).astype(BF16)
    tm = hb.shape[0]
    hd = A_HEAD_DIM
    ones = jnp.ones((BF16_SUBLANE_PACK, tm), BF16)

    at = _dot_nt(wat_ref[...], hb)
    tab = tab_ref[...]
    q = [_norm_rope_t(at[h * hd:(h + 1) * hd], gq_ref[...], tab) for h in range(A_Q_HEADS)]
    qa_ref[...] = (jnp.concatenate(q, axis=0) * (hd ** -0.5 * LOG2E)).astype(BF16)
    k0 = A_Q_HEADS * hd
    v0 = k0 + A_KV_HEADS * hd
    for g in range(A_KV_HEADS):
        kt = _norm_rope_t(at[k0 + g * hd:k0 + (g + 1) * hd], gk_ref[...], tab)
        ka_ref[g] = jnp.transpose(kt).astype(BF16)
        va_ref[g, :hd] = at[v0 + g * hd:v0 + (g + 1) * hd].astype(BF16)
        va_ref[g, hd:] = ones

    b = _dot(hb, wb_ref[...])
    dx = SSM_HEADS * SSM_HEAD_DIM
    conv_ch = dx + 2 * SSM_GROUPS * SSM_STATE
    z_ref[...] = b[:, :dx]
    xbc_ref[...] = b[:, dx:dx + conv_ch]
    dt_ref[...] = b[:, dx + conv_ch:]

    ct = _dot_nt(wct_ref[...], hb)
    nqc = DIFF_HEADS * 2 * DIFF_QK_DIM
    cq_ref[...] = (ct[:nqc] * (DIFF_QK_DIM ** -0.5 * LOG2E)).astype(BF16)
    ck = jnp.transpose(ct[nqc:2 * nqc]).astype(BF16)
    for h in range(DIFF_HEADS):
        for c in range(2):
            lo = (2 * h + c) * DIFF_QK_DIM
            ck_ref[h, c] = ck[:, lo:lo + DIFF_QK_DIM]
        cv_ref[h, :DIFF_V_DIM] = ct[2 * nqc + h * DIFF_V_DIM:2 * nqc + (h + 1) * DIFF_V_DIM].astype(BF16)
        cv_ref[h, DIFF_V_DIM:] = ones

    d = jax.nn.gelu(_dot(hb, wd_ref[...]))
    w = SGU_GROUPS * SGU_GROUP_DIM
    du_ref[...] = d[:, :w]
    dv_ref[...] = _rms(d[:, w:], gsgu_ref[...]).astype(BF16)


def _inproj(x, gmix, wat, wb, wct, wd, tab, gq, gk, gsgu, b, s, tm):
    t, dm = x.shape
    nst = s // tm
    row = lambda w: pl.BlockSpec((tm, w), lambda i: (i, 0))
    lanes = lambda *lead: pl.BlockSpec((None,) + lead + (tm,), lambda i: (i // nst,) + (0,) * len(lead) + (i % nst,))
    hd, pk = A_HEAD_DIM, BF16_SUBLANE_PACK
    specs_shapes = [
        (lanes(A_Q_HEADS * hd), (b, A_Q_HEADS * hd, s), BF16),
        (pl.BlockSpec((None, A_KV_HEADS, tm, hd), lambda i: (i // nst, 0, i % nst, 0)), (b, A_KV_HEADS, s, hd), BF16),
        (lanes(A_KV_HEADS, hd + pk), (b, A_KV_HEADS, hd + pk, s), BF16),
        (row(256), (t, 256), F32), (row(768), (t, 768), F32), (row(128), (t, 128), F32),
        (lanes(DIFF_HEADS * 2 * DIFF_QK_DIM), (b, DIFF_HEADS * 2 * DIFF_QK_DIM, s), BF16),
        (pl.BlockSpec((None, DIFF_HEADS, 2, tm, DIFF_QK_DIM), lambda i: (i // nst, 0, 0, i % nst, 0)),
         (b, DIFF_HEADS, 2, s, DIFF_QK_DIM), BF16),
        (lanes(DIFF_HEADS, DIFF_V_DIM + pk), (b, DIFF_HEADS, DIFF_V_DIM + pk, s), BF16),
        (row(256), (t, 256), F32), (row(256), (t, 256), BF16)]
    return pl.pallas_call(
        _inproj_kernel,
        grid=(t // tm,),
        in_specs=[row(dm), _full(gmix.shape), _full(wat.shape), _full(wb.shape), _full(wct.shape),
                  _full(wd.shape), pl.BlockSpec((tab.shape[0], tm), lambda i: (0, i % nst)),
                  _full(gq.shape), _full(gk.shape), _full(gsgu.shape)],
        out_specs=[sp for sp, _, _ in specs_shapes],
        out_shape=[jax.ShapeDtypeStruct(shp, dt) for _, shp, dt in specs_shapes],
        compiler_params=_cparams(("parallel",)),
        name="inproj",
    )(x, gmix, wat, wb, wct, wd, tab, gq, gk, gsgu)


def _osm_tile(s, mt, shift, vt, m_ref, acc_ref, c):
    m_old = m_ref[c]
    if shift is not None:
        mt = mt + shift
    m_new = jnp.maximum(m_old, mt)
    p = jnp.exp2(s - (m_new if shift is None else m_new - shift))
    acc_ref[c] = jnp.exp2(m_old - m_new) * acc_ref[c] + _dot(vt, p.astype(BF16))
    m_ref[c] = m_new


def _osm_init(m_ref, acc_ref):
    m_ref[...] = jnp.full(m_ref.shape, NEG_BIG, F32)
    acc_ref[...] = jnp.zeros(acc_ref.shape, F32)


def _osm_result(acc_ref, c, dv):
    acc = acc_ref[c]
    return acc[:dv] / acc[dv:dv + 1]


def _attn_scores(tk, j, qt_ref, k_of, buf, bias_tile=None):
    s_ref, mx_ref = buf
    off = pl.multiple_of(j * tk, tk)
    for c in range(qt_ref.shape[0]):
        s = _dot(k_of(c, off), qt_ref[c])
        if bias_tile is not None:
            s = s + bias_tile
        s_ref[c] = s
        mx_ref[c] = jnp.max(s, axis=0, keepdims=True)


def _attn_values(tk, j, nmaps, vt_ref, buf, m_ref, acc_ref, shift=None):
    s_ref, mx_ref = buf
    vt = vt_ref[:, pl.ds(pl.multiple_of(j * tk, tk), tk)]
    for c in range(nmaps):
        _osm_tile(s_ref[c], mx_ref[c], shift, vt, m_ref, acc_ref, c)


def _pipelined(n, scores, values, bufs, inner):
    nb = len(bufs)
    ahead = nb - 1
    last = jnp.maximum(n - 1, 0)
    for r in range(ahead):
        scores(jnp.minimum(r, last), bufs[r])

    def ring(g):
        for r in range(nb):
            scores(jnp.minimum(g + r + ahead, last), bufs[(r + ahead) % nb])
            values(g + r, bufs[r])

    span = nb * inner

    def unrolled(t, carry):
        for w in range(inner):
            ring(span * t + nb * w)
        return carry

    lax.fori_loop(0, n // span, unrolled, 0)
    base = (n // span) * span

    def rolled(t, carry):
        ring(base + nb * t)
        return carry

    lax.fori_loop(0, (n - base) // nb, rolled, 0)
    done = (n // nb) * nb
    for r in range(ahead):

        @pl.when(n - done > r)
        def _():
            values(done + r, bufs[r])


def _ring_scratch(nring, nmaps, tk, tq):
    return [pltpu.VMEM(shape, F32) for _ in range(nring) for shape in ((nmaps, tk, tq), (nmaps, 1, tq))]


def _ring_bufs(refs):
    return tuple((refs[2 * r], refs[2 * r + 1]) for r in range(len(refs) // 2))


def _attn_a_kernel(tk, inner, qt_ref, k_ref, vt_ref, o_ref, m_ref, acc_ref, *ring_refs):
    _osm_init(m_ref, acc_ref)
    nq = qt_ref.shape[0]
    k_of = lambda c, off: k_ref[pl.ds(off, tk), :]
    _pipelined(k_ref.shape[0] // tk,
               lambda j, buf: _attn_scores(tk, j, qt_ref, k_of, buf),
               lambda j, buf: _attn_values(tk, j, nq, vt_ref, buf, m_ref, acc_ref),
               _ring_bufs(ring_refs), inner)
    for c in range(nq):
        o_ref[c] = _osm_result(acc_ref, c, o_ref.shape[1]).astype(o_ref.dtype)


def _attn_a(qt, k, vt, tq, tk, inner):
    b, hq, d, s = qt.shape
    hkv = k.shape[1]
    rep = hq // hkv
    da = vt.shape[2]
    qspec = pl.BlockSpec((None, rep, d, tq), lambda bi, g, i: (bi, g, 0, i))
    return pl.pallas_call(
        functools.partial(_attn_a_kernel, tk, inner),
        grid=(b, hkv, s // tq),
        in_specs=[qspec,
                  pl.BlockSpec((None, None, s, d), lambda bi, g, i: (bi, g, 0, 0)),
                  pl.BlockSpec((None, None, da, s), lambda bi, g, i: (bi, g, 0, 0))],
        out_specs=qspec,
        out_shape=jax.ShapeDtypeStruct(qt.shape, BF16),
        scratch_shapes=[pltpu.VMEM((rep, 1, tq), F32), pltpu.VMEM((rep, da, tq), F32)]
        + _ring_scratch(ATTN_A_RING, rep, tk, tq),
        compiler_params=_cparams(("parallel", "parallel", "arbitrary")),
        name="attn_a",
    )(qt, k, vt)


def _attn_c_kernel(tk, inner, out_scale, qt_ref, k_ref, vt_ref, gband_ref, cfar_ref, lam_ref, gd_ref, o_ref,
                   m_ref, acc_ref, *ring_refs):
    i = pl.program_id(2)
    tq = qt_ref.shape[2]
    nk = k_ref.shape[1] // tk
    nband = gband_ref.shape[0]
    dv = o_ref.shape[0]
    _osm_init(m_ref, acc_ref)
    k_of = lambda c, off: k_ref[c, pl.ds(off, tk), :]

    j_lo = (tq // tk) * i - 1
    n_left = jnp.clip(j_lo, 0, nk)
    j_hi = jnp.clip(j_lo + nband, 0, nk)
    n_far = n_left + nk - j_hi
    cf = cfar_ref[...]

    def far_tile(t):
        j = jnp.where(t < n_left, t, t - n_left + j_hi)
        return jnp.clip(j, 0, nk - 1), jnp.where(t < n_left, cf[:, 0:1], cf[:, 1:2])

    def far_scores(t, buf):
        _attn_scores(tk, far_tile(t)[0], qt_ref, k_of, buf)

    def far_values(t, buf):
        j, sh = far_tile(t)
        _attn_values(tk, j, 2, vt_ref, buf, m_ref, acc_ref, shift=sh)

    bufs = _ring_bufs(ring_refs)
    _pipelined(n_far, far_scores, far_values, bufs, inner)

    d_lo = n_left - j_lo

    def band_scores(t, buf):
        d = jnp.clip(d_lo + t, 0, nband - 1)
        g = gband_ref[d]
        tile = pltpu.roll(jnp.broadcast_to(g, (tk, g.shape[1])), tq + 1, 1, stride=1, stride_axis=0)
        _attn_scores(tk, jnp.clip(j_lo + d, 0, nk - 1), qt_ref, k_of, buf, bias_tile=tile[:, :tq])

    def band_values(t, buf):
        _attn_values(tk, j_lo + d_lo + t, 2, vt_ref, buf, m_ref, acc_ref)

    _pipelined(j_hi - n_left, band_scores, band_values, bufs, 1)

    o = _osm_result(acc_ref, 0, dv) - lam_ref[...] * _osm_result(acc_ref, 1, dv)
    o = o * lax.rsqrt(jnp.mean(o * o, axis=0, keepdims=True) + EPS) * gd_ref[...]
    o_ref[...] = (o * out_scale).astype(o_ref.dtype)


def _attn_c(qt, k, vt, gband, cfar, lam, gd, out_scale, tq, tk, inner):
    b, h, _, dqk, s = qt.shape
    dva = vt.shape[2]
    dv = gd.shape[0]
    nband = gband.shape[1]
    return pl.pallas_call(
        functools.partial(_attn_c_kernel, tk, inner, out_scale),
        grid=(b, h, s // tq),
        in_specs=[pl.BlockSpec((None, None, 2, dqk, tq), lambda bi, hi, i: (bi, hi, 0, 0, i)),
                  pl.BlockSpec((None, None, 2, s, dqk), lambda bi, hi, i: (bi, hi, 0, 0, 0)),
                  pl.BlockSpec((None, None, dva, s), lambda bi, hi, i: (bi, hi, 0, 0)),
                  pl.BlockSpec((None, nband, 1, tq + tk), lambda bi, hi, i: (hi, 0, 0, 0)),
                  pl.BlockSpec((None, 1, 2), lambda bi, hi, i: (hi, 0, 0)),
                  _full(lam.shape), _full(gd.shape)],
        out_specs=pl.BlockSpec((None, None, dv, tq), lambda bi, hi, i: (bi, hi, 0, i)),
        out_shape=jax.ShapeDtypeStruct((b, h, dv, s), BF16),
        scratch_shapes=[pltpu.VMEM((2, 1, tq), F32), pltpu.VMEM((2, dva, tq), F32)]
        + _ring_scratch(ATTN_C_RING, 2, tk, tq),
        compiler_params=_cparams(("parallel", "parallel", "arbitrary")),
        name="attn_c",
    )(qt, k, vt, gband, cfar, lam, gd)


def _ssd_prep_kernel(nst, xc_ref, xp_ref, xn_ref, dt_ref, cw_ref, cb_ref, dtb_ref, xa_ref, dl_ref, ext_ref):
    i = pl.program_id(0)
    tm = xc_ref.shape[0]
    halo = V7X_SUBLANES
    pos = i % nst
    keep_prev = (pos != 0).astype(F32)
    keep_next = (pos != nst - 1).astype(F32)
    ext_ref[0:halo, :] = xp_ref[...] * keep_prev
    ext_ref[halo:halo + tm, :] = xc_ref[...]
    ext_ref[halo + tm:, :] = xn_ref[...] * keep_next
    pad_l = (SSM_CONV - 1) // 2
    cw = cw_ref[...]
    acc = jnp.zeros(xc_ref.shape, F32) + cb_ref[...]
    for t in range(SSM_CONV):
        acc = acc + ext_ref[halo - pad_l + t:halo - pad_l + t + tm, :] * cw[t:t + 1, :]
    xa_ref[...] = jax.nn.silu(acc)
    dl_ref[...] = jax.nn.softplus(dt_ref[...] + dtb_ref[...])


def _ssd_prep(xbc, dt, cw, cb, dtb, s, tm):
    t, c = xbc.shape
    nst = s // tm
    hb = tm // V7X_SUBLANES
    nhb = t // V7X_SUBLANES
    return pl.pallas_call(
        functools.partial(_ssd_prep_kernel, nst),
        grid=(t // tm,),
        in_specs=[pl.BlockSpec((tm, c), lambda i: (i, 0)),
                  pl.BlockSpec((V7X_SUBLANES, c), lambda i: (jnp.maximum(i * hb - 1, 0), 0)),
                  pl.BlockSpec((V7X_SUBLANES, c), lambda i: (jnp.minimum((i + 1) * hb, nhb - 1), 0)),
                  pl.BlockSpec((tm, dt.shape[1]), lambda i: (i, 0)),
                  _full(cw.shape), _full(cb.shape), _full(dtb.shape)],
        out_specs=[pl.BlockSpec((tm, c), lambda i: (i, 0)),
                   pl.BlockSpec((tm, dt.shape[1]), lambda i: (i, 0))],
        out_shape=[jax.ShapeDtypeStruct((t, c), F32), jax.ShapeDtypeStruct(dt.shape, F32)],
        scratch_shapes=[pltpu.VMEM((tm + 2 * V7X_SUBLANES, c), F32)],
        compiler_params=_cparams(("parallel",)),
        name="ssd_prep",
    )(xbc, xbc, xbc, dt, cw, cb, dtb)


def _ssd_chunk(reverse, xa, dl, alog, st_ref):
    ln = SSM_CHUNK
    row = lax.broadcasted_iota(jnp.int32, (ln, ln), 0)
    col = lax.broadcasted_iota(jnp.int32, (ln, ln), 1)
    tri = (col >= row) if reverse else (col <= row)
    eye = row == col
    last = 0 if reverse else ln - 1
    lane0 = SSM_HEADS if reverse else 0
    a_neg = -jnp.exp(alog)
    acs = jnp.dot(tri.astype(F32), dl * a_neg, precision=HIGHEST, preferred_element_type=F32)
    dx = SSM_HEADS * SSM_HEAD_DIM
    gw = SSM_STATE
    hpg = SSM_HEADS // SSM_GROUPS
    ys = []
    for g in range(SSM_GROUPS):
        bg = xa[:, dx + g * gw:dx + (g + 1) * gw].astype(BF16)
        cg = xa[:, dx + SSM_GROUPS * gw + g * gw:dx + SSM_GROUPS * gw + (g + 1) * gw].astype(BF16)
        cb = _dot_nt(cg, bg)
        for r in range(hpg):
            h = g * hpg + r
            ln_h = lane0 + h
            ac = acs[:, ln_h:ln_h + 1]
            ar = jnp.sum(jnp.where(eye, ac, 0.0), axis=0, keepdims=True)
            dec = jnp.exp(jnp.where(tri, ac - ar, NEG_BIG))
            xdt = xa[:, h * SSM_HEAD_DIM:(h + 1) * SSM_HEAD_DIM] * dl[:, ln_h:ln_h + 1]
            st = st_ref[h]
            y = _dot((cb * dec).astype(BF16), xdt.astype(BF16))
            y = y + _dot(cg, st.astype(BF16)) * jnp.exp(ac)
            a_last = acs[last:last + 1, ln_h:ln_h + 1]
            st_ref[h] = st * jnp.exp(a_last) + _dot_tn(bg, (xdt * jnp.exp(a_last - ac)).astype(BF16))
            ys.append(y)
    return jnp.concatenate(ys, axis=1)


def _ssd_fwd_kernel(cps, xa_ref, dl_ref, alog_ref, y_ref, st_ref):
    @pl.when(pl.program_id(1) == 0)
    def _():
        st_ref[...] = jnp.zeros(st_ref.shape, F32)

    for c in range(cps):
        sl = slice(c * SSM_CHUNK, (c + 1) * SSM_CHUNK)
        y_ref[sl, :] = _ssd_chunk(False, xa_ref[sl, :], dl_ref[sl, :], alog_ref[...], st_ref)


def _ssd_bwd_kernel(cps, xa_ref, dl_ref, alog_ref, yf_ref, z_ref, dsk_ref, g_ref, o_ref, st_ref):
    @pl.when(pl.program_id(1) == 0)
    def _():
        st_ref[...] = jnp.zeros(st_ref.shape, F32)

    dx = SSM_HEADS * SSM_HEAD_DIM
    for c in reversed(range(cps)):
        sl = slice(c * SSM_CHUNK, (c + 1) * SSM_CHUNK)
        xa = xa_ref[sl, :]
        yb = _ssd_chunk(True, xa, dl_ref[sl, :], alog_ref[...], st_ref)
        y = yf_ref[sl, :] + yb + dsk_ref[...] * xa[:, :dx]
        y = y * jax.nn.silu(z_ref[sl, :])
        o_ref[sl, :] = _rms(y, g_ref[...]).astype(o_ref.dtype)


def _ssd(xa, dl, alog, z, dsk, gssm, b, s, cps):
    t, c = xa.shape
    tm = cps * SSM_CHUNK
    nblk = s // tm
    dx = SSM_HEADS * SSM_HEAD_DIM
    st = pltpu.VMEM((SSM_HEADS, SSM_STATE, SSM_HEAD_DIM), F32)
    fw = lambda w: pl.BlockSpec((tm, w), lambda bi, ci: (bi * nblk + ci, 0))
    bw = lambda w: pl.BlockSpec((tm, w), lambda bi, ci: (bi * nblk + nblk - 1 - ci, 0))
    yf = pl.pallas_call(
        functools.partial(_ssd_fwd_kernel, cps),
        grid=(b, nblk),
        in_specs=[fw(c), fw(dl.shape[1]), _full(alog.shape)],
        out_specs=fw(dx),
        out_shape=jax.ShapeDtypeStruct((t, dx), F32),
        scratch_shapes=[st],
        compiler_params=_cparams(("parallel", "arbitrary")),
        name="ssd_fwd",
    )(xa, dl, alog)
    return pl.pallas_call(
        functools.partial(_ssd_bwd_kernel, cps),
        grid=(b, nblk),
        in_specs=[bw(c), bw(dl.shape[1]), _full(alog.shape), bw(dx), bw(dx), _full(dsk.shape),
                  _full(gssm.shape)],
        out_specs=bw(dx),
        out_shape=jax.ShapeDtypeStruct((t, dx), BF16),
        scratch_shapes=[st],
        compiler_params=_cparams(("parallel", "arbitrary")),
        name="ssd_bwd",
    )(xa, dl, alog, yf, z, dsk, gssm)


def _sgu_kernel(nck, u_ref, v_ref, w_ref, bias_ref, o_ref):
    for c in range(nck):
        sl = slice(c * SGU_CHUNK, (c + 1) * SGU_CHUNK)
        v = v_ref[sl, :]
        sv = [_dot(w_ref[g], v[:, g * SGU_GROUP_DIM:(g + 1) * SGU_GROUP_DIM]) for g in range(SGU_GROUPS)]
        o_ref[sl, :] = (u_ref[sl, :] * (jnp.concatenate(sv, axis=1) + bias_ref[...])).astype(o_ref.dtype)


def _sgu(u, v, w, bias, nck):
    t, c = u.shape
    tm = nck * SGU_CHUNK
    return pl.pallas_call(
        functools.partial(_sgu_kernel, nck),
        grid=(t // tm,),
        in_specs=[pl.BlockSpec((tm, c), lambda i: (i, 0)), pl.BlockSpec((tm, c), lambda i: (i, 0)),
                  _full(w.shape), _full(bias.shape)],
        out_specs=pl.BlockSpec((tm, c), lambda i: (i, 0)),
        out_shape=jax.ShapeDtypeStruct((t, c), BF16),
        compiler_params=_cparams(("parallel",)),
        name="sgu",
    )(u, v, w, bias)


def _merge_kernel(x_ref, gmix_ref, oat_ref, ob_ref, oct_ref, od_ref, wg_ref, wbr_ref, wo_ref, gmoe_ref, wrt_ref,
                  o_ref, h_ref, aff_ref):
    x = x_ref[...]
    dm = x.shape[1]
    hb = _rms(x, gmix_ref[...]).astype(BF16)
    merged = jnp.zeros(x.shape, F32)
    for n, (br, transposed) in enumerate(((oat_ref, True), (ob_ref, False), (oct_ref, True), (od_ref, False))):
        gate = jax.nn.sigmoid(_dot(hb, wg_ref[:, n * dm:(n + 1) * dm]))
        wide = _dot_tn(br[...], wbr_ref[n]) if transposed else _dot(br[...], wbr_ref[n])
        merged = merged + gate * wide
    x = x + _dot(merged.astype(BF16), wo_ref[...])
    o_ref[...] = x

    h = _rms(x, gmoe_ref[...])
    h_ref[...] = h.astype(BF16)
    logits = lax.dot_general(wrt_ref[...], h, (((1,), (1,)), ((), ())), precision=HIGHEST,
                             preferred_element_type=F32)
    e = jnp.exp(logits - jnp.max(logits, axis=0, keepdims=True))
    aff_ref[...] = e / jnp.sum(e, axis=0, keepdims=True)


def _merge(x, gmix, oat, ob, oct, od, wg, wbr, wo, gmoe, wrt, b, s, tm):
    t, dm = x.shape
    ne = wrt.shape[0]
    nst = s // tm
    row = lambda w: pl.BlockSpec((tm, w), lambda i: (i, 0))
    lanes = lambda w: pl.BlockSpec((None, w, tm), lambda i: (i // nst, 0, i % nst))
    return pl.pallas_call(
        _merge_kernel,
        grid=(t // tm,),
        in_specs=[row(dm), _full(gmix.shape), lanes(BRANCH_WIDTH), row(BRANCH_WIDTH), lanes(BRANCH_WIDTH),
                  row(BRANCH_WIDTH), _full(wg.shape), _full(wbr.shape), _full(wo.shape), _full(gmoe.shape),
                  _full(wrt.shape)],
        out_specs=[row(dm), row(dm), lanes(ne)],
        out_shape=[jax.ShapeDtypeStruct((t, dm), F32), jax.ShapeDtypeStruct((t, dm), BF16),
                   jax.ShapeDtypeStruct((b, ne, s), F32)],
        compiler_params=_cparams(("parallel",)),
        name="merge",
    )(x, gmix, oat, ob, oct, od, wg, wbr, wo, gmoe, wrt)


def _select_kernel(cap, aff_ref, gate_ref, pos_ref):
    ne, s = aff_ref.shape
    bits = pltpu.bitcast(aff_ref[...], jnp.int32)
    capf = jnp.float32(cap)

    def bisect(i, thr):
        cand = thr | lax.shift_left(jnp.int32(1), 30 - i)
        cnt = jnp.sum((bits >= cand).astype(F32), axis=1, keepdims=True)
        return jnp.where(cnt >= capf, cand, thr)

    thr = lax.fori_loop(0, 31, bisect, jnp.zeros((ne, 1), jnp.int32))
    need = capf - jnp.sum((bits > thr).astype(F32), axis=1, keepdims=True)
    lanes = V7X_LANES
    r = lax.broadcasted_iota(jnp.int32, (lanes, lanes), 0)
    c = lax.broadcasted_iota(jnp.int32, (lanes, lanes), 1)
    before = (r < c).astype(BF16)

    def block(jb, carry):
        n_eq, n_keep = carry
        off = pl.multiple_of(jb * lanes, lanes)
        a = aff_ref[:, pl.ds(off, lanes)]
        bb = pltpu.bitcast(a, jnp.int32)
        eq = bb == thr
        eqf = eq.astype(F32)
        keep = jnp.logical_or(bb > thr, jnp.logical_and(eq, n_eq + _dot(eqf.astype(BF16), before) < need))
        keepf = keep.astype(F32)
        gate_ref[:, pl.ds(off, lanes)] = jnp.where(keep, a, 0.0)
        pos_ref[:, pl.ds(off, lanes)] = jnp.where(keep, n_keep + _dot(keepf.astype(BF16), before), -1.0)
        return n_eq + jnp.sum(eqf, axis=1, keepdims=True), n_keep + jnp.sum(keepf, axis=1, keepdims=True)

    zero = jnp.zeros((ne, 1), F32)
    lax.fori_loop(0, s // lanes, block, (zero, zero))


def _select(aff, cap):
    b, ne, s = aff.shape
    spec = pl.BlockSpec((None, ne, s), lambda i: (i, 0, 0))
    return pl.pallas_call(
        functools.partial(_select_kernel, cap),
        grid=(b,),
        in_specs=[spec],
        out_specs=[spec, spec],
        out_shape=[jax.ShapeDtypeStruct(aff.shape, F32), jax.ShapeDtypeStruct(aff.shape, F32)],
        compiler_params=_cparams(("parallel",)),
        name="moe_select",
    )(aff)


def _moe_kernel(rb, tt, cnt_ref, base_ref, h_ref, pos_ref, gate_ref, wg_ref, wu_ref, wd_ref, o_ref):
    i = pl.program_id(0)
    e = pl.program_id(1)
    nsub = h_ref.shape[0] // tt

    @pl.when(e == 0)
    def _():
        o_ref[...] = jnp.zeros(o_ref.shape, F32)

    for sub in range(nsub):
        rows = slice(sub * tt, (sub + 1) * tt)
        tile = i * nsub + sub
        pos = pos_ref[:, rows] - base_ref[tile, e].astype(F32)
        gate = gate_ref[:, rows]

        def block(bi, carry):
            slot = lax.broadcasted_iota(jnp.int32, (rb, 1), 0).astype(F32) + (bi * rb).astype(F32)
            hit = pos == slot
            onehot = hit.astype(BF16)
            xc = _dot(onehot, h_ref[rows, :]).astype(BF16)
            hid = jax.nn.silu(_dot(xc, wg_ref[...])) * _dot(xc, wu_ref[...])
            y = _dot(hid.astype(BF16), wd_ref[...])
            g = jnp.sum(jnp.where(hit, gate, 0.0), axis=1, keepdims=True)
            o_ref[rows, :] += _dot_tn(onehot, (y * g).astype(BF16))
            return carry

        lax.fori_loop(0, (cnt_ref[tile, e] + rb - 1) // rb, block, 0)


def _moe(h, pos, gates, cnt, base, wg, wu, wd, tt, nsub, rb):
    t, dm = h.shape
    ne, _, dff = wg.shape
    ts = tt * nsub
    nst = pos.shape[2] // ts
    lane_row = pl.BlockSpec((None, 1, ts), lambda i, e, *_: ((i // nst) * ne + e, 0, i % nst))
    return pl.pallas_call(
        functools.partial(_moe_kernel, rb, tt),
        grid_spec=pltpu.PrefetchScalarGridSpec(
            num_scalar_prefetch=2,
            grid=(t // ts, ne),
            in_specs=[pl.BlockSpec((ts, dm), lambda i, e, *_: (i, 0)), lane_row, lane_row,
                      pl.BlockSpec((None, dm, dff), lambda i, e, *_: (e, 0, 0)),
                      pl.BlockSpec((None, dm, dff), lambda i, e, *_: (e, 0, 0)),
                      pl.BlockSpec((None, dff, dm), lambda i, e, *_: (e, 0, 0))],
            out_specs=pl.BlockSpec((ts, dm), lambda i, e, *_: (i, 0))),
        out_shape=jax.ShapeDtypeStruct((t, dm), F32),
        compiler_params=_cparams(("parallel", "arbitrary")),
        name="moe_ffn",
    )(cnt, base, h, pos, gates, wg, wu, wd)


def _ple_kernel(final, x_ref, moe_ref, p_ref, g_ref, wpg_ref, wp_ref, gf_ref, o_ref):
    x = x_ref[...] + moe_ref[...]
    gate = jax.nn.sigmoid(_dot(_rms(x, g_ref[...]).astype(BF16), wpg_ref[...]))
    y = x + gate * _dot(p_ref[...].astype(BF16), wp_ref[...])
    o_ref[...] = _rms(y, gf_ref[...]) if final else y


def _ple(x, moe, p, g, wpg, wp, gf, final, tm):
    t, dm = x.shape
    return pl.pallas_call(
        functools.partial(_ple_kernel, final),
        grid=(t // tm,),
        in_specs=[pl.BlockSpec((tm, dm), lambda i: (i, 0)), pl.BlockSpec((tm, dm), lambda i: (i, 0)),
                  pl.BlockSpec((tm, p.shape[1]), lambda i: (i, 0)),
                  _full(g.shape), _full(wpg.shape), _full(wp.shape), _full(gf.shape)],
        out_specs=pl.BlockSpec((tm, dm), lambda i: (i, 0)),
        out_shape=jax.ShapeDtypeStruct((t, dm), F32),
        compiler_params=_cparams(("parallel",)),
        name="ple",
    )(x, moe, p, g, wpg, wp, gf)


def _rope_table(s):
    half = A_HEAD_DIM // 2
    pos = jnp.arange(s, dtype=jnp.int32)
    freqs = ROPE_THETA ** (-jnp.arange(0, half, 2, dtype=F32) / half)
    ang_r = freqs[:, None] * (pos // GRID_W).astype(F32)[None, :]
    ang_c = freqs[:, None] * (pos % GRID_W).astype(F32)[None, :]
    return jnp.concatenate([jnp.cos(ang_r), jnp.sin(ang_r), jnp.cos(ang_c), jnp.sin(ang_c)], axis=0)


def _t5_bucket(rel):
    nb = REL_BUCKETS // 2
    max_exact = nb // 2
    ret = jnp.where(rel > 0, nb, 0)
    r = jnp.abs(rel)
    rf = jnp.maximum(r, 1).astype(F32)
    large = max_exact + (jnp.log(rf / max_exact) / math.log(REL_MAX_DIST / max_exact)
                         * (nb - max_exact)).astype(jnp.int32)
    large = jnp.minimum(large, nb - 1)
    return ret + jnp.where(r < max_exact, r, large)


def _bias_tables(rel_bias, tq, tk):
    assert tk >= REL_MAX_DIST and tq % tk == 0
    rep = tq // tk
    offs = jnp.arange(-1, rep + 1, dtype=jnp.int32) * tk
    u = jnp.arange(tq + tk, dtype=jnp.int32)
    rel = offs[:, None] + (tk - 1) - u[None, :]
    gband = jnp.transpose(rel_bias[_t5_bucket(rel)], (2, 0, 1))[:, :, None, :].astype(F32) * LOG2E
    far = jnp.array([-REL_MAX_DIST, REL_MAX_DIST], dtype=jnp.int32)
    cfar = jnp.transpose(rel_bias[_t5_bucket(far)], (1, 0))[:, None, :].astype(F32) * LOG2E
    return gband, cfar


def _tile(n, pref):
    t = min(n, pref)
    assert n % t == 0
    return t


def kernel(x, p, rel_bias, g_mix, w_in, g_qnorm, g_knorm, conv_w, conv_b, dt_bias_f, dt_bias_b, a_log_f, a_log_b, d_skip, g_ssm, lambda_q1, lambda_k1, lambda_q2, lambda_k2, g_diff, g_sgu, w_spatial, b_spatial, w_branch, w_branch_gate, w_out, g_moe, w_router, w_exp_gate, w_exp_up, w_exp_down, g_ple, w_ple_gate, w_ple, g_final):
    b, s, dm = x.shape
    depth = w_in.shape[0]
    t = b * s
    cap = EC_CAPACITY * s // N_EXPERTS

    tm_proj = _tile(s, 512)
    tm_prep = _tile(s, 1024)
    tq_a, tk_a = _tile(s, 512), _tile(s, 1024)
    tq_c, tk_c = _tile(s, 512), _tile(s, 512)
    cps = _tile(s // SSM_CHUNK, 4)
    nck = _tile(s // SGU_CHUNK, 8)
    tt_moe = _tile(s, 1024)

    rope_tab = _rope_table(s)
    gband, cfar = _bias_tables(rel_bias, tq_c, tk_c)
    hd = A_HEAD_DIM

    sizes = (A_Q_HEADS * hd, A_KV_HEADS * hd, A_KV_HEADS * hd,
             SSM_HEADS * SSM_HEAD_DIM, SSM_HEADS * SSM_HEAD_DIM, SSM_GROUPS * SSM_STATE,
             SSM_GROUPS * SSM_STATE, SSM_HEADS, SSM_HEADS,
             DIFF_HEADS * 2 * DIFF_QK_DIM, DIFF_HEADS * 2 * DIFF_QK_DIM, DIFF_HEADS * DIFF_V_DIM,
             2 * SGU_GROUPS * SGU_GROUP_DIM)
    off = [0]
    for sz in sizes:
        off.append(off[-1] + sz)
    o_a0, o_b0, o_dt0, o_c0, o_d0, o_end = off[0], off[3], off[7], off[9], off[12], off[13]
    dt_pad = V7X_LANES - 2 * SSM_HEADS

    def lane_row(v, width=None):
        v = v.astype(F32).reshape(1, -1)
        if width is not None and v.shape[1] < width:
            v = jnp.pad(v, ((0, 0), (0, width - v.shape[1])))
        return v

    xt = x.reshape(t, dm)
    for i in range(depth):
        wi = w_in[i]
        wat = wi[:, o_a0:o_b0].T.astype(BF16)
        wb = jnp.pad(wi[:, o_b0:o_c0], ((0, 0), (0, dt_pad))).astype(BF16)
        wct = wi[:, o_c0:o_d0].T.astype(BF16)
        wd = wi[:, o_d0:o_end].astype(BF16)
        gmix = lane_row(g_mix[i])
        col = lambda v: v.astype(F32).reshape(-1, 1)
        (qt, kh, vt, z, xbc, dt, cqt, ckh, cvt, du, dv) = _inproj(
            xt, gmix, wat, wb, wct, wd, rope_tab, col(g_qnorm[i]), col(g_knorm[i]), lane_row(g_sgu[i]),
            b, s, tm_proj)

        o_at = _attn_a(qt.reshape(b, A_Q_HEADS, hd, s), kh, vt, tq_a, tk_a, ATTN_A_INNER).reshape(b, -1, s)

        xa, dl = _ssd_prep(xbc, dt, conv_w[i].astype(F32), lane_row(conv_b[i]),
                           lane_row(jnp.concatenate([dt_bias_f[i], dt_bias_b[i]]), V7X_LANES), s, tm_prep)
        o_b = _ssd(xa, dl, lane_row(jnp.concatenate([a_log_f[i], a_log_b[i]]), V7X_LANES), z,
                   lane_row(jnp.repeat(d_skip[i], SSM_HEAD_DIM)), lane_row(g_ssm[i]), b, s, cps)

        lam_init = 0.8 - 0.6 * math.exp(-0.3 * i)
        lam = (jnp.exp(jnp.sum(lambda_q1[i].astype(F32) * lambda_k1[i].astype(F32)))
               - jnp.exp(jnp.sum(lambda_q2[i].astype(F32) * lambda_k2[i].astype(F32))) + lam_init).reshape(1, 1)
        o_ct = _attn_c(cqt.reshape(b, DIFF_HEADS, 2, DIFF_QK_DIM, s), ckh, cvt, gband, cfar, lam, col(g_diff[i]),
                       1.0 - lam_init, tq_c, tk_c, ATTN_C_INNER).reshape(b, -1, s)

        sgu_bias = jnp.repeat(b_spatial[i].T.astype(F32), SGU_GROUP_DIM, axis=1)
        o_d = _sgu(du, dv, w_spatial[i].astype(BF16), sgu_bias, nck)

        xt, h2, aff = _merge(xt, gmix, o_at, o_b, o_ct, o_d, w_branch_gate[i].astype(BF16),
                             w_branch[i].astype(BF16), w_out[i].astype(BF16), lane_row(g_moe[i]),
                             w_router[i].T.astype(F32), b, s, tm_proj)

        gates, pos = _select(aff, cap)
        cnt = jnp.sum((pos >= 0).reshape(b, N_EXPERTS, s // tt_moe, tt_moe), axis=3, dtype=jnp.int32)
        tile_major = lambda a: jnp.transpose(a, (0, 2, 1)).reshape(t // tt_moe, N_EXPERTS)
        moe = _moe(h2, pos.reshape(b * N_EXPERTS, 1, s), gates.reshape(b * N_EXPERTS, 1, s), tile_major(cnt),
                   tile_major(jnp.cumsum(cnt, axis=2) - cnt), w_exp_gate[i].astype(BF16), w_exp_up[i].astype(BF16), w_exp_down[i].astype(BF16),
                   tt_moe, _tile(s // tt_moe, 2), MOE_ROW_BLOCK)

        xt = _ple(xt, moe, p[i].reshape(t, -1), lane_row(g_ple[i]), w_ple_gate[i].astype(BF16),
                  w_ple[i].astype(BF16), lane_row(g_final), i == depth - 1, tm_proj)

    return xt.reshape(b, s, dm)
```

```python
import functools
import math

import jax
import jax.numpy as jnp
from jax import lax
from jax.experimental import pallas as pl
from jax.experimental.pallas import tpu as pltpu

F32 = jnp.float32
BF16 = jnp.bfloat16
HIGHEST = lax.Precision.HIGHEST

EPS = 1e-6
GRID_W = 64
A_Q_HEADS, A_KV_HEADS, A_HEAD_DIM = 4, 2, 64
ROPE_THETA = 10000.0
SSM_HEADS, SSM_HEAD_DIM, SSM_GROUPS, SSM_STATE, SSM_CONV, SSM_CHUNK = 4, 64, 2, 128, 5, 128
DIFF_HEADS, DIFF_QK_DIM, DIFF_V_DIM = 4, 32, 64
REL_BUCKETS, REL_MAX_DIST = 32, 128
SGU_GROUPS, SGU_GROUP_DIM, SGU_CHUNK = 4, 64, 128
N_EXPERTS, EC_CAPACITY = 16, 2
N_BRANCHES, BRANCH_WIDTH = 4, 256

V7X_LANES = 128
V7X_SUBLANES = 8
BF16_SUBLANE_PACK = 16
V7X_VMEM_LIMIT_BYTES = 56 * 1024 * 1024

NEG_BIG = -1e30
LOG2E = math.log2(math.e)
MOE_ROW_BLOCK = 160
ATTN_A_RING, ATTN_A_INNER = 3, 2
ATTN_C_RING, ATTN_C_INNER = 2, 7


def _cparams(sem):
    return pltpu.CompilerParams(dimension_semantics=sem, vmem_limit_bytes=V7X_VMEM_LIMIT_BYTES)


def _rms(x, g):
    return x * lax.rsqrt(jnp.mean(x * x, axis=-1, keepdims=True) + EPS) * g


def _dot(a, b):
    return jnp.dot(a, b, preferred_element_type=F32)


def _dot_nt(a, b):
    return lax.dot_general(a, b, (((1,), (1,)), ((), ())), preferred_element_type=F32)


def _dot_tn(a, b):
    return lax.dot_general(a, b, (((0,), (0,)), ((), ())), preferred_element_type=F32)


def _full(shape):
    n = len(shape)
    return pl.BlockSpec(shape, lambda *_: (0,) * n)


def _norm_rope_t(x, g_col, tab):
    xn = x * lax.rsqrt(jnp.mean(x * x, axis=0, keepdims=True) + EPS) * g_col
    qd = A_HEAD_DIM // 4
    cr, sr, cc, sc = (tab[i * qd:(i + 1) * qd] for i in range(4))
    r1, r2, c1, c2 = (xn[i * qd:(i + 1) * qd] for i in range(4))
    return jnp.concatenate([r1 * cr - r2 * sr, r1 * sr + r2 * cr, c1 * cc - c2 * sc, c1 * sc + c2 * cc], axis=0)


def _inproj_kernel(x_ref, gmix_ref, wat_ref, wb_ref, wct_ref, wd_ref, tab_ref, gq_ref, gk_ref, gsgu_ref,
                   qa_ref, ka_ref, va_ref, z_ref, xbc_ref, dt_ref, cq_ref, ck_ref, cv_ref,
                   du_ref, dv_ref):
    hb = _rms(x_ref[...], gmix_ref[...]).astype(BF16)
    tm = hb.shape[0]
    hd = A_HEAD_DIM
    ones = jnp.ones((BF16_SUBLANE_PACK, tm), BF16)

    at = _dot_nt(wat_ref[...], hb)
    tab = tab_ref[...]
    q = [_norm_rope_t(at[h * hd:(h + 1) * hd], gq_ref[...], tab) for h in range(A_Q_HEADS)]
    qa_ref[...] = (jnp.concatenate(q, axis=0) * (hd ** -0.5 * LOG2E)).astype(BF16)
    k0 = A_Q_HEADS * hd
    v0 = k0 + A_KV_HEADS * hd
    for g in range(A_KV_HEADS):
        kt = _norm_rope_t(at[k0 + g * hd:k0 + (g + 1) * hd], gk_ref[...], tab)
        ka_ref[g] = jnp.transpose(kt).astype(BF16)
        va_ref[g, :hd] = at[v0 + g * hd:v0 + (g + 1) * hd].astype(BF16)
        va_ref[g, hd:] = ones

    b = _dot(hb, wb_ref[...])
    dx = SSM_HEADS * SSM_HEAD_DIM
    conv_ch = dx + 2 * SSM_GROUPS * SSM_STATE
    z_ref[...] = b[:, :dx]
    xbc_ref[...] = b[:, dx:dx + conv_ch]
    dt_ref[...] = b[:, dx + conv_ch:]

    ct = _dot_nt(wct_ref[...], hb)
    nqc = DIFF_HEADS * 2 * DIFF_QK_DIM
    cq_ref[...] = (ct[:nqc] * (DIFF_QK_DIM ** -0.5 * LOG2E)).astype(BF16)
    ck = jnp.transpose(ct[nqc:2 * nqc]).astype(BF16)
    for h in range(DIFF_HEADS):
        for c in range(2):
            lo = (2 * h + c) * DIFF_QK_DIM
            ck_ref[h, c] = ck[:, lo:lo + DIFF_QK_DIM]
        cv_ref[h, :DIFF_V_DIM] = ct[2 * nqc + h * DIFF_V_DIM:2 * nqc + (h + 1) * DIFF_V_DIM].astype(BF16)
        cv_ref[h, DIFF_V_DIM:] = ones

    d = jax.nn.gelu(_dot(hb, wd_ref[...]))
    w = SGU_GROUPS * SGU_GROUP_DIM
    du_ref[...] = d[:, :w]
    dv_ref[...] = _rms(d[:, w:], gsgu_ref[...]).astype(BF16)


def _inproj(x, gmix, wat, wb, wct, wd, tab, gq, gk, gsgu, b, s, tm):
    t, dm = x.shape
    nst = s // tm
    row = lambda w: pl.BlockSpec((tm, w), lambda i: (i, 0))
    lanes = lambda *lead: pl.BlockSpec((None,) + lead + (tm,), lambda i: (i // nst,) + (0,) * len(lead) + (i % nst,))
    hd, pk = A_HEAD_DIM, BF16_SUBLANE_PACK
    specs_shapes = [
        (lanes(A_Q_HEADS * hd), (b, A_Q_HEADS * hd, s), BF16),
        (pl.BlockSpec((None, A_KV_HEADS, tm, hd), lambda i: (i // nst, 0, i % nst, 0)), (b, A_KV_HEADS, s, hd), BF16),
        (lanes(A_KV_HEADS, hd + pk), (b, A_KV_HEADS, hd + pk, s), BF16),
        (row(256), (t, 256), F32), (row(768), (t, 768), F32), (row(128), (t, 128), F32),
        (lanes(DIFF_HEADS * 2 * DIFF_QK_DIM), (b, DIFF_HEADS * 2 * DIFF_QK_DIM, s), BF16),
        (pl.BlockSpec((None, DIFF_HEADS, 2, tm, DIFF_QK_DIM), lambda i: (i // nst, 0, 0, i % nst, 0)),
         (b, DIFF_HEADS, 2, s, DIFF_QK_DIM), BF16),
        (lanes(DIFF_HEADS, DIFF_V_DIM + pk), (b, DIFF_HEADS, DIFF_V_DIM + pk, s), BF16),
        (row(256), (t, 256), F32), (row(256), (t, 256), BF16)]
    return pl.pallas_call(
        _inproj_kernel,
        grid=(t // tm,),
        in_specs=[row(dm), _full(gmix.shape), _full(wat.shape), _full(wb.shape), _full(wct.shape),
                  _full(wd.shape), pl.BlockSpec((tab.shape[0], tm), lambda i: (0, i % nst)),
                  _full(gq.shape), _full(gk.shape), _full(gsgu.shape)],
        out_specs=[sp for sp, _, _ in specs_shapes],
        out_shape=[jax.ShapeDtypeStruct(shp, dt) for _, shp, dt in specs_shapes],
        compiler_params=_cparams(("parallel",)),
        name="inproj",
    )(x, gmix, wat, wb, wct, wd, tab, gq, gk, gsgu)


def _osm_tile(s, mt, shift, vt, m_ref, acc_ref, c):
    m_old = m_ref[c]
    if shift is not None:
        mt = mt + shift
    m_new = jnp.maximum(m_old, mt)
    p = jnp.exp2(s - (m_new if shift is None else m_new - shift))
    acc_ref[c] = jnp.exp2(m_old - m_new) * acc_ref[c] + _dot(vt, p.astype(BF16))
    m_ref[c] = m_new


def _osm_init(m_ref, acc_ref):
    m_ref[...] = jnp.full(m_ref.shape, NEG_BIG, F32)
    acc_ref[...] = jnp.zeros(acc_ref.shape, F32)


def _osm_result(acc_ref, c, dv):
    acc = acc_ref[c]
    return acc[:dv] / acc[dv:dv + 1]


def _attn_scores(tk, j, qt_ref, k_of, buf, bias_tile=None):
    s_ref, mx_ref = buf
    off = pl.multiple_of(j * tk, tk)
    for c in range(qt_ref.shape[0]):
        s = _dot(k_of(c, off), qt_ref[c])
        if bias_tile is not None:
            s = s + bias_tile
        s_ref[c] = s
        mx_ref[c] = jnp.max(s, axis=0, keepdims=True)


def _attn_values(tk, j, nmaps, vt_ref, buf, m_ref, acc_ref, shift=None):
    s_ref, mx_ref = buf
    vt = vt_ref[:, pl.ds(pl.multiple_of(j * tk, tk), tk)]
    for c in range(nmaps):
        _osm_tile(s_ref[c], mx_ref[c], shift, vt, m_ref, acc_ref, c)


def _pipelined(n, scores, values, bufs, inner):
    nb = len(bufs)
    ahead = nb - 1
    last = jnp.maximum(n - 1, 0)
    for r in range(ahead):
        scores(jnp.minimum(r, last), bufs[r])

    def ring(g):
        for r in range(nb):
            scores(jnp.minimum(g + r + ahead, last), bufs[(r + ahead) % nb])
            values(g + r, bufs[r])

    span = nb * inner

    def unrolled(t, carry):
        for w in range(inner):
            ring(span * t + nb * w)
        return carry

    lax.fori_loop(0, n // span, unrolled, 0)
    base = (n // span) * span

    def rolled(t, carry):
        ring(base + nb * t)
        return carry

    lax.fori_loop(0, (n - base) // nb, rolled, 0)
    done = (n // nb) * nb
    for r in range(ahead):

        @pl.when(n - done > r)
        def _():
            values(done + r, bufs[r])


def _ring_scratch(nring, nmaps, tk, tq):
    return [pltpu.VMEM(shape, F32) for _ in range(nring) for shape in ((nmaps, tk, tq), (nmaps, 1, tq))]


def _ring_bufs(refs):
    return tuple((refs[2 * r], refs[2 * r + 1]) for r in range(len(refs) // 2))


def _attn_a_kernel(tk, inner, qt_ref, k_ref, vt_ref, o_ref, m_ref, acc_ref, *ring_refs):
    _osm_init(m_ref, acc_ref)
    nq = qt_ref.shape[0]
    k_of = lambda c, off: k_ref[pl.ds(off, tk), :]
    _pipelined(k_ref.shape[0] // tk,
               lambda j, buf: _attn_scores(tk, j, qt_ref, k_of, buf),
               lambda j, buf: _attn_values(tk, j, nq, vt_ref, buf, m_ref, acc_ref),
               _ring_bufs(ring_refs), inner)
    for c in range(nq):
        o_ref[c] = _osm_result(acc_ref, c, o_ref.shape[1]).astype(o_ref.dtype)


def _attn_a(qt, k, vt, tq, tk, inner):
    b, hq, d, s = qt.shape
    hkv = k.shape[1]
    rep = hq // hkv
    da = vt.shape[2]
    qspec = pl.BlockSpec((None, rep, d, tq), lambda bi, g, i: (bi, g, 0, i))
    return pl.pallas_call(
        functools.partial(_attn_a_kernel, tk, inner),
        grid=(b, hkv, s // tq),
        in_specs=[qspec,
                  pl.BlockSpec((None, None, s, d), lambda bi, g, i: (bi, g, 0, 0)),
                  pl.BlockSpec((None, None, da, s), lambda bi, g, i: (bi, g, 0, 0))],
        out_specs=qspec,
        out_shape=jax.ShapeDtypeStruct(qt.shape, BF16),
        scratch_shapes=[pltpu.VMEM((rep, 1, tq), F32), pltpu.VMEM((rep, da, tq), F32)]
        + _ring_scratch(ATTN_A_RING, rep, tk, tq),
        compiler_params=_cparams(("parallel", "parallel", "arbitrary")),
        name="attn_a",
    )(qt, k, vt)


def _attn_c_kernel(tk, inner, out_scale, qt_ref, k_ref, vt_ref, gband_ref, cfar_ref, lam_ref, gd_ref, o_ref,
                   m_ref, acc_ref, *ring_refs):
    i = pl.program_id(2)
    tq = qt_ref.shape[2]
    nk = k_ref.shape[1] // tk
    nband = gband_ref.shape[0]
    dv = o_ref.shape[0]
    _osm_init(m_ref, acc_ref)
    k_of = lambda c, off: k_ref[c, pl.ds(off, tk), :]

    j_lo = (tq // tk) * i - 1
    n_left = jnp.clip(j_lo, 0, nk)
    j_hi = jnp.clip(j_lo + nband, 0, nk)
    n_far = n_left + nk - j_hi
    cf = cfar_ref[...]

    def far_tile(t):
        j = jnp.where(t < n_left, t, t - n_left + j_hi)
        return jnp.clip(j, 0, nk - 1), jnp.where(t < n_left, cf[:, 0:1], cf[:, 1:2])

    def far_scores(t, buf):
        _attn_scores(tk, far_tile(t)[0], qt_ref, k_of, buf)

    def far_values(t, buf):
        j, sh = far_tile(t)
        _attn_values(tk, j, 2, vt_ref, buf, m_ref, acc_ref, shift=sh)

    bufs = _ring_bufs(ring_refs)
    _pipelined(n_far, far_scores, far_values, bufs, inner)

    d_lo = n_left - j_lo

    def band_scores(t, buf):
        d = jnp.clip(d_lo + t, 0, nband - 1)
        g = gband_ref[d]
        tile = pltpu.roll(jnp.broadcast_to(g, (tk, g.shape[1])), tq + 1, 1, stride=1, stride_axis=0)
        _attn_scores(tk, jnp.clip(j_lo + d, 0, nk - 1), qt_ref, k_of, buf, bias_tile=tile[:, :tq])

    def band_values(t, buf):
        _attn_values(tk, j_lo + d_lo + t, 2, vt_ref, buf, m_ref, acc_ref)

    _pipelined(j_hi - n_left, band_scores, band_values, bufs, 1)

    o = _osm_result(acc_ref, 0, dv) - lam_ref[...] * _osm_result(acc_ref, 1, dv)
    o = o * lax.rsqrt(jnp.mean(o * o, axis=0, keepdims=True) + EPS) * gd_ref[...]
    o_ref[...] = (o * out_scale).astype(o_ref.dtype)


def _attn_c(qt, k, vt, gband, cfar, lam, gd, out_scale, tq, tk, inner):
    b, h, _, dqk, s = qt.shape
    dva = vt.shape[2]
    dv = gd.shape[0]
    nband = gband.shape[1]
    return pl.pallas_call(
        functools.partial(_attn_c_kernel, tk, inner, out_scale),
        grid=(b, h, s // tq),
        in_specs=[pl.BlockSpec((None, None, 2, dqk, tq), lambda bi, hi, i: (bi, hi, 0, 0, i)),
                  pl.BlockSpec((None, None, 2, s, dqk), lambda bi, hi, i: (bi, hi, 0, 0, 0)),
                  pl.BlockSpec((None, None, dva, s), lambda bi, hi, i: (bi, hi, 0, 0)),
                  pl.BlockSpec((None, nband, 1, tq + tk), lambda bi, hi, i: (hi, 0, 0, 0)),
                  pl.BlockSpec((None, 1, 2), lambda bi, hi, i: (hi, 0, 0)),
                  _full(lam.shape), _full(gd.shape)],
        out_specs=pl.BlockSpec((None, None, dv, tq), lambda bi, hi, i: (bi, hi, 0, i)),
        out_shape=jax.ShapeDtypeStruct((b, h, dv, s), BF16),
        scratch_shapes=[pltpu.VMEM((2, 1, tq), F32), pltpu.VMEM((2, dva, tq), F32)]
        + _ring_scratch(ATTN_C_RING, 2, tk, tq),
        compiler_params=_cparams(("parallel", "parallel", "arbitrary")),
        name="attn_c",
    )(qt, k, vt, gband, cfar, lam, gd)


def _ssd_prep_kernel(nst, xc_ref, xp_ref, xn_ref, dt_ref, cw_ref, cb_ref, dtb_ref, xa_ref, dl_ref, ext_ref):
    i = pl.program_id(0)
    tm = xc_ref.shape[0]
    halo = V7X_SUBLANES
    pos = i % nst
    keep_prev = (pos != 0).astype(F32)
    keep_next = (pos != nst - 1).astype(F32)
    ext_ref[0:halo, :] = xp_ref[...] * keep_prev
    ext_ref[halo:halo + tm, :] = xc_ref[...]
    ext_ref[halo + tm:, :] = xn_ref[...] * keep_next
    pad_l = (SSM_CONV - 1) // 2
    cw = cw_ref[...]
    acc = jnp.zeros(xc_ref.shape, F32) + cb_ref[...]
    for t in range(SSM_CONV):
        acc = acc + ext_ref[halo - pad_l + t:halo - pad_l + t + tm, :] * cw[t:t + 1, :]
    xa_ref[...] = jax.nn.silu(acc)
    dl_ref[...] = jax.nn.softplus(dt_ref[...] + dtb_ref[...])


def _ssd_prep(xbc, dt, cw, cb, dtb, s, tm):
    t, c = xbc.shape
    nst = s // tm
    hb = tm // V7X_SUBLANES
    nhb = t // V7X_SUBLANES
    return pl.pallas_call(
        functools.partial(_ssd_prep_kernel, nst),
        grid=(t // tm,),
        in_specs=[pl.BlockSpec((tm, c), lambda i: (i, 0)),
                  pl.BlockSpec((V7X_SUBLANES, c), lambda i: (jnp.maximum(i * hb - 1, 0), 0)),
                  pl.BlockSpec((V7X_SUBLANES, c), lambda i: (jnp.minimum((i + 1) * hb, nhb - 1), 0)),
                  pl.BlockSpec((tm, dt.shape[1]), lambda i: (i, 0)),
                  _full(cw.shape), _full(cb.shape), _full(dtb.shape)],
        out_specs=[pl.BlockSpec((tm, c), lambda i: (i, 0)),
                   pl.BlockSpec((tm, dt.shape[1]), lambda i: (i, 0))],
        out_shape=[jax.ShapeDtypeStruct((t, c), F32), jax.ShapeDtypeStruct(dt.shape, F32)],
        scratch_shapes=[pltpu.VMEM((tm + 2 * V7X_SUBLANES, c), F32)],
        compiler_params=_cparams(("parallel",)),
        name="ssd_prep",
    )(xbc, xbc, xbc, dt, cw, cb, dtb)


def _ssd_chunk(reverse, xa, dl, alog, st_ref):
    ln = SSM_CHUNK
    row = lax.broadcasted_iota(jnp.int32, (ln, ln), 0)
    col = lax.broadcasted_iota(jnp.int32, (ln, ln), 1)
    tri = (col >= row) if reverse else (col <= row)
    eye = row == col
    last = 0 if reverse else ln - 1
    lane0 = SSM_HEADS if reverse else 0
    a_neg = -jnp.exp(alog)
    acs = jnp.dot(tri.astype(F32), dl * a_neg, precision=HIGHEST, preferred_element_type=F32)
    dx = SSM_HEADS * SSM_HEAD_DIM
    gw = SSM_STATE
    hpg = SSM_HEADS // SSM_GROUPS
    ys = []
    for g in range(SSM_GROUPS):
        bg = xa[:, dx + g * gw:dx + (g + 1) * gw].astype(BF16)
        cg = xa[:, dx + SSM_GROUPS * gw + g * gw:dx + SSM_GROUPS * gw + (g + 1) * gw].astype(BF16)
        cb = _dot_nt(cg, bg)
        for r in range(hpg):
            h = g * hpg + r
            ln_h = lane0 + h
            ac = acs[:, ln_h:ln_h + 1]
            ar = jnp.sum(jnp.where(eye, ac, 0.0), axis=0, keepdims=True)
            dec = jnp.exp(jnp.where(tri, ac - ar, NEG_BIG))
            xdt = xa[:, h * SSM_HEAD_DIM:(h + 1) * SSM_HEAD_DIM] * dl[:, ln_h:ln_h + 1]
            st = st_ref[h]
            y = _dot((cb * dec).astype(BF16), xdt.astype(BF16))
            y = y + _dot(cg, st.astype(BF16)) * jnp.exp(ac)
            a_last = acs[last:last + 1, ln_h:ln_h + 1]
            st_ref[h] = st * jnp.exp(a_last) + _dot_tn(bg, (xdt * jnp.exp(a_last - ac)).astype(BF16))
            ys.append(y)
    return jnp.concatenate(ys, axis=1)


def _ssd_fwd_kernel(cps, xa_ref, dl_ref, alog_ref, y_ref, st_ref):
    @pl.when(pl.program_id(1) == 0)
    def _():
        st_ref[...] = jnp.zeros(st_ref.shape, F32)

    for c in range(cps):
        sl = slice(c * SSM_CHUNK, (c + 1) * SSM_CHUNK)
        y_ref[sl, :] = _ssd_chunk(False, xa_ref[sl, :], dl_ref[sl, :], alog_ref[...], st_ref)


def _ssd_bwd_kernel(cps, xa_ref, dl_ref, alog_ref, yf_ref, z_ref, dsk_ref, g_ref, o_ref, st_ref):
    @pl.when(pl.program_id(1) == 0)
    def _():
        st_ref[...] = jnp.zeros(st_ref.shape, F32)

    dx = SSM_HEADS * SSM_HEAD_DIM
    for c in reversed(range(cps)):
        sl = slice(c * SSM_CHUNK, (c + 1) * SSM_CHUNK)
        xa = xa_ref[sl, :]
        yb = _ssd_chunk(True, xa, dl_ref[sl, :], alog_ref[...], st_ref)
        y = yf_ref[sl, :] + yb + dsk_ref[...] * xa[:, :dx]
        y = y * jax.nn.silu(z_ref[sl, :])
        o_ref[sl, :] = _rms(y, g_ref[...]).astype(o_ref.dtype)


def _ssd(xa, dl, alog, z, dsk, gssm, b, s, cps):
    t, c = xa.shape
    tm = cps * SSM_CHUNK
    nblk = s // tm
    dx = SSM_HEADS * SSM_HEAD_DIM
    st = pltpu.VMEM((SSM_HEADS, SSM_STATE, SSM_HEAD_DIM), F32)
    fw = lambda w: pl.BlockSpec((tm, w), lambda bi, ci: (bi * nblk + ci, 0))
    bw = lambda w: pl.BlockSpec((tm, w), lambda bi, ci: (bi * nblk + nblk - 1 - ci, 0))
    yf = pl.pallas_call(
        functools.partial(_ssd_fwd_kernel, cps),
        grid=(b, nblk),
        in_specs=[fw(c), fw(dl.shape[1]), _full(alog.shape)],
        out_specs=fw(dx),
        out_shape=jax.ShapeDtypeStruct((t, dx), F32),
        scratch_shapes=[st],
        compiler_params=_cparams(("parallel", "arbitrary")),
        name="ssd_fwd",
    )(xa, dl, alog)
    return pl.pallas_call(
        functools.partial(_ssd_bwd_kernel, cps),
        grid=(b, nblk),
        in_specs=[bw(c), bw(dl.shape[1]), _full(alog.shape), bw(dx), bw(dx), _full(dsk.shape),
                  _full(gssm.shape)],
        out_specs=bw(dx),
        out_shape=jax.ShapeDtypeStruct((t, dx), BF16),
        scratch_shapes=[st],
        compiler_params=_cparams(("parallel", "arbitrary")),
        name="ssd_bwd",
    )(xa, dl, alog, yf, z, dsk, gssm)


def _sgu_kernel(nck, u_ref, v_ref, w_ref, bias_ref, o_ref):
    for c in range(nck):
        sl = slice(c * SGU_CHUNK, (c + 1) * SGU_CHUNK)
        v = v_ref[sl, :]
        sv = [_dot(w_ref[g], v[:, g * SGU_GROUP_DIM:(g + 1) * SGU_GROUP_DIM]) for g in range(SGU_GROUPS)]
        o_ref[sl, :] = (u_ref[sl, :] * (jnp.concatenate(sv, axis=1) + bias_ref[...])).astype(o_ref.dtype)


def _sgu(u, v, w, bias, nck):
    t, c = u.shape
    tm = nck * SGU_CHUNK
    return pl.pallas_call(
        functools.partial(_sgu_kernel, nck),
        grid=(t // tm,),
        in_specs=[pl.BlockSpec((tm, c), lambda i: (i, 0)), pl.BlockSpec((tm, c), lambda i: (i, 0)),
                  _full(w.shape), _full(bias.shape)],
        out_specs=pl.BlockSpec((tm, c), lambda i: (i, 0)),
        out_shape=jax.ShapeDtypeStruct((t, c), BF16),
        compiler_params=_cparams(("parallel",)),
        name="sgu",
    )(u, v, w, bias)


def _merge_kernel(x_ref, gmix_ref, oat_ref, ob_ref, oct_ref, od_ref, wg_ref, wbr_ref, wo_ref, gmoe_ref, wrt_ref,
                  o_ref, h_ref, aff_ref):
    x = x_ref[...]
    dm = x.shape[1]
    hb = _rms(x, gmix_ref[...]).astype(BF16)
    merged = jnp.zeros(x.shape, F32)
    for n, (br, transposed) in enumerate(((oat_ref, True), (ob_ref, False), (oct_ref, True), (od_ref, False))):
        gate = jax.nn.sigmoid(_dot(hb, wg_ref[:, n * dm:(n + 1) * dm]))
        wide = _dot_tn(br[...], wbr_ref[n]) if transposed else _dot(br[...], wbr_ref[n])
        merged = merged + gate * wide
    x = x + _dot(merged.astype(BF16), wo_ref[...])
    o_ref[...] = x

    h = _rms(x, gmoe_ref[...])
    h_ref[...] = h.astype(BF16)
    logits = lax.dot_general(wrt_ref[...], h, (((1,), (1,)), ((), ())), precision=HIGHEST,
                             preferred_element_type=F32)
    e = jnp.exp(logits - jnp.max(logits, axis=0, keepdims=True))
    aff_ref[...] = e / jnp.sum(e, axis=0, keepdims=True)


def _merge(x, gmix, oat, ob, oct, od, wg, wbr, wo, gmoe, wrt, b, s, tm):
    t, dm = x.shape
    ne = wrt.shape[0]
    nst = s // tm
    row = lambda w: pl.BlockSpec((tm, w), lambda i: (i, 0))
    lanes = lambda w: pl.BlockSpec((None, w, tm), lambda i: (i // nst, 0, i % nst))
    return pl.pallas_call(
        _merge_kernel,
        grid=(t // tm,),
        in_specs=[row(dm), _full(gmix.shape), lanes(BRANCH_WIDTH), row(BRANCH_WIDTH), lanes(BRANCH_WIDTH),
                  row(BRANCH_WIDTH), _full(wg.shape), _full(wbr.shape), _full(wo.shape), _full(gmoe.shape),
                  _full(wrt.shape)],
        out_specs=[row(dm), row(dm), lanes(ne)],
        out_shape=[jax.ShapeDtypeStruct((t, dm), F32), jax.ShapeDtypeStruct((t, dm), BF16),
                   jax.ShapeDtypeStruct((b, ne, s), F32)],
        compiler_params=_cparams(("parallel",)),
        name="merge",
    )(x, gmix, oat, ob, oct, od, wg, wbr, wo, gmoe, wrt)


def _select_kernel(cap, aff_ref, gate_ref, pos_ref):
    ne, s = aff_ref.shape
    bits = pltpu.bitcast(aff_ref[...], jnp.int32)
    capf = jnp.float32(cap)

    def bisect(i, thr):
        cand = thr | lax.shift_left(jnp.int32(1), 30 - i)
        cnt = jnp.sum((bits >= cand).astype(F32), axis=1, keepdims=True)
        return jnp.where(cnt >= capf, cand, thr)

    thr = lax.fori_loop(0, 31, bisect, jnp.zeros((ne, 1), jnp.int32))
    need = capf - jnp.sum((bits > thr).astype(F32), axis=1, keepdims=True)
    lanes = V7X_LANES
    r = lax.broadcasted_iota(jnp.int32, (lanes, lanes), 0)
    c = lax.broadcasted_iota(jnp.int32, (lanes, lanes), 1)
    before = (r < c).astype(BF16)

    nblk = s // lanes
    group = 4 if nblk % 4 == 0 else 1

    def step(jg, carry):
        n_eq, n_keep = carry
        blocks = []
        for u in range(group):
            off = pl.multiple_of((jg * group + u) * lanes, lanes)
            a = aff_ref[:, pl.ds(off, lanes)]
            bb = pltpu.bitcast(a, jnp.int32)
            eq = bb == thr
            eqf = eq.astype(F32)
            blocks.append((off, a, bb, eq, _dot(eqf.astype(BF16), before), jnp.sum(eqf, axis=1, keepdims=True)))
        for off, a, bb, eq, eq_before, eq_total in blocks:
            keep = jnp.logical_or(bb > thr, jnp.logical_and(eq, n_eq + eq_before < need))
            keepf = keep.astype(F32)
            gate_ref[:, pl.ds(off, lanes)] = jnp.where(keep, a, 0.0)
            pos_ref[:, pl.ds(off, lanes)] = jnp.where(keep, n_keep + _dot(keepf.astype(BF16), before), -1.0)
            n_eq = n_eq + eq_total
            n_keep = n_keep + jnp.sum(keepf, axis=1, keepdims=True)
        return n_eq, n_keep

    zero = jnp.zeros((ne, 1), F32)
    lax.fori_loop(0, nblk // group, step, (zero, zero))


def _select(aff, cap):
    b, ne, s = aff.shape
    spec = pl.BlockSpec((None, ne, s), lambda i: (i, 0, 0))
    return pl.pallas_call(
        functools.partial(_select_kernel, cap),
        grid=(b,),
        in_specs=[spec],
        out_specs=[spec, spec],
        out_shape=[jax.ShapeDtypeStruct(aff.shape, F32), jax.ShapeDtypeStruct(aff.shape, F32)],
        compiler_params=_cparams(("parallel",)),
        name="moe_select",
    )(aff)


def _moe_kernel(rb, tt, cnt_ref, base_ref, h_ref, pos_ref, gate_ref, wg_ref, wu_ref, wd_ref, o_ref):
    i = pl.program_id(0)
    e = pl.program_id(1)
    nsub = h_ref.shape[0] // tt

    @pl.when(e == 0)
    def _():
        o_ref[...] = jnp.zeros(o_ref.shape, F32)

    for sub in range(nsub):
        rows = slice(sub * tt, (sub + 1) * tt)
        tile = i * nsub + sub
        pos = pos_ref[:, rows] - base_ref[tile, e].astype(F32)
        gate = gate_ref[:, rows]

        def block(bi, carry):
            slot = lax.broadcasted_iota(jnp.int32, (rb, 1), 0).astype(F32) + (bi * rb).astype(F32)
            hit = pos == slot
            onehot = hit.astype(BF16)
            xc = _dot(onehot, h_ref[rows, :]).astype(BF16)
            hid = jax.nn.silu(_dot(xc, wg_ref[...])) * _dot(xc, wu_ref[...])
            y = _dot(hid.astype(BF16), wd_ref[...])
            g = jnp.sum(jnp.where(hit, gate, 0.0), axis=1, keepdims=True)
            o_ref[rows, :] += _dot_tn(onehot, (y * g).astype(BF16))
            return carry

        lax.fori_loop(0, (cnt_ref[tile, e] + rb - 1) // rb, block, 0)


def _moe(h, pos, gates, cnt, base, wg, wu, wd, tt, nsub, rb):
    t, dm = h.shape
    ne, _, dff = wg.shape
    ts = tt * nsub
    nst = pos.shape[2] // ts
    lane_row = pl.BlockSpec((None, 1, ts), lambda i, e, *_: ((i // nst) * ne + e, 0, i % nst))
    return pl.pallas_call(
        functools.partial(_moe_kernel, rb, tt),
        grid_spec=pltpu.PrefetchScalarGridSpec(
            num_scalar_prefetch=2,
            grid=(t // ts, ne),
            in_specs=[pl.BlockSpec((ts, dm), lambda i, e, *_: (i, 0)), lane_row, lane_row,
                      pl.BlockSpec((None, dm, dff), lambda i, e, *_: (e, 0, 0)),
                      pl.BlockSpec((None, dm, dff), lambda i, e, *_: (e, 0, 0)),
                      pl.BlockSpec((None, dff, dm), lambda i, e, *_: (e, 0, 0))],
            out_specs=pl.BlockSpec((ts, dm), lambda i, e, *_: (i, 0))),
        out_shape=jax.ShapeDtypeStruct((t, dm), F32),
        compiler_params=_cparams(("parallel", "arbitrary")),
        name="moe_ffn",
    )(cnt, base, h, pos, gates, wg, wu, wd)


def _ple_kernel(final, x_ref, moe_ref, p_ref, g_ref, wpg_ref, wp_ref, gf_ref, o_ref):
    x = x_ref[...] + moe_ref[...]
    gate = jax.nn.sigmoid(_dot(_rms(x, g_ref[...]).astype(BF16), wpg_ref[...]))
    y = x + gate * _dot(p_ref[...].astype(BF16), wp_ref[...])
    o_ref[...] = _rms(y, gf_ref[...]) if final else y


def _ple(x, moe, p, g, wpg, wp, gf, final, tm):
    t, dm = x.shape
    return pl.pallas_call(
        functools.partial(_ple_kernel, final),
        grid=(t // tm,),
        in_specs=[pl.BlockSpec((tm, dm), lambda i: (i, 0)), pl.BlockSpec((tm, dm), lambda i: (i, 0)),
                  pl.BlockSpec((tm, p.shape[1]), lambda i: (i, 0)),
                  _full(g.shape), _full(wpg.shape), _full(wp.shape), _full(gf.shape)],
        out_specs=pl.BlockSpec((tm, dm), lambda i: (i, 0)),
        out_shape=jax.ShapeDtypeStruct((t, dm), F32),
        compiler_params=_cparams(("parallel",)),
        name="ple",
    )(x, moe, p, g, wpg, wp, gf)


def _rope_table(s):
    half = A_HEAD_DIM // 2
    pos = jnp.arange(s, dtype=jnp.int32)
    freqs = ROPE_THETA ** (-jnp.arange(0, half, 2, dtype=F32) / half)
    ang_r = freqs[:, None] * (pos // GRID_W).astype(F32)[None, :]
    ang_c = freqs[:, None] * (pos % GRID_W).astype(F32)[None, :]
    return jnp.concatenate([jnp.cos(ang_r), jnp.sin(ang_r), jnp.cos(ang_c), jnp.sin(ang_c)], axis=0)


def _t5_bucket(rel):
    nb = REL_BUCKETS // 2
    max_exact = nb // 2
    ret = jnp.where(rel > 0, nb, 0)
    r = jnp.abs(rel)
    rf = jnp.maximum(r, 1).astype(F32)
    large = max_exact + (jnp.log(rf / max_exact) / math.log(REL_MAX_DIST / max_exact)
                         * (nb - max_exact)).astype(jnp.int32)
    large = jnp.minimum(large, nb - 1)
    return ret + jnp.where(r < max_exact, r, large)


def _bias_tables(rel_bias, tq, tk):
    assert tk >= REL_MAX_DIST and tq % tk == 0
    rep = tq // tk
    offs = jnp.arange(-1, rep + 1, dtype=jnp.int32) * tk
    u = jnp.arange(tq + tk, dtype=jnp.int32)
    rel = offs[:, None] + (tk - 1) - u[None, :]
    gband = jnp.transpose(rel_bias[_t5_bucket(rel)], (2, 0, 1))[:, :, None, :].astype(F32) * LOG2E
    far = jnp.array([-REL_MAX_DIST, REL_MAX_DIST], dtype=jnp.int32)
    cfar = jnp.transpose(rel_bias[_t5_bucket(far)], (1, 0))[:, None, :].astype(F32) * LOG2E
    return gband, cfar


def _tile(n, pref):
    t = min(n, pref)
    assert n % t == 0
    return t


def kernel(x, p, rel_bias, g_mix, w_in, g_qnorm, g_knorm, conv_w, conv_b, dt_bias_f, dt_bias_b, a_log_f, a_log_b, d_skip, g_ssm, lambda_q1, lambda_k1, lambda_q2, lambda_k2, g_diff, g_sgu, w_spatial, b_spatial, w_branch, w_branch_gate, w_out, g_moe, w_router, w_exp_gate, w_exp_up, w_exp_down, g_ple, w_ple_gate, w_ple, g_final):
    b, s, dm = x.shape
    depth = w_in.shape[0]
    t = b * s
    cap = EC_CAPACITY * s // N_EXPERTS

    tm_proj = _tile(s, 512)
    tm_prep = _tile(s, 1024)
    tq_a, tk_a = _tile(s, 512), _tile(s, 1024)
    tq_c, tk_c = _tile(s, 512), _tile(s, 512)
    cps = _tile(s // SSM_CHUNK, 4)
    nck = _tile(s // SGU_CHUNK, 8)
    tt_moe = _tile(s, 1024)

    rope_tab = _rope_table(s)
    gband, cfar = _bias_tables(rel_bias, tq_c, tk_c)
    hd = A_HEAD_DIM

    sizes = (A_Q_HEADS * hd, A_KV_HEADS * hd, A_KV_HEADS * hd,
             SSM_HEADS * SSM_HEAD_DIM, SSM_HEADS * SSM_HEAD_DIM, SSM_GROUPS * SSM_STATE,
             SSM_GROUPS * SSM_STATE, SSM_HEADS, SSM_HEADS,
             DIFF_HEADS * 2 * DIFF_QK_DIM, DIFF_HEADS * 2 * DIFF_QK_DIM, DIFF_HEADS * DIFF_V_DIM,
             2 * SGU_GROUPS * SGU_GROUP_DIM)
    off = [0]
    for sz in sizes:
        off.append(off[-1] + sz)
    o_a0, o_b0, o_dt0, o_c0, o_d0, o_end = off[0], off[3], off[7], off[9], off[12], off[13]
    dt_pad = V7X_LANES - 2 * SSM_HEADS

    def lane_row(v, width=None):
        v = v.astype(F32).reshape(1, -1)
        if width is not None and v.shape[1] < width:
            v = jnp.pad(v, ((0, 0), (0, width - v.shape[1])))
        return v

    xt = x.reshape(t, dm)
    for i in range(depth):
        wi = w_in[i]
        wat = wi[:, o_a0:o_b0].T.astype(BF16)
        wb = jnp.pad(wi[:, o_b0:o_c0], ((0, 0), (0, dt_pad))).astype(BF16)
        wct = wi[:, o_c0:o_d0].T.astype(BF16)
        wd = wi[:, o_d0:o_end].astype(BF16)
        gmix = lane_row(g_mix[i])
        col = lambda v: v.astype(F32).reshape(-1, 1)
        (qt, kh, vt, z, xbc, dt, cqt, ckh, cvt, du, dv) = _inproj(
            xt, gmix, wat, wb, wct, wd, rope_tab, col(g_qnorm[i]), col(g_knorm[i]), lane_row(g_sgu[i]),
            b, s, tm_proj)

        o_at = _attn_a(qt.reshape(b, A_Q_HEADS, hd, s), kh, vt, tq_a, tk_a, ATTN_A_INNER).reshape(b, -1, s)

        xa, dl = _ssd_prep(xbc, dt, conv_w[i].astype(F32), lane_row(conv_b[i]),
                           lane_row(jnp.concatenate([dt_bias_f[i], dt_bias_b[i]]), V7X_LANES), s, tm_prep)
        o_b = _ssd(xa, dl, lane_row(jnp.concatenate([a_log_f[i], a_log_b[i]]), V7X_LANES), z,
                   lane_row(jnp.repeat(d_skip[i], SSM_HEAD_DIM)), lane_row(g_ssm[i]), b, s, cps)

        lam_init = 0.8 - 0.6 * math.exp(-0.3 * i)
        lam = (jnp.exp(jnp.sum(lambda_q1[i].astype(F32) * lambda_k1[i].astype(F32)))
               - jnp.exp(jnp.sum(lambda_q2[i].astype(F32) * lambda_k2[i].astype(F32))) + lam_init).reshape(1, 1)
        o_ct = _attn_c(cqt.reshape(b, DIFF_HEADS, 2, DIFF_QK_DIM, s), ckh, cvt, gband, cfar, lam, col(g_diff[i]),
                       1.0 - lam_init, tq_c, tk_c, ATTN_C_INNER).reshape(b, -1, s)

        sgu_bias = jnp.repeat(b_spatial[i].T.astype(F32), SGU_GROUP_DIM, axis=1)
        o_d = _sgu(du, dv, w_spatial[i].astype(BF16), sgu_bias, nck)

        xt, h2, aff = _merge(xt, gmix, o_at, o_b, o_ct, o_d, w_branch_gate[i].astype(BF16),
                             w_branch[i].astype(BF16), w_out[i].astype(BF16), lane_row(g_moe[i]),
                             w_router[i].T.astype(F32), b, s, tm_proj)

        gates, pos = _select(aff, cap)
        cnt = jnp.sum((pos >= 0).reshape(b, N_EXPERTS, s // tt_moe, tt_moe), axis=3, dtype=jnp.int32)
        tile_major = lambda a: jnp.transpose(a, (0, 2, 1)).reshape(t // tt_moe, N_EXPERTS)
        moe = _moe(h2, pos.reshape(b * N_EXPERTS, 1, s), gates.reshape(b * N_EXPERTS, 1, s), tile_major(cnt),
                   tile_major(jnp.cumsum(cnt, axis=2) - cnt), w_exp_gate[i].astype(BF16), w_exp_up[i].astype(BF16), w_exp_down[i].astype(BF16),
                   tt_moe, _tile(s // tt_moe, 2), MOE_ROW_BLOCK)

        xt = _ple(xt, moe, p[i].reshape(t, -1), lane_row(g_ple[i]), w_ple_gate[i].astype(BF16),
                  w_ple[i].astype(BF16), lane_row(g_final), i == depth - 1, tm_proj)

    return xt.reshape(b, s, dm)
```

```python
import functools
import math

import jax
import jax.numpy as jnp
from jax import lax
from jax.experimental import pallas as pl
from jax.experimental.pallas import tpu as pltpu

F32 = jnp.float32
BF16 = jnp.bfloat16
HIGHEST = lax.Precision.HIGHEST

EPS = 1e-6
GRID_W = 64
A_Q_HEADS, A_KV_HEADS, A_HEAD_DIM = 4, 2, 64
ROPE_THETA = 10000.0
SSM_HEADS, SSM_HEAD_DIM, SSM_GROUPS, SSM_STATE, SSM_CONV, SSM_CHUNK = 4, 64, 2, 128, 5, 128
DIFF_HEADS, DIFF_QK_DIM, DIFF_V_DIM = 4, 32, 64
REL_BUCKETS, REL_MAX_DIST = 32, 128
SGU_GROUPS, SGU_GROUP_DIM, SGU_CHUNK = 4, 64, 128
N_EXPERTS, EC_CAPACITY = 16, 2
N_BRANCHES, BRANCH_WIDTH = 4, 256

V7X_LANES = 128
V7X_SUBLANES = 8
BF16_SUBLANE_PACK = 16
V7X_VMEM_LIMIT_BYTES = 56 * 1024 * 1024

NEG_BIG = -1e30
LOG2E = math.log2(math.e)
MOE_ROW_BLOCK = 160
ATTN_A_RING, ATTN_A_INNER = 3, 2
ATTN_C_RING, ATTN_C_INNER = 2, 7


def _cparams(sem):
    return pltpu.CompilerParams(dimension_semantics=sem, vmem_limit_bytes=V7X_VMEM_LIMIT_BYTES)


def _rms(x, g):
    return x * lax.rsqrt(jnp.mean(x * x, axis=-1, keepdims=True) + EPS) * g


def _dot(a, b):
    return jnp.dot(a, b, preferred_element_type=F32)


def _dot_nt(a, b):
    return lax.dot_general(a, b, (((1,), (1,)), ((), ())), preferred_element_type=F32)


def _dot_tn(a, b):
    return lax.dot_general(a, b, (((0,), (0,)), ((), ())), preferred_element_type=F32)


def _full(shape):
    n = len(shape)
    return pl.BlockSpec(shape, lambda *_: (0,) * n)


def _norm_rope_t(x, g_col, tab):
    xn = x * lax.rsqrt(jnp.mean(x * x, axis=0, keepdims=True) + EPS) * g_col
    qd = A_HEAD_DIM // 4
    cr, sr, cc, sc = (tab[i * qd:(i + 1) * qd] for i in range(4))
    r1, r2, c1, c2 = (xn[i * qd:(i + 1) * qd] for i in range(4))
    return jnp.concatenate([r1 * cr - r2 * sr, r1 * sr + r2 * cr, c1 * cc - c2 * sc, c1 * sc + c2 * cc], axis=0)


def _inproj_kernel(x_ref, gmix_ref, wat_ref, wb_ref, wct_ref, wd_ref, tab_ref, gq_ref, gk_ref, gsgu_ref,
                   qa_ref, ka_ref, va_ref, z_ref, xbc_ref, dt_ref, cq_ref, ck_ref, cv_ref,
                   du_ref, dv_ref):
    hb = _rms(x_ref[...], gmix_ref[...]).astype(BF16)
    tm = hb.shape[0]
    hd = A_HEAD_DIM
    ones = jnp.ones((BF16_SUBLANE_PACK, tm), BF16)

    at = _dot_nt(wat_ref[...], hb)
    tab = tab_ref[...]
    q = [_norm_rope_t(at[h * hd:(h + 1) * hd], gq_ref[...], tab) for h in range(A_Q_HEADS)]
    qa_ref[...] = (jnp.concatenate(q, axis=0) * (hd ** -0.5 * LOG2E)).astype(BF16)
    k0 = A_Q_HEADS * hd
    v0 = k0 + A_KV_HEADS * hd
    for g in range(A_KV_HEADS):
        kt = _norm_rope_t(at[k0 + g * hd:k0 + (g + 1) * hd], gk_ref[...], tab)
        ka_ref[g] = jnp.transpose(kt).astype(BF16)
        va_ref[g, :hd] = at[v0 + g * hd:v0 + (g + 1) * hd].astype(BF16)
        va_ref[g, hd:] = ones

    b = _dot(hb, wb_ref[...])
    dx = SSM_HEADS * SSM_HEAD_DIM
    conv_ch = dx + 2 * SSM_GROUPS * SSM_STATE
    z_ref[...] = b[:, :dx]
    xbc_ref[...] = b[:, dx:dx + conv_ch]
    dt_ref[...] = b[:, dx + conv_ch:]

    ct = _dot_nt(wct_ref[...], hb)
    nqc = DIFF_HEADS * 2 * DIFF_QK_DIM
    cq_ref[...] = (ct[:nqc] * (DIFF_QK_DIM ** -0.5 * LOG2E)).astype(BF16)
    ck = jnp.transpose(ct[nqc:2 * nqc]).astype(BF16)
    for h in range(DIFF_HEADS):
        for c in range(2):
            lo = (2 * h + c) * DIFF_QK_DIM
            ck_ref[h, c] = ck[:, lo:lo + DIFF_QK_DIM]
        cv_ref[h, :DIFF_V_DIM] = ct[2 * nqc + h * DIFF_V_DIM:2 * nqc + (h + 1) * DIFF_V_DIM].astype(BF16)
        cv_ref[h, DIFF_V_DIM:] = ones

    d = jax.nn.gelu(_dot(hb, wd_ref[...]))
    w = SGU_GROUPS * SGU_GROUP_DIM
    du_ref[...] = d[:, :w]
    dv_ref[...] = _rms(d[:, w:], gsgu_ref[...]).astype(BF16)


def _inproj(x, gmix, wat, wb, wct, wd, tab, gq, gk, gsgu, b, s, tm):
    t, dm = x.shape
    nst = s // tm
    row = lambda w: pl.BlockSpec((tm, w), lambda i: (i, 0))
    lanes = lambda *lead: pl.BlockSpec((None,) + lead + (tm,), lambda i: (i // nst,) + (0,) * len(lead) + (i % nst,))
    hd, pk = A_HEAD_DIM, BF16_SUBLANE_PACK
    specs_shapes = [
        (lanes(A_Q_HEADS * hd), (b, A_Q_HEADS * hd, s), BF16),
        (pl.BlockSpec((None, A_KV_HEADS, tm, hd), lambda i: (i // nst, 0, i % nst, 0)), (b, A_KV_HEADS, s, hd), BF16),
        (lanes(A_KV_HEADS, hd + pk), (b, A_KV_HEADS, hd + pk, s), BF16),
        (row(256), (t, 256), F32), (row(768), (t, 768), F32), (row(128), (t, 128), F32),
        (lanes(DIFF_HEADS * 2 * DIFF_QK_DIM), (b, DIFF_HEADS * 2 * DIFF_QK_DIM, s), BF16),
        (pl.BlockSpec((None, DIFF_HEADS, 2, tm, DIFF_QK_DIM), lambda i: (i // nst, 0, 0, i % nst, 0)),
         (b, DIFF_HEADS, 2, s, DIFF_QK_DIM), BF16),
        (lanes(DIFF_HEADS, DIFF_V_DIM + pk), (b, DIFF_HEADS, DIFF_V_DIM + pk, s), BF16),
        (row(256), (t, 256), F32), (row(256), (t, 256), BF16)]
    return pl.pallas_call(
        _inproj_kernel,
        grid=(t // tm,),
        in_specs=[row(dm), _full(gmix.shape), _full(wat.shape), _full(wb.shape), _full(wct.shape),
                  _full(wd.shape), pl.BlockSpec((tab.shape[0], tm), lambda i: (0, i % nst)),
                  _full(gq.shape), _full(gk.shape), _full(gsgu.shape)],
        out_specs=[sp for sp, _, _ in specs_shapes],
        out_shape=[jax.ShapeDtypeStruct(shp, dt) for _, shp, dt in specs_shapes],
        compiler_params=_cparams(("parallel",)),
        name="inproj",
    )(x, gmix, wat, wb, wct, wd, tab, gq, gk, gsgu)


def _osm_tile(s, mt, shift, vt, m_ref, acc_ref, c):
    m_old = m_ref[c]
    if shift is not None:
        mt = mt + shift
    m_new = jnp.maximum(m_old, mt)
    p = jnp.exp2(s - (m_new if shift is None else m_new - shift))
    acc_ref[c] = jnp.exp2(m_old - m_new) * acc_ref[c] + _dot(vt, p.astype(BF16))
    m_ref[c] = m_new


def _osm_init(m_ref, acc_ref):
    m_ref[...] = jnp.full(m_ref.shape, NEG_BIG, F32)
    acc_ref[...] = jnp.zeros(acc_ref.shape, F32)


def _osm_result(acc_ref, c, dv):
    acc = acc_ref[c]
    return acc[:dv] / acc[dv:dv + 1]


def _attn_scores(tk, j, qt_ref, k_of, buf, bias_tile=None):
    s_ref, mx_ref = buf
    off = pl.multiple_of(j * tk, tk)
    for c in range(qt_ref.shape[0]):
        s = _dot(k_of(c, off), qt_ref[c])
        if bias_tile is not None:
            s = s + bias_tile
        s_ref[c] = s
        mx_ref[c] = jnp.max(s, axis=0, keepdims=True)


def _attn_values(tk, j, nmaps, vt_ref, buf, m_ref, acc_ref, shift=None):
    s_ref, mx_ref = buf
    vt = vt_ref[:, pl.ds(pl.multiple_of(j * tk, tk), tk)]
    for c in range(nmaps):
        _osm_tile(s_ref[c], mx_ref[c], shift, vt, m_ref, acc_ref, c)


def _pipelined(n, scores, values, bufs, inner):
    nb = len(bufs)
    ahead = nb - 1
    last = jnp.maximum(n - 1, 0)
    for r in range(ahead):
        scores(jnp.minimum(r, last), bufs[r])

    def ring(g):
        for r in range(nb):
            scores(jnp.minimum(g + r + ahead, last), bufs[(r + ahead) % nb])
            values(g + r, bufs[r])

    span = nb * inner

    def unrolled(t, carry):
        for w in range(inner):
            ring(span * t + nb * w)
        return carry

    lax.fori_loop(0, n // span, unrolled, 0)
    base = (n // span) * span

    def rolled(t, carry):
        ring(base + nb * t)
        return carry

    lax.fori_loop(0, (n - base) // nb, rolled, 0)
    done = (n // nb) * nb
    for r in range(ahead):

        @pl.when(n - done > r)
        def _():
            values(done + r, bufs[r])


def _ring_scratch(nring, nmaps, tk, tq):
    return [pltpu.VMEM(shape, F32) for _ in range(nring) for shape in ((nmaps, tk, tq), (nmaps, 1, tq))]


def _ring_bufs(refs):
    return tuple((refs[2 * r], refs[2 * r + 1]) for r in range(len(refs) // 2))


def _attn_a_kernel(tk, inner, qt_ref, k_ref, vt_ref, o_ref, m_ref, acc_ref, *ring_refs):
    _osm_init(m_ref, acc_ref)
    nq = qt_ref.shape[0]
    k_of = lambda c, off: k_ref[pl.ds(off, tk), :]
    _pipelined(k_ref.shape[0] // tk,
               lambda j, buf: _attn_scores(tk, j, qt_ref, k_of, buf),
               lambda j, buf: _attn_values(tk, j, nq, vt_ref, buf, m_ref, acc_ref),
               _ring_bufs(ring_refs), inner)
    for c in range(nq):
        o_ref[c] = _osm_result(acc_ref, c, o_ref.shape[1]).astype(o_ref.dtype)


def _attn_a(qt, k, vt, tq, tk, inner):
    b, hq, d, s = qt.shape
    hkv = k.shape[1]
    rep = hq // hkv
    da = vt.shape[2]
    qspec = pl.BlockSpec((None, rep, d, tq), lambda bi, g, i: (bi, g, 0, i))
    return pl.pallas_call(
        functools.partial(_attn_a_kernel, tk, inner),
        grid=(b, hkv, s // tq),
        in_specs=[qspec,
                  pl.BlockSpec((None, None, s, d), lambda bi, g, i: (bi, g, 0, 0)),
                  pl.BlockSpec((None, None, da, s), lambda bi, g, i: (bi, g, 0, 0))],
        out_specs=qspec,
        out_shape=jax.ShapeDtypeStruct(qt.shape, BF16),
        scratch_shapes=[pltpu.VMEM((rep, 1, tq), F32), pltpu.VMEM((rep, da, tq), F32)]
        + _ring_scratch(ATTN_A_RING, rep, tk, tq),
        compiler_params=_cparams(("parallel", "parallel", "arbitrary")),
        name="attn_a",
    )(qt, k, vt)


def _attn_c_kernel(tk, inner, out_scale, qt_ref, k_ref, vt_ref, gband_ref, cfar_ref, lam_ref, gd_ref, o_ref,
                   m_ref, acc_ref, *ring_refs):
    i = pl.program_id(2)
    tq = qt_ref.shape[2]
    nk = k_ref.shape[1] // tk
    nband = gband_ref.shape[0]
    dv = o_ref.shape[0]
    _osm_init(m_ref, acc_ref)
    k_of = lambda c, off: k_ref[c, pl.ds(off, tk), :]

    j_lo = (tq // tk) * i - 1
    n_left = jnp.clip(j_lo, 0, nk)
    j_hi = jnp.clip(j_lo + nband, 0, nk)
    n_far = n_left + nk - j_hi
    cf = cfar_ref[...]

    def far_tile(t):
        j = jnp.where(t < n_left, t, t - n_left + j_hi)
        return jnp.clip(j, 0, nk - 1), jnp.where(t < n_left, cf[:, 0:1], cf[:, 1:2])

    def far_scores(t, buf):
        _attn_scores(tk, far_tile(t)[0], qt_ref, k_of, buf)

    def far_values(t, buf):
        j, sh = far_tile(t)
        _attn_values(tk, j, 2, vt_ref, buf, m_ref, acc_ref, shift=sh)

    bufs = _ring_bufs(ring_refs)
    _pipelined(n_far, far_scores, far_values, bufs, inner)

    d_lo = n_left - j_lo

    def band_scores(t, buf):
        d = jnp.clip(d_lo + t, 0, nband - 1)
        g = gband_ref[d]
        tile = pltpu.roll(jnp.broadcast_to(g, (tk, g.shape[1])), tq + 1, 1, stride=1, stride_axis=0)
        _attn_scores(tk, jnp.clip(j_lo + d, 0, nk - 1), qt_ref, k_of, buf, bias_tile=tile[:, :tq])

    def band_values(t, buf):
        _attn_values(tk, j_lo + d_lo + t, 2, vt_ref, buf, m_ref, acc_ref)

    _pipelined(j_hi - n_left, band_scores, band_values, bufs, 1)

    o = _osm_result(acc_ref, 0, dv) - lam_ref[...] * _osm_result(acc_ref, 1, dv)
    o = o * lax.rsqrt(jnp.mean(o * o, axis=0, keepdims=True) + EPS) * gd_ref[...]
    o_ref[...] = (o * out_scale).astype(o_ref.dtype)


def _attn_c(qt, k, vt, gband, cfar, lam, gd, out_scale, tq, tk, inner):
    b, h, _, dqk, s = qt.shape
    dva = vt.shape[2]
    dv = gd.shape[0]
    nband = gband.shape[1]
    return pl.pallas_call(
        functools.partial(_attn_c_kernel, tk, inner, out_scale),
        grid=(b, h, s // tq),
        in_specs=[pl.BlockSpec((None, None, 2, dqk, tq), lambda bi, hi, i: (bi, hi, 0, 0, i)),
                  pl.BlockSpec((None, None, 2, s, dqk), lambda bi, hi, i: (bi, hi, 0, 0, 0)),
                  pl.BlockSpec((None, None, dva, s), lambda bi, hi, i: (bi, hi, 0, 0)),
                  pl.BlockSpec((None, nband, 1, tq + tk), lambda bi, hi, i: (hi, 0, 0, 0)),
                  pl.BlockSpec((None, 1, 2), lambda bi, hi, i: (hi, 0, 0)),
                  _full(lam.shape), _full(gd.shape)],
        out_specs=pl.BlockSpec((None, None, dv, tq), lambda bi, hi, i: (bi, hi, 0, i)),
        out_shape=jax.ShapeDtypeStruct((b, h, dv, s), BF16),
        scratch_shapes=[pltpu.VMEM((2, 1, tq), F32), pltpu.VMEM((2, dva, tq), F32)]
        + _ring_scratch(ATTN_C_RING, 2, tk, tq),
        compiler_params=_cparams(("parallel", "parallel", "arbitrary")),
        name="attn_c",
    )(qt, k, vt, gband, cfar, lam, gd)


def _ssd_prep_kernel(nst, xc_ref, xp_ref, xn_ref, dt_ref, cw_ref, cb_ref, dtb_ref, xa_ref, dl_ref, ext_ref):
    i = pl.program_id(0)
    tm = xc_ref.shape[0]
    halo = V7X_SUBLANES
    pos = i % nst
    keep_prev = (pos != 0).astype(F32)
    keep_next = (pos != nst - 1).astype(F32)
    ext_ref[0:halo, :] = xp_ref[...] * keep_prev
    ext_ref[halo:halo + tm, :] = xc_ref[...]
    ext_ref[halo + tm:, :] = xn_ref[...] * keep_next
    pad_l = (SSM_CONV - 1) // 2
    cw = cw_ref[...]
    acc = jnp.zeros(xc_ref.shape, F32) + cb_ref[...]
    for t in range(SSM_CONV):
        acc = acc + ext_ref[halo - pad_l + t:halo - pad_l + t + tm, :] * cw[t:t + 1, :]
    xa_ref[...] = jax.nn.silu(acc)
    dl_ref[...] = jax.nn.softplus(dt_ref[...] + dtb_ref[...])


def _ssd_prep(xbc, dt, cw, cb, dtb, s, tm):
    t, c = xbc.shape
    nst = s // tm
    hb = tm // V7X_SUBLANES
    nhb = t // V7X_SUBLANES
    return pl.pallas_call(
        functools.partial(_ssd_prep_kernel, nst),
        grid=(t // tm,),
        in_specs=[pl.BlockSpec((tm, c), lambda i: (i, 0)),
                  pl.BlockSpec((V7X_SUBLANES, c), lambda i: (jnp.maximum(i * hb - 1, 0), 0)),
                  pl.BlockSpec((V7X_SUBLANES, c), lambda i: (jnp.minimum((i + 1) * hb, nhb - 1), 0)),
                  pl.BlockSpec((tm, dt.shape[1]), lambda i: (i, 0)),
                  _full(cw.shape), _full(cb.shape), _full(dtb.shape)],
        out_specs=[pl.BlockSpec((tm, c), lambda i: (i, 0)),
                   pl.BlockSpec((tm, dt.shape[1]), lambda i: (i, 0))],
        out_shape=[jax.ShapeDtypeStruct((t, c), F32), jax.ShapeDtypeStruct(dt.shape, F32)],
        scratch_shapes=[pltpu.VMEM((tm + 2 * V7X_SUBLANES, c), F32)],
        compiler_params=_cparams(("parallel",)),
        name="ssd_prep",
    )(xbc, xbc, xbc, dt, cw, cb, dtb)


def _ssd_chunk(reverse, xa, dl, alog, st_ref):
    ln = SSM_CHUNK
    row = lax.broadcasted_iota(jnp.int32, (ln, ln), 0)
    col = lax.broadcasted_iota(jnp.int32, (ln, ln), 1)
    tri = (col >= row) if reverse else (col <= row)
    eye = row == col
    last = 0 if reverse else ln - 1
    lane0 = SSM_HEADS if reverse else 0
    a_neg = -jnp.exp(alog)
    acs = jnp.dot(tri.astype(F32), dl * a_neg, precision=HIGHEST, preferred_element_type=F32)
    dx = SSM_HEADS * SSM_HEAD_DIM
    gw = SSM_STATE
    hpg = SSM_HEADS // SSM_GROUPS
    ys = []
    for g in range(SSM_GROUPS):
        bg = xa[:, dx + g * gw:dx + (g + 1) * gw].astype(BF16)
        cg = xa[:, dx + SSM_GROUPS * gw + g * gw:dx + SSM_GROUPS * gw + (g + 1) * gw].astype(BF16)
        cb = _dot_nt(cg, bg)
        for r in range(hpg):
            h = g * hpg + r
            ln_h = lane0 + h
            ac = acs[:, ln_h:ln_h + 1]
            ar = jnp.sum(jnp.where(eye, ac, 0.0), axis=0, keepdims=True)
            dec = jnp.exp(jnp.where(tri, ac - ar, NEG_BIG))
            xdt = xa[:, h * SSM_HEAD_DIM:(h + 1) * SSM_HEAD_DIM] * dl[:, ln_h:ln_h + 1]
            st = st_ref[h]
            y = _dot((cb * dec).astype(BF16), xdt.astype(BF16))
            y = y + _dot(cg, st.astype(BF16)) * jnp.exp(ac)
            a_last = acs[last:last + 1, ln_h:ln_h + 1]
            st_ref[h] = st * jnp.exp(a_last) + _dot_tn(bg, (xdt * jnp.exp(a_last - ac)).astype(BF16))
            ys.append(y)
    return jnp.concatenate(ys, axis=1)


def _ssd_fwd_kernel(cps, xa_ref, dl_ref, alog_ref, y_ref, st_ref):
    @pl.when(pl.program_id(1) == 0)
    def _():
        st_ref[...] = jnp.zeros(st_ref.shape, F32)

    for c in range(cps):
        sl = slice(c * SSM_CHUNK, (c + 1) * SSM_CHUNK)
        y_ref[sl, :] = _ssd_chunk(False, xa_ref[sl, :], dl_ref[sl, :], alog_ref[...], st_ref)


def _ssd_bwd_kernel(cps, xa_ref, dl_ref, alog_ref, yf_ref, z_ref, dsk_ref, g_ref, o_ref, st_ref):
    @pl.when(pl.program_id(1) == 0)
    def _():
        st_ref[...] = jnp.zeros(st_ref.shape, F32)

    dx = SSM_HEADS * SSM_HEAD_DIM
    for c in reversed(range(cps)):
        sl = slice(c * SSM_CHUNK, (c + 1) * SSM_CHUNK)
        xa = xa_ref[sl, :]
        yb = _ssd_chunk(True, xa, dl_ref[sl, :], alog_ref[...], st_ref)
        y = yf_ref[sl, :] + yb + dsk_ref[...] * xa[:, :dx]
        y = y * jax.nn.silu(z_ref[sl, :])
        o_ref[sl, :] = _rms(y, g_ref[...]).astype(o_ref.dtype)


def _ssd(xa, dl, alog, z, dsk, gssm, b, s, cps):
    t, c = xa.shape
    tm = cps * SSM_CHUNK
    nblk = s // tm
    dx = SSM_HEADS * SSM_HEAD_DIM
    st = pltpu.VMEM((SSM_HEADS, SSM_STATE, SSM_HEAD_DIM), F32)
    fw = lambda w: pl.BlockSpec((tm, w), lambda bi, ci: (bi * nblk + ci, 0))
    bw = lambda w: pl.BlockSpec((tm, w), lambda bi, ci: (bi * nblk + nblk - 1 - ci, 0))
    yf = pl.pallas_call(
        functools.partial(_ssd_fwd_kernel, cps),
        grid=(b, nblk),
        in_specs=[fw(c), fw(dl.shape[1]), _full(alog.shape)],
        out_specs=fw(dx),
        out_shape=jax.ShapeDtypeStruct((t, dx), F32),
        scratch_shapes=[st],
        compiler_params=_cparams(("parallel", "arbitrary")),
        name="ssd_fwd",
    )(xa, dl, alog)
    return pl.pallas_call(
        functools.partial(_ssd_bwd_kernel, cps),
        grid=(b, nblk),
        in_specs=[bw(c), bw(dl.shape[1]), _full(alog.shape), bw(dx), bw(dx), _full(dsk.shape),
                  _full(gssm.shape)],
        out_specs=bw(dx),
        out_shape=jax.ShapeDtypeStruct((t, dx), BF16),
        scratch_shapes=[st],
        compiler_params=_cparams(("parallel", "arbitrary")),
        name="ssd_bwd",
    )(xa, dl, alog, yf, z, dsk, gssm)


def _sgu_kernel(nck, u_ref, v_ref, w_ref, bias_ref, o_ref):
    for c in range(nck):
        sl = slice(c * SGU_CHUNK, (c + 1) * SGU_CHUNK)
        v = v_ref[sl, :]
        sv = [_dot(w_ref[g], v[:, g * SGU_GROUP_DIM:(g + 1) * SGU_GROUP_DIM]) for g in range(SGU_GROUPS)]
        o_ref[sl, :] = (u_ref[sl, :] * (jnp.concatenate(sv, axis=1) + bias_ref[...])).astype(o_ref.dtype)


def _sgu(u, v, w, bias, nck):
    t, c = u.shape
    tm = nck * SGU_CHUNK
    return pl.pallas_call(
        functools.partial(_sgu_kernel, nck),
        grid=(t // tm,),
        in_specs=[pl.BlockSpec((tm, c), lambda i: (i, 0)), pl.BlockSpec((tm, c), lambda i: (i, 0)),
                  _full(w.shape), _full(bias.shape)],
        out_specs=pl.BlockSpec((tm, c), lambda i: (i, 0)),
        out_shape=jax.ShapeDtypeStruct((t, c), BF16),
        compiler_params=_cparams(("parallel",)),
        name="sgu",
    )(u, v, w, bias)


def _merge_kernel(x_ref, gmix_ref, oat_ref, ob_ref, oct_ref, od_ref, wg_ref, wbr_ref, wo_ref, gmoe_ref, wrt_ref,
                  o_ref, h_ref, aff_ref):
    x = x_ref[...]
    dm = x.shape[1]
    hb = _rms(x, gmix_ref[...]).astype(BF16)
    merged = jnp.zeros(x.shape, F32)
    for n, (br, transposed) in enumerate(((oat_ref, True), (ob_ref, False), (oct_ref, True), (od_ref, False))):
        gate = jax.nn.sigmoid(_dot(hb, wg_ref[:, n * dm:(n + 1) * dm]))
        wide = _dot_tn(br[...], wbr_ref[n]) if transposed else _dot(br[...], wbr_ref[n])
        merged = merged + gate * wide
    x = x + _dot(merged.astype(BF16), wo_ref[...])
    o_ref[...] = x

    h = _rms(x, gmoe_ref[...])
    h_ref[...] = h.astype(BF16)
    logits = lax.dot_general(wrt_ref[...], h, (((1,), (1,)), ((), ())), precision=HIGHEST,
                             preferred_element_type=F32)
    e = jnp.exp(logits - jnp.max(logits, axis=0, keepdims=True))
    aff_ref[...] = e / jnp.sum(e, axis=0, keepdims=True)


def _merge(x, gmix, oat, ob, oct, od, wg, wbr, wo, gmoe, wrt, b, s, tm):
    t, dm = x.shape
    ne = wrt.shape[0]
    nst = s // tm
    row = lambda w: pl.BlockSpec((tm, w), lambda i: (i, 0))
    lanes = lambda w: pl.BlockSpec((None, w, tm), lambda i: (i // nst, 0, i % nst))
    return pl.pallas_call(
        _merge_kernel,
        grid=(t // tm,),
        in_specs=[row(dm), _full(gmix.shape), lanes(BRANCH_WIDTH), row(BRANCH_WIDTH), lanes(BRANCH_WIDTH),
                  row(BRANCH_WIDTH), _full(wg.shape), _full(wbr.shape), _full(wo.shape), _full(gmoe.shape),
                  _full(wrt.shape)],
        out_specs=[row(dm), row(dm), lanes(ne)],
        out_shape=[jax.ShapeDtypeStruct((t, dm), F32), jax.ShapeDtypeStruct((t, dm), BF16),
                   jax.ShapeDtypeStruct((b, ne, s), F32)],
        compiler_params=_cparams(("parallel",)),
        name="merge",
    )(x, gmix, oat, ob, oct, od, wg, wbr, wo, gmoe, wrt)


def _select_kernel(cap, aff_ref, gate_ref, pos_ref):
    ne, s = aff_ref.shape
    bits = pltpu.bitcast(aff_ref[...], jnp.int32)
    capf = jnp.float32(cap)

    def bisect(i, thr):
        cand = thr | lax.shift_left(jnp.int32(1), 30 - i)
        cnt = jnp.sum((bits >= cand).astype(F32), axis=1, keepdims=True)
        return jnp.where(cnt >= capf, cand, thr)

    thr = lax.fori_loop(0, 31, bisect, jnp.zeros((ne, 1), jnp.int32))
    need = capf - jnp.sum((bits > thr).astype(F32), axis=1, keepdims=True)
    lanes = V7X_LANES
    r = lax.broadcasted_iota(jnp.int32, (lanes, lanes), 0)
    c = lax.broadcasted_iota(jnp.int32, (lanes, lanes), 1)
    before = (r < c).astype(BF16)

    nblk = s // lanes
    group = 4 if nblk % 4 == 0 else 1

    def step(jg, carry):
        n_eq, n_keep = carry
        blocks = []
        for u in range(group):
            off = pl.multiple_of((jg * group + u) * lanes, lanes)
            a = aff_ref[:, pl.ds(off, lanes)]
            bb = pltpu.bitcast(a, jnp.int32)
            eq = bb == thr
            eqf = eq.astype(F32)
            blocks.append((off, a, bb, eq, _dot(eqf.astype(BF16), before), jnp.sum(eqf, axis=1, keepdims=True)))
        for off, a, bb, eq, eq_before, eq_total in blocks:
            keep = jnp.logical_or(bb > thr, jnp.logical_and(eq, n_eq + eq_before < need))
            keepf = keep.astype(F32)
            gate_ref[:, pl.ds(off, lanes)] = jnp.where(keep, a, 0.0)
            pos_ref[:, pl.ds(off, lanes)] = jnp.where(keep, n_keep + _dot(keepf.astype(BF16), before), -1.0)
            n_eq = n_eq + eq_total
            n_keep = n_keep + jnp.sum(keepf, axis=1, keepdims=True)
        return n_eq, n_keep

    zero = jnp.zeros((ne, 1), F32)
    lax.fori_loop(0, nblk // group, step, (zero, zero))


def _select(aff, cap):
    b, ne, s = aff.shape
    spec = pl.BlockSpec((None, ne, s), lambda i: (i, 0, 0))
    return pl.pallas_call(
        functools.partial(_select_kernel, cap),
        grid=(b,),
        in_specs=[spec],
        out_specs=[spec, spec],
        out_shape=[jax.ShapeDtypeStruct(aff.shape, F32), jax.ShapeDtypeStruct(aff.shape, F32)],
        compiler_params=_cparams(("parallel",)),
        name="moe_select",
    )(aff)


def _moe_kernel(rb, tt, cnt_ref, base_ref, h_ref, pos_ref, gate_ref, wg_ref, wu_ref, wd_ref, o_ref):
    i = pl.program_id(0)
    e = pl.program_id(1)
    nsub = h_ref.shape[0] // tt

    @pl.when(e == 0)
    def _():
        o_ref[...] = jnp.zeros(o_ref.shape, F32)

    for sub in range(nsub):
        rows = slice(sub * tt, (sub + 1) * tt)
        tile = i * nsub + sub
        pos = pos_ref[:, rows] - base_ref[tile, e].astype(F32)
        gate = gate_ref[:, rows]

        def block(bi, carry):
            slot = lax.broadcasted_iota(jnp.int32, (rb, 1), 0).astype(F32) + (bi * rb).astype(F32)
            hit = pos == slot
            onehot = hit.astype(BF16)
            xc = _dot(onehot, h_ref[rows, :]).astype(BF16)
            hid = jax.nn.silu(_dot(xc, wg_ref[...])) * _dot(xc, wu_ref[...])
            y = _dot(hid.astype(BF16), wd_ref[...])
            g = jnp.sum(jnp.where(hit, gate, 0.0), axis=1, keepdims=True)
            o_ref[rows, :] += _dot_tn(onehot, (y * g).astype(BF16))
            return carry

        lax.fori_loop(0, (cnt_ref[tile, e] + rb - 1) // rb, block, 0)


def _moe(h, pos, gates, cnt, base, wg, wu, wd, tt, nsub, rb):
    t, dm = h.shape
    ne, _, dff = wg.shape
    ts = tt * nsub
    nst = pos.shape[2] // ts
    lane_row = pl.BlockSpec((None, 1, ts), lambda i, e, *_: ((i // nst) * ne + e, 0, i % nst))
    return pl.pallas_call(
        functools.partial(_moe_kernel, rb, tt),
        grid_spec=pltpu.PrefetchScalarGridSpec(
            num_scalar_prefetch=2,
            grid=(t // ts, ne),
            in_specs=[pl.BlockSpec((ts, dm), lambda i, e, *_: (i, 0)), lane_row, lane_row,
                      pl.BlockSpec((None, dm, dff), lambda i, e, *_: (e, 0, 0)),
                      pl.BlockSpec((None, dm, dff), lambda i, e, *_: (e, 0, 0)),
                      pl.BlockSpec((None, dff, dm), lambda i, e, *_: (e, 0, 0))],
            out_specs=pl.BlockSpec((ts, dm), lambda i, e, *_: (i, 0))),
        out_shape=jax.ShapeDtypeStruct((t, dm), F32),
        compiler_params=_cparams(("parallel", "arbitrary")),
        name="moe_ffn",
    )(cnt, base, h, pos, gates, wg, wu, wd)


def _ple_kernel(final, x_ref, moe_ref, p_ref, g_ref, wpg_ref, wp_ref, gf_ref, o_ref):
    x = x_ref[...] + moe_ref[...]
    gate = jax.nn.sigmoid(_dot(_rms(x, g_ref[...]).astype(BF16), wpg_ref[...]))
    y = x + gate * _dot(p_ref[...].astype(BF16), wp_ref[...])
    o_ref[...] = _rms(y, gf_ref[...]) if final else y


def _ple(x, moe, p, g, wpg, wp, gf, final, tm):
    t, dm = x.shape
    return pl.pallas_call(
        functools.partial(_ple_kernel, final),
        grid=(t // tm,),
        in_specs=[pl.BlockSpec((tm, dm), lambda i: (i, 0)), pl.BlockSpec((tm, dm), lambda i: (i, 0)),
                  pl.BlockSpec((tm, p.shape[1]), lambda i: (i, 0)),
                  _full(g.shape), _full(wpg.shape), _full(wp.shape), _full(gf.shape)],
        out_specs=pl.BlockSpec((tm, dm), lambda i: (i, 0)),
        out_shape=jax.ShapeDtypeStruct((t, dm), F32),
        compiler_params=_cparams(("parallel",)),
        name="ple",
    )(x, moe, p, g, wpg, wp, gf)


def _rope_table(s):
    half = A_HEAD_DIM // 2
    pos = jnp.arange(s, dtype=jnp.int32)
    freqs = ROPE_THETA ** (-jnp.arange(0, half, 2, dtype=F32) / half)
    ang_r = freqs[:, None] * (pos // GRID_W).astype(F32)[None, :]
    ang_c = freqs[:, None] * (pos % GRID_W).astype(F32)[None, :]
    return jnp.concatenate([jnp.cos(ang_r), jnp.sin(ang_r), jnp.cos(ang_c), jnp.sin(ang_c)], axis=0)


def _t5_bucket(rel):
    nb = REL_BUCKETS // 2
    max_exact = nb // 2
    ret = jnp.where(rel > 0, nb, 0)
    r = jnp.abs(rel)
    rf = jnp.maximum(r, 1).astype(F32)
    large = max_exact + (jnp.log(rf / max_exact) / math.log(REL_MAX_DIST / max_exact)
                         * (nb - max_exact)).astype(jnp.int32)
    large = jnp.minimum(large, nb - 1)
    return ret + jnp.where(r < max_exact, r, large)


def _bias_tables(rel_bias, tq, tk):
    assert tk >= REL_MAX_DIST and tq % tk == 0
    rep = tq // tk
    offs = jnp.arange(-1, rep + 1, dtype=jnp.int32) * tk
    u = jnp.arange(tq + tk, dtype=jnp.int32)
    rel = offs[:, None] + (tk - 1) - u[None, :]
    gband = jnp.transpose(rel_bias[_t5_bucket(rel)], (2, 0, 1))[:, :, None, :].astype(F32) * LOG2E
    far = jnp.array([-REL_MAX_DIST, REL_MAX_DIST], dtype=jnp.int32)
    cfar = jnp.transpose(rel_bias[_t5_bucket(far)], (1, 0))[:, None, :].astype(F32) * LOG2E
    return gband, cfar


def _tile(n, pref):
    t = min(n, pref)
    assert n % t == 0
    return t


def kernel(x, p, rel_bias, g_mix, w_in, g_qnorm, g_knorm, conv_w, conv_b, dt_bias_f, dt_bias_b, a_log_f, a_log_b, d_skip, g_ssm, lambda_q1, lambda_k1, lambda_q2, lambda_k2, g_diff, g_sgu, w_spatial, b_spatial, w_branch, w_branch_gate, w_out, g_moe, w_router, w_exp_gate, w_exp_up, w_exp_down, g_ple, w_ple_gate, w_ple, g_final):
    b, s, dm = x.shape
    depth = w_in.shape[0]
    t = b * s
    cap = EC_CAPACITY * s // N_EXPERTS

    tm_proj = _tile(s, 512)
    tm_ple = _tile(s, 1024)
    tm_prep = _tile(s, 1024)
    tq_a, tk_a = _tile(s, 512), _tile(s, 1024)
    tq_c, tk_c = _tile(s, 512), _tile(s, 512)
    cps = _tile(s // SSM_CHUNK, 8)
    nck = _tile(s // SGU_CHUNK, 8)
    tt_moe = _tile(s, 1024)

    rope_tab = _rope_table(s)
    gband, cfar = _bias_tables(rel_bias, tq_c, tk_c)
    hd = A_HEAD_DIM

    sizes = (A_Q_HEADS * hd, A_KV_HEADS * hd, A_KV_HEADS * hd,
             SSM_HEADS * SSM_HEAD_DIM, SSM_HEADS * SSM_HEAD_DIM, SSM_GROUPS * SSM_STATE,
             SSM_GROUPS * SSM_STATE, SSM_HEADS, SSM_HEADS,
             DIFF_HEADS * 2 * DIFF_QK_DIM, DIFF_HEADS * 2 * DIFF_QK_DIM, DIFF_HEADS * DIFF_V_DIM,
             2 * SGU_GROUPS * SGU_GROUP_DIM)
    off = [0]
    for sz in sizes:
        off.append(off[-1] + sz)
    o_a0, o_b0, o_dt0, o_c0, o_d0, o_end = off[0], off[3], off[7], off[9], off[12], off[13]
    dt_pad = V7X_LANES - 2 * SSM_HEADS

    def lane_row(v, width=None):
        v = v.astype(F32).reshape(1, -1)
        if width is not None and v.shape[1] < width:
            v = jnp.pad(v, ((0, 0), (0, width - v.shape[1])))
        return v

    xt = x.reshape(t, dm)
    for i in range(depth):
        wi = w_in[i]
        wat = wi[:, o_a0:o_b0].T.astype(BF16)
        wb = jnp.pad(wi[:, o_b0:o_c0], ((0, 0), (0, dt_pad))).astype(BF16)
        wct = wi[:, o_c0:o_d0].T.astype(BF16)
        wd = wi[:, o_d0:o_end].astype(BF16)
        gmix = lane_row(g_mix[i])
        col = lambda v: v.astype(F32).reshape(-1, 1)
        (qt, kh, vt, z, xbc, dt, cqt, ckh, cvt, du, dv) = _inproj(
            xt, gmix, wat, wb, wct, wd, rope_tab, col(g_qnorm[i]), col(g_knorm[i]), lane_row(g_sgu[i]),
            b, s, tm_proj)

        o_at = _attn_a(qt.reshape(b, A_Q_HEADS, hd, s), kh, vt, tq_a, tk_a, ATTN_A_INNER).reshape(b, -1, s)

        xa, dl = _ssd_prep(xbc, dt, conv_w[i].astype(F32), lane_row(conv_b[i]),
                           lane_row(jnp.concatenate([dt_bias_f[i], dt_bias_b[i]]), V7X_LANES), s, tm_prep)
        o_b = _ssd(xa, dl, lane_row(jnp.concatenate([a_log_f[i], a_log_b[i]]), V7X_LANES), z,
                   lane_row(jnp.repeat(d_skip[i], SSM_HEAD_DIM)), lane_row(g_ssm[i]), b, s, cps)

        lam_init = 0.8 - 0.6 * math.exp(-0.3 * i)
        lam = (jnp.exp(jnp.sum(lambda_q1[i].astype(F32) * lambda_k1[i].astype(F32)))
               - jnp.exp(jnp.sum(lambda_q2[i].astype(F32) * lambda_k2[i].astype(F32))) + lam_init).reshape(1, 1)
        o_ct = _attn_c(cqt.reshape(b, DIFF_HEADS, 2, DIFF_QK_DIM, s), ckh, cvt, gband, cfar, lam, col(g_diff[i]),
                       1.0 - lam_init, tq_c, tk_c, ATTN_C_INNER).reshape(b, -1, s)

        sgu_bias = jnp.repeat(b_spatial[i].T.astype(F32), SGU_GROUP_DIM, axis=1)
        o_d = _sgu(du, dv, w_spatial[i].astype(BF16), sgu_bias, nck)

        xt, h2, aff = _merge(xt, gmix, o_at, o_b, o_ct, o_d, w_branch_gate[i].astype(BF16),
                             w_branch[i].astype(BF16), w_out[i].astype(BF16), lane_row(g_moe[i]),
                             w_router[i].T.astype(F32), b, s, tm_proj)

        gates, pos = _select(aff, cap)
        cnt = jnp.sum((pos >= 0).reshape(b, N_EXPERTS, s // tt_moe, tt_moe), axis=3, dtype=jnp.int32)
        tile_major = lambda a: jnp.transpose(a, (0, 2, 1)).reshape(t // tt_moe, N_EXPERTS)
        moe = _moe(h2, pos.reshape(b * N_EXPERTS, 1, s), gates.reshape(b * N_EXPERTS, 1, s), tile_major(cnt),
                   tile_major(jnp.cumsum(cnt, axis=2) - cnt), w_exp_gate[i].astype(BF16), w_exp_up[i].astype(BF16), w_exp_down[i].astype(BF16),
                   tt_moe, _tile(s // tt_moe, 2), MOE_ROW_BLOCK)

        xt = _ple(xt, moe, p[i].reshape(t, -1), lane_row(g_ple[i]), w_ple_gate[i].astype(BF16),
                  w_ple[i].astype(BF16), lane_row(g_final), i == depth - 1, tm_ple)

    return xt.reshape(b, s, dm)
```
